```python
import math
import jax, jax.numpy as jnp
from jax import lax
import numpy as np

D_MODEL = 1024
BATCH = 4
SEQ = 4096
DEPTH = 4

N_ATTN_HEADS = 8
HEAD_DIM = 64
ATTN_WIDTH = N_ATTN_HEADS * HEAD_DIM
CONV_WIDTH = D_MODEL - ATTN_WIDTH
CONV_GROUPS = 8
MIX_WIDTH = ATTN_WIDTH + CONV_WIDTH
CONV_KERNEL = 31
IDX_HEADS = 8
IDX_DIM = 64
TOPK_KEYS = 256
ROPE_THETA = 500000.0
ROPE_DIM = HEAD_DIM // 4
Q_BLOCK = 128
PEER_HEADS = 8
PEER_KEYS = 128
PEER_EXPERTS = PEER_KEYS * PEER_KEYS
PEER_QDIM = 256
PEER_TOPK = 16
PEER_BLOCK = 128
NORM_EPS = 1e-6
IN_SIZES = (ATTN_WIDTH, ATTN_WIDTH, ATTN_WIDTH, IDX_HEADS * IDX_DIM, IDX_DIM, IDX_HEADS, 2 * CONV_WIDTH)
IN_COLS = 3 * ATTN_WIDTH + IDX_HEADS * IDX_DIM + IDX_DIM + IDX_HEADS + 2 * CONV_WIDTH

kernel_name = "hymba_dsa_conformer_peer_trunk"

F32 = jnp.float32


def rms_norm(x, gain=None, eps=NORM_EPS):
    xf = x.astype(F32)
    y = xf * lax.rsqrt(jnp.mean(xf * xf, axis=-1, keepdims=True) + eps)
    if gain is not None:
        y = y * gain.astype(F32)
    return y.astype(x.dtype)


def layer_norm(x, gain, bias, eps=NORM_EPS):
    xf = x.astype(F32)
    mu = jnp.mean(xf, axis=-1, keepdims=True)
    var = jnp.mean(jnp.square(xf - mu), axis=-1, keepdims=True)
    y = (xf - mu) * lax.rsqrt(var + eps) * gain.astype(F32) + bias.astype(F32)
    return y.astype(x.dtype)


def rope_tables(positions):
    inv_freq = jnp.power(jnp.float32(ROPE_THETA), -jnp.arange(0, ROPE_DIM, 2, dtype=F32) / ROPE_DIM)
    ang = positions.astype(F32)[..., None] * inv_freq
    return jnp.cos(ang)[:, :, None, :], jnp.sin(ang)[:, :, None, :]


def partial_rope(x, cos, sin):
    half = ROPE_DIM // 2
    x1, x2, xp = x[..., :half], x[..., half:ROPE_DIM], x[..., ROPE_DIM:]
    c = cos.astype(x.dtype)
    s = sin.astype(x.dtype)
    return jnp.concatenate([x1 * c - x2 * s, x2 * c + x1 * s, xp], axis=-1)


def split_offsets():
    return [int(v) for v in np.cumsum(np.array(IN_SIZES))[:-1]]


def dsa_attention(q, k, v, q_idx, k_idx, w_idx):
    B, T = q.shape[0], q.shape[1]
    n_sel = min(TOPK_KEYS, T // 4)
    nb = T // Q_BLOCK

    def to_blocks(a):
        return jnp.swapaxes(a.reshape((B, nb, Q_BLOCK) + a.shape[2:]), 0, 1)

    starts = jnp.arange(nb, dtype=jnp.int32) * Q_BLOCK
    kpos = jnp.arange(T, dtype=jnp.int32)
    idx_scale = IDX_DIM ** -0.5
    w_scale = IDX_HEADS ** -0.5
    attn_scale = HEAD_DIM ** -0.5
    k_idx_f = k_idx.astype(F32)

    def block(args):
        qb, qib, wb, start = args
        qpos = start + jnp.arange(Q_BLOCK, dtype=jnp.int32)
        causal = kpos[None, :] <= qpos[:, None]
        dots = jnp.einsum('bqhd,bkd->bqhk', qib.astype(F32), k_idx_f) * idx_scale
        score = jnp.einsum('bqh,bqhk->bqk', wb.astype(F32) * w_scale, jax.nn.relu(dots))
        score = jnp.where(causal[None], score, -jnp.inf)
        _, sel = lax.top_k(score, n_sel)
        valid = sel <= qpos[None, :, None]
        kg = jax.vmap(lambda kb, ib: kb[ib])(k, sel)
        vg = jax.vmap(lambda vb, ib: vb[ib])(v, sel)
        logits = jnp.einsum('bqhd,bqnhd->bhqn', qb.astype(F32), kg.astype(F32)) * attn_scale
        logits = jnp.where(valid[:, None], logits, -jnp.inf)
        p = jax.nn.softmax(logits, axis=-1)
        o = jnp.einsum('bhqn,bqnhd->bqhd', p, vg.astype(F32))
        return o.astype(q.dtype)

    out = lax.map(block, (to_blocks(q), to_blocks(q_idx), to_blocks(w_idx), starts))
    return jnp.swapaxes(out, 0, 1).reshape(B, T, N_ATTN_HEADS, HEAD_DIM)


def conformer_conv(u, w_dw, b_dw, ln_g, ln_b):
    a, g = jnp.split(u, 2, axis=-1)
    h = a * jax.nn.sigmoid(g)
    h = lax.conv_general_dilated(
        h, w_dw[:, None, :].astype(h.dtype), window_strides=(1,),
        padding=((CONV_KERNEL - 1, 0),), dimension_numbers=('NWC', 'WIO', 'NWC'),
        feature_group_count=CONV_WIDTH) + b_dw.astype(h.dtype)
    h = layer_norm(h, ln_g, ln_b)
    return jax.nn.silu(h)


def peer_ffn(h, w_q, sub_keys, exp_u, exp_v):
    B, T, D = h.shape
    nb = T // PEER_BLOCK
    half = PEER_QDIM // 2
    hb = jnp.swapaxes(h.reshape(B, nb, PEER_BLOCK, D), 0, 1)

    def block(xb):
        q = (xb @ w_q).reshape(B, PEER_BLOCK, PEER_HEADS, 2, half)
        s = jnp.einsum('bthpd,hpnd->bthpn', q.astype(F32), sub_keys.astype(F32))
        s1, i1 = lax.top_k(s[..., 0, :], PEER_TOPK)
        s2, i2 = lax.top_k(s[..., 1, :], PEER_TOPK)
        cand = (s1[..., :, None] + s2[..., None, :]).reshape(B, PEER_BLOCK, PEER_HEADS, PEER_TOPK * PEER_TOPK)
        cand_idx = (i1[..., :, None] * PEER_KEYS + i2[..., None, :]).reshape(B, PEER_BLOCK, PEER_HEADS, PEER_TOPK * PEER_TOPK)
        top_s, pos = lax.top_k(cand, PEER_TOPK)
        e = jnp.take_along_axis(cand_idx, pos, axis=-1)
        g = jax.nn.softmax(top_s, axis=-1)
        u = exp_u[e]
        act = jax.nn.gelu(jnp.einsum('btd,bthkd->bthk', xb.astype(F32), u.astype(F32)), approximate=False)
        coef = (g * act).astype(xb.dtype)
        vv = exp_v[e]
        return jnp.einsum('bthk,bthkd->btd', coef, vv)

    out = lax.map(block, hb)
    return jnp.swapaxes(out, 0, 1).reshape(B, T, D)


def setup_inputs(seed: int = 0) -> dict:
    key = jax.random.key(seed)
    ks = jax.random.split(key, 24)
    n = jax.random.normal
    L, D = DEPTH, D_MODEL
    x = n(ks[0], (BATCH, SEQ, D), F32)
    c = n(ks[1], (BATCH, D), F32)
    positions = jnp.broadcast_to(jnp.arange(SEQ, dtype=jnp.int32)[None, :], (BATCH, SEQ))
    w_ada = n(ks[2], (L, D, 6 * D), F32) * (0.5 * D ** -0.5)
    b_ada = n(ks[3], (L, 6 * D), F32) * 0.02
    w_in = n(ks[4], (L, D, IN_COLS), F32) * D ** -0.5
    q_norm_g = 1.0 + 0.02 * n(ks[5], (L, HEAD_DIM), F32)
    k_norm_g = 1.0 + 0.02 * n(ks[6], (L, HEAD_DIM), F32)
    idx_k_ln_g = 1.0 + 0.02 * n(ks[7], (L, IDX_DIM), F32)
    idx_k_ln_b = 0.02 * n(ks[8], (L, IDX_DIM), F32)
    conv_dw = n(ks[9], (L, CONV_KERNEL, CONV_WIDTH), F32) * CONV_KERNEL ** -0.5
    conv_b = 0.02 * n(ks[10], (L, CONV_WIDTH), F32)
    conv_ln_g = 1.0 + 0.02 * n(ks[11], (L, CONV_WIDTH), F32)
    conv_ln_b = 0.02 * n(ks[12], (L, CONV_WIDTH), F32)
    out_g_attn = 1.0 + 0.02 * n(ks[13], (L, ATTN_WIDTH), F32)
    out_g_conv = 1.0 + 0.02 * n(ks[14], (L, CONV_WIDTH), F32)
    w_out = n(ks[15], (L, MIX_WIDTH, D), F32) * MIX_WIDTH ** -0.5
    peer_wq = n(ks[16], (L, D, PEER_HEADS * PEER_QDIM), F32) * D ** -0.5
    peer_sub_keys = n(ks[17], (L, PEER_HEADS, 2, PEER_KEYS, PEER_QDIM // 2), F32) * (PEER_QDIM // 2) ** -0.5
    peer_u = n(ks[18], (L, PEER_EXPERTS, D), F32) * D ** -0.5
    peer_v = n(ks[19], (L, PEER_EXPERTS, D), F32) * PEER_HEADS ** -0.5
    return {"x": x, "c": c, "positions": positions, "w_ada": w_ada, "b_ada": b_ada,
            "w_in": w_in, "q_norm_g": q_norm_g, "k_norm_g": k_norm_g,
            "idx_k_ln_g": idx_k_ln_g, "idx_k_ln_b": idx_k_ln_b, "conv_dw": conv_dw,
            "conv_b": conv_b, "conv_ln_g": conv_ln_g, "conv_ln_b": conv_ln_b,
            "out_g_attn": out_g_attn, "out_g_conv": out_g_conv, "w_out": w_out,
            "peer_wq": peer_wq, "peer_sub_keys": peer_sub_keys, "peer_u": peer_u,
            "peer_v": peer_v}


def reference(x, c, positions, w_ada, b_ada, w_in, q_norm_g, k_norm_g, idx_k_ln_g,
              idx_k_ln_b, conv_dw, conv_b, conv_ln_g, conv_ln_b, out_g_attn, out_g_conv,
              w_out, peer_wq, peer_sub_keys, peer_u, peer_v):
    B, T, D = x.shape
    cos, sin = rope_tables(positions)
    c_act = jax.nn.silu(c)
    offs = split_offsets()
    for l in range(DEPTH):
        mod = (c_act @ w_ada[l] + b_ada[l])[:, None, :]
        sh1, sc1, g1, sh2, sc2, g2 = jnp.split(mod, 6, axis=-1)
        h = rms_norm(x) * (1.0 + sc1) + sh1
        proj = h @ w_in[l]
        q, k, v, qi, ki, wi, cu = jnp.split(proj, offs, axis=-1)
        q = partial_rope(rms_norm(q.reshape(B, T, N_ATTN_HEADS, HEAD_DIM), q_norm_g[l]), cos, sin)
        k = partial_rope(rms_norm(k.reshape(B, T, N_ATTN_HEADS, HEAD_DIM), k_norm_g[l]), cos, sin)
        v = v.reshape(B, T, N_ATTN_HEADS, HEAD_DIM)
        qi = partial_rope(qi.reshape(B, T, IDX_HEADS, IDX_DIM), cos, sin)
        ki = partial_rope(layer_norm(ki, idx_k_ln_g[l], idx_k_ln_b[l])[:, :, None, :], cos, sin)[:, :, 0, :]
        attn = dsa_attention(q, k, v, qi, ki, wi).reshape(B, T, ATTN_WIDTH)
        conv = conformer_conv(cu, conv_dw[l], conv_b[l], conv_ln_g[l], conv_ln_b[l])
        mixed = jnp.concatenate([rms_norm(attn, out_g_attn[l]), rms_norm(conv, out_g_conv[l])], axis=-1)
        x = x + g1 * (mixed @ w_out[l])
        h2 = rms_norm(x) * (1.0 + sc2) + sh2
        x = x + g2 * peer_ffn(h2, peer_wq[l], peer_sub_keys[l], peer_u[l], peer_v[l])
    return x
```

```python
import functools

import jax
import jax.numpy as jnp
from jax import lax
from jax.experimental import pallas as pl
from jax.experimental.pallas import tpu as pltpu

F32 = jnp.float32
BF16 = jnp.bfloat16
I32 = jnp.int32

N_HEADS = 8
HEAD_DIM = 64
ATTN_WIDTH = N_HEADS * HEAD_DIM
CONV_KERNEL = 31
IDX_HEADS = 8
IDX_DIM = 64
TOPK_KEYS = 256
ROPE_THETA = 500000.0
ROPE_DIM = HEAD_DIM // 4
PEER_HEADS = 8
PEER_KEYS = 128
PEER_TOPK = 16
PEER_PAIRS = PEER_HEADS * PEER_TOPK
NORM_EPS = 1e-6

LANES = 128
SUBLANES = 8
VMEM_LIMIT = 56 * 1024 * 1024

NEG_BIG = -1e30
INT_MIN = -2 ** 31


def _cparams(sem):
    return pltpu.CompilerParams(dimension_semantics=sem, vmem_limit_bytes=VMEM_LIMIT)


def _nt_dot(a, b):
    return lax.dot_general(a, b, (((1,), (1,)), ((), ())), preferred_element_type=F32)


def _dot(a, b):
    return jnp.dot(a, b, preferred_element_type=F32)


def _ada_kernel(c_ref, w_ref, b_ref, o_ref):
    c = c_ref[...]
    ca = c * jax.nn.sigmoid(c)
    o_ref[0] = _dot(ca, w_ref[0]) + b_ref[0]


def _ada(c_pad, w_ada, b_ada):
    depth, d, n6 = w_ada.shape
    tn = n6 // 4
    return pl.pallas_call(
        _ada_kernel,
        grid=(depth, n6 // tn),
        in_specs=[
            pl.BlockSpec((SUBLANES, d), lambda l, j: (0, 0)),
            pl.BlockSpec((1, d, tn), lambda l, j: (l, 0, j)),
            pl.BlockSpec((1, 1, tn), lambda l, j: (l, 0, j)),
        ],
        out_specs=pl.BlockSpec((1, SUBLANES, tn), lambda l, j: (l, 0, j)),
        out_shape=jax.ShapeDtypeStruct((depth, SUBLANES, n6), F32),
        compiler_params=_cparams(("arbitrary", "arbitrary")),
        name="adaln",
    )(c_pad, w_ada, b_ada.reshape(depth, 1, n6))


def _rope(x, c, s1, s2):
    w = x.shape[-1]
    half = ROPE_DIM // 2
    return x * c + pltpu.roll(x, w - half, 1) * s1 + pltpu.roll(x, half, 1) * s2


def _split_hi_lo(x):
    hi = x.astype(BF16)
    lo = (x - hi.astype(F32)).astype(BF16)
    return hi, lo


def _inproj_kernel(x_ref, sh_ref, sc_ref, wa_ref, wk_ref, ww_ref, wc_ref, gq_ref, gk_ref,
                   lng_ref, lnb_ref, rc_ref, rs1_ref, rs2_ref, gsum_ref,
                   q_ref, k_ref, v_ref, qi_ref, kk_ref, wi_ref, hg_ref):
    x = x_ref[...]
    ms = jnp.mean(x * x, axis=-1, keepdims=True)
    h = x * lax.rsqrt(ms + NORM_EPS) * (1.0 + sc_ref[0]) + sh_ref[0]
    hb = h.astype(BF16)

    rc1, rs11, rs21 = rc_ref[...], rs1_ref[...], rs2_ref[...]
    reps = ATTN_WIDTH // LANES
    rc = jnp.concatenate([rc1] * reps, axis=1)
    rs1 = jnp.concatenate([rs11] * reps, axis=1)
    rs2 = jnp.concatenate([rs21] * reps, axis=1)
    gsum = gsum_ref[...]

    def head_norm_rope(t, gain):
        hi, lo = _split_hi_lo(t * t)
        ssq = _dot(hi, gsum) + _dot(lo, gsum)
        tn = t * lax.rsqrt(ssq * (1.0 / HEAD_DIM) + NORM_EPS) * gain
        return _rope(tn, rc, rs1, rs2)

    lo_half = lax.broadcasted_iota(I32, (x.shape[0], LANES), 1) < HEAD_DIM

    def one_head_per_group(t):
        parts = []
        for p in range(reps):
            g = t[:, p * LANES:(p + 1) * LANES]
            parts += [jnp.where(lo_half, g, 0.0), jnp.where(lo_half, 0.0, g)]
        return jnp.concatenate(parts, axis=1).astype(BF16)

    w = ATTN_WIDTH
    pa = _dot(hb, wa_ref[...])
    q_ref[...] = one_head_per_group(head_norm_rope(pa[:, 0:w], gq_ref[...]) * (HEAD_DIM ** -0.5))
    k_ref[...] = head_norm_rope(pa[:, w:2 * w], gk_ref[...]).astype(BF16)
    v_ref[...] = pa[:, 2 * w:3 * w].astype(BF16)
    qi_ref[...] = one_head_per_group(_rope(pa[:, 3 * w:4 * w], rc, rs1, rs2) * (IDX_DIM ** -0.5))

    pk = _dot(hb, wk_ref[...])
    mu = jnp.mean(pk, axis=-1, keepdims=True)
    var = jnp.mean(jnp.square(pk - mu), axis=-1, keepdims=True)
    kn = (pk - mu) * lax.rsqrt(var + NORM_EPS) * lng_ref[...] + lnb_ref[...]
    kk_ref[...] = _rope(kn, rc1, rs11, rs21).astype(BF16)

    wi_ref[...] = _dot(hb, ww_ref[...]) * (IDX_HEADS ** -0.5)

    pc = _dot(hb, wc_ref[...])
    cw = pc.shape[1] // 2
    hg_ref[...] = pc[:, :cw] * jax.nn.sigmoid(pc[:, cw:])


def _inproj(x2, sh, sc, wa, wk, ww, wc, gq, gk, lng, lnb, rc, rs1, rs2, gsum, seq, tm):
    n, d = x2.shape
    per_b = seq // tm
    row = lambda i: (i, 0)
    const = lambda i: (0, 0)
    bat = lambda i: (i // per_b, 0, 0)
    cw = wc.shape[1] // 2
    outs = [
        jax.ShapeDtypeStruct((n, 2 * ATTN_WIDTH), BF16),
        jax.ShapeDtypeStruct((n, ATTN_WIDTH), BF16),
        jax.ShapeDtypeStruct((n, ATTN_WIDTH), BF16),
        jax.ShapeDtypeStruct((n, 2 * ATTN_WIDTH), BF16),
        jax.ShapeDtypeStruct((n, LANES), BF16),
        jax.ShapeDtypeStruct((n, LANES), F32),
        jax.ShapeDtypeStruct((n, cw), F32),
    ]
    return pl.pallas_call(
        _inproj_kernel,
        grid=(n // tm,),
        in_specs=[
            pl.BlockSpec((tm, d), row),
            pl.BlockSpec((1, 1, d), bat),
            pl.BlockSpec((1, 1, d), bat),
            pl.BlockSpec(wa.shape, const),
            pl.BlockSpec(wk.shape, const),
            pl.BlockSpec(ww.shape, const),
            pl.BlockSpec(wc.shape, const),
            pl.BlockSpec(gq.shape, const),
            pl.BlockSpec(gk.shape, const),
            pl.BlockSpec(lng.shape, const),
            pl.BlockSpec(lnb.shape, const),
            pl.BlockSpec((tm, LANES), row),
            pl.BlockSpec((tm, LANES), row),
            pl.BlockSpec((tm, LANES), row),
            pl.BlockSpec(gsum.shape, const),
        ],
        out_specs=[pl.BlockSpec((tm, o.shape[1]), row) for o in outs],
        out_shape=outs,
        compiler_params=_cparams(("arbitrary",)),
        name="inproj",
    )(x2, sh, sc, wa, wk, ww, wc, gq, gk, lng, lnb, rc, rs1, rs2, gsum)


CONV_HALO = 32
CONV_ROWS = 64


def _conv_kernel(cur_ref, halo_ref, w_ref, b_ref, lg_ref, lb_ref, og_ref, o_ref, ext_ref, *, tt):
    i = pl.program_id(1)
    ext_ref[0:CONV_HALO, :] = jnp.where(i > 0, halo_ref[0], 0.0)
    ext_ref[CONV_HALO:CONV_HALO + tt, :] = cur_ref[0]
    lead = CONV_HALO - (CONV_KERNEL - 1)
    for r0 in range(0, tt, CONV_ROWS):
        acc = jnp.zeros((CONV_ROWS, ext_ref.shape[1]), F32)
        for j in range(CONV_KERNEL):
            acc = acc + ext_ref[r0 + lead + j:r0 + lead + j + CONV_ROWS, :] * w_ref[j:j + 1, :]
        hcv = acc + b_ref[...]
        mu = jnp.mean(hcv, axis=-1, keepdims=True)
        var = jnp.mean(jnp.square(hcv - mu), axis=-1, keepdims=True)
        y = (hcv - mu) * lax.rsqrt(var + NORM_EPS) * lg_ref[...] + lb_ref[...]
        y = y * jax.nn.sigmoid(y)
        ms = jnp.mean(y * y, axis=-1, keepdims=True)
        o_ref[0, r0:r0 + CONV_ROWS, :] = (y * lax.rsqrt(ms + NORM_EPS) * og_ref[...]).astype(BF16)


def _conv(hg3, w_pad, b, lg, lb, og, tt):
    bsz, seq, cw = hg3.shape
    hb = tt // CONV_HALO
    const = lambda bb, i: (0, 0)
    return pl.pallas_call(
        functools.partial(_conv_kernel, tt=tt),
        grid=(bsz, seq // tt),
        in_specs=[
            pl.BlockSpec((1, tt, cw), lambda bb, i: (bb, i, 0)),
            pl.BlockSpec((1, CONV_HALO, cw), lambda bb, i: (bb, jnp.maximum(i * hb - 1, 0), 0)),
            pl.BlockSpec(w_pad.shape, const),
            pl.BlockSpec(b.shape, const),
            pl.BlockSpec(lg.shape, const),
            pl.BlockSpec(lb.shape, const),
            pl.BlockSpec(og.shape, const),
        ],
        out_specs=pl.BlockSpec((1, tt, cw), lambda bb, i: (bb, i, 0)),
        out_shape=jax.ShapeDtypeStruct((bsz, seq, cw), BF16),
        scratch_shapes=[pltpu.VMEM((CONV_HALO + tt, cw), F32)],
        compiler_params=_cparams(("arbitrary", "arbitrary")),
        name="conv",
    )(hg3, hg3, w_pad, b, lg, lb, og)


def _attn_kernel(q_ref, qi_ref, wi_ref, k_ref, v_ref, kk_ref, og_ref, o_ref,
                 key_ref, bias_ref, m_ref, l_ref, acc_ref, *, tq, n_sel, seq):
    i = pl.program_id(1)
    nk = i + 1
    n_pairs = N_HEADS // 2
    lane = lax.broadcasted_iota(I32, (tq, LANES), 1)
    lo_half = lane < HEAD_DIM
    kcol = lax.broadcasted_iota(I32, (tq, tq), 1)
    qrow = lax.broadcasted_iota(I32, (tq, tq), 0) + i * tq

    def pair_halves(ref, p):
        return ref[0, :, 2 * p * LANES:(2 * p + 1) * LANES], ref[0, :, (2 * p + 1) * LANES:(2 * p + 2) * LANES]

    wi = wi_ref[0]

    def score_chunk(c, carry):
        kic = kk_ref[0, pl.ds(pl.multiple_of(c * tq, tq), tq), :]
        s = jnp.zeros((tq, tq), F32)
        for p in range(n_pairs):
            qa, qb = pair_halves(qi_ref, p)
            s = s + wi[:, 2 * p:2 * p + 1] * jnp.maximum(_nt_dot(qa, kic), 0.0)
            s = s + wi[:, 2 * p + 1:2 * p + 2] * jnp.maximum(_nt_dot(qb, kic), 0.0)
        s = jnp.where(kcol + c * tq <= qrow, s, -jnp.inf)
        bits = pltpu.bitcast(s, I32)
        key = jnp.where(bits < 0, bits ^ jnp.int32(0x7FFFFFFF), bits)
        key_ref[c] = jnp.where(s == 0.0, 0, key)
        return carry

    lax.fori_loop(0, nk, score_chunk, 0)

    ones = jnp.ones((tq, LANES), BF16)
    reps = tq // LANES

    def wide(v):
        return jnp.concatenate([v] * reps, axis=1)

    def count(pred_fn):
        def body(c, acc):
            hit = jnp.where(pred_fn(c, key_ref[c]), 1.0, 0.0).astype(BF16)
            return acc + _dot(hit, ones)
        return lax.fori_loop(0, nk, body, jnp.zeros((tq, LANES), F32))

    def count_ge(thr):
        thr_w = wide(thr)
        return count(lambda c, key: key >= thr_w)

    k_f = jnp.float32(n_sel)
    cand = jnp.where(count_ge(jnp.zeros((tq, LANES), I32)) >= k_f, 0, INT_MIN).astype(I32)

    def bit_step(bi, cand):
        trial = cand + jnp.left_shift(jnp.int32(1), 30 - bi)
        return jnp.where(count_ge(trial) >= k_f, trial, cand)

    tau = lax.fori_loop(0, 31, bit_step, cand)
    n_ge = count_ge(tau)
    n_gt = count_ge(tau + 1)
    need = k_f - n_gt
    tau_w = wide(tau)

    def tie_cut():
        def step(bi, jm):
            trial = jm + jnp.left_shift(jnp.int32(1), (seq.bit_length() - 2) - bi)
            trial_w = wide(trial)
            f = count(lambda c, key: jnp.where(key == tau_w, kcol + c * tq, seq) < trial_w)
            return jnp.where(f < need, trial, jm)
        return lax.fori_loop(0, seq.bit_length() - 1, step, jnp.zeros((tq, LANES), I32))

    has_tie = jnp.max(n_ge) > k_f
    jm = lax.cond(has_tie, tie_cut, lambda: jnp.full((tq, LANES), seq, I32))
    jm_w = wide(jm)

    def bias_chunk(c, carry):
        key = key_ref[c]
        kidx = kcol + c * tq
        tied = jnp.where(kidx <= jm_w, 0.0, NEG_BIG)
        above = jnp.where(key > tau_w, 0.0, NEG_BIG)
        bias_ref[c] = jnp.where(kidx <= qrow, jnp.where(key == tau_w, tied, above), NEG_BIG)
        return carry

    lax.fori_loop(0, nk, bias_chunk, 0)

    m_ref[...] = jnp.full(m_ref.shape, NEG_BIG, F32)
    l_ref[...] = jnp.zeros(l_ref.shape, F32)
    acc_ref[...] = jnp.zeros(acc_ref.shape, F32)

    def attn_chunk(c, carry):
        bias = bias_ref[c]
        rows = pl.ds(pl.multiple_of(c * tq, tq), tq)
        for p in range(n_pairs):
            kp = k_ref[0, rows, p * LANES:(p + 1) * LANES]
            vp = v_ref[0, rows, p * LANES:(p + 1) * LANES]
            halves = pair_halves(q_ref, p)
            pv, alpha = [], []
            for hh in range(2):
                hd = 2 * p + hh
                lm = _nt_dot(halves[hh], kp) + bias
                m_old = m_ref[hd]
                m_new = jnp.maximum(m_old, jnp.max(lm, axis=1, keepdims=True))
                pr = jnp.exp(lm - m_new)
                a = jnp.exp(m_old - m_new)
                l_ref[hd] = a * l_ref[hd] + jnp.sum(pr, axis=1, keepdims=True)
                m_ref[hd] = m_new
                pv.append(_dot(pr.astype(BF16), vp))
                alpha.append(a)
            acc_ref[p] = (acc_ref[p] * jnp.where(lo_half, alpha[0], alpha[1])
                          + jnp.where(lo_half, pv[0], pv[1]))
        return carry

    lax.fori_loop(0, nk, attn_chunk, 0)

    outs = []
    for p in range(n_pairs):
        outs.append(acc_ref[p] / jnp.where(lo_half, l_ref[2 * p], l_ref[2 * p + 1]))
    o = jnp.concatenate(outs, axis=1)
    ms = jnp.mean(o * o, axis=-1, keepdims=True)
    o_ref[0] = (o * lax.rsqrt(ms + NORM_EPS) * og_ref[...]).astype(BF16)


def _attention(q3, k3, v3, qi3, kk3, wi3, og, tq):
    bsz, seq, w = k3.shape
    n_sel = min(TOPK_KEYS, seq // 4)
    nc = seq // tq
    qblk = lambda bb, i: (bb, i, 0)
    full = lambda bb, i: (bb, 0, 0)
    return pl.pallas_call(
        functools.partial(_attn_kernel, tq=tq, n_sel=n_sel, seq=seq),
        grid=(bsz, nc),
        in_specs=[
            pl.BlockSpec((1, tq, 2 * w), qblk),
            pl.BlockSpec((1, tq, 2 * w), qblk),
            pl.BlockSpec((1, tq, LANES), qblk),
            pl.BlockSpec((1, seq, w), full),
            pl.BlockSpec((1, seq, w), full),
            pl.BlockSpec((1, seq, LANES), full),
            pl.BlockSpec(og.shape, lambda bb, i: (0, 0)),
        ],
        out_specs=pl.BlockSpec((1, tq, w), qblk),
        out_shape=jax.ShapeDtypeStruct((bsz, seq, w), BF16),
        scratch_shapes=[
            pltpu.VMEM((nc, tq, tq), I32),
            pltpu.VMEM((nc, tq, tq), F32),
            pltpu.VMEM((N_HEADS, tq, 1), F32),
            pltpu.VMEM((N_HEADS, tq, 1), F32),
            pltpu.VMEM((N_HEADS // 2, tq, LANES), F32),
        ],
        compiler_params=_cparams(("arbitrary", "arbitrary")),
        name="dsa_attention",
    )(q3, qi3, wi3, k3, v3, kk3, og)


def _outproj_kernel(an_ref, cn_ref, x_ref, g1_ref, sh_ref, sc_ref, woa_ref, woc_ref, wq_ref,
                    x1_ref, h2_ref, pq_ref):
    y = _dot(an_ref[...], woa_ref[...]) + _dot(cn_ref[...], woc_ref[...])
    x1 = x_ref[...] + g1_ref[0] * y
    x1_ref[...] = x1
    ms = jnp.mean(x1 * x1, axis=-1, keepdims=True)
    h2 = (x1 * lax.rsqrt(ms + NORM_EPS) * (1.0 + sc_ref[0]) + sh_ref[0]).astype(BF16)
    h2_ref[...] = h2
    pq_ref[...] = _dot(h2, wq_ref[...]).astype(BF16)


def _outproj(an, cn, x2, g1, sh2, sc2, woa, woc, wq, seq, tm):
    n, d = x2.shape
    per_b = seq // tm
    row = lambda i: (i, 0)
    const = lambda i: (0, 0)
    bat = lambda i: (i // per_b, 0, 0)
    return pl.pallas_call(
        _outproj_kernel,
        grid=(n // tm,),
        in_specs=[
            pl.BlockSpec((tm, an.shape[1]), row),
            pl.BlockSpec((tm, cn.shape[1]), row),
            pl.BlockSpec((tm, d), row),
            pl.BlockSpec((1, 1, d), bat),
            pl.BlockSpec((1, 1, d), bat),
            pl.BlockSpec((1, 1, d), bat),
            pl.BlockSpec(woa.shape, const),
            pl.BlockSpec(woc.shape, const),
            pl.BlockSpec(wq.shape, const),
        ],
        out_specs=[pl.BlockSpec((tm, d), row), pl.BlockSpec((tm, d), row),
                   pl.BlockSpec((tm, wq.shape[1]), row)],
        out_shape=[jax.ShapeDtypeStruct((n, d), F32), jax.ShapeDtypeStruct((n, d), BF16),
                   jax.ShapeDtypeStruct((n, wq.shape[1]), BF16)],
        compiler_params=_cparams(("arbitrary",)),
        name="outproj",
    )(an, cn, x2, g1, sh2, sc2, woa, woc, wq)


def _top_rows(s, n_rows, ids=None):
    iota = lax.broadcasted_iota(I32, s.shape, 0)
    vals, picks = [], []
    for _ in range(PEER_TOPK):
        m = jnp.max(s, axis=0, keepdims=True)
        pos = jnp.min(jnp.where(s == m, iota, n_rows), axis=0, keepdims=True)
        hit = iota == pos
        vals.append(m)
        picks.append(pos if ids is None else jnp.max(jnp.where(hit, ids, -1), axis=0, keepdims=True))
        s = jnp.where(hit, -jnp.inf, s)
    return jnp.concatenate(vals, axis=0), jnp.concatenate(picks, axis=0)


def _route_kernel(pq_ref, sk_ref, a_ref, b_ref, g_ref, e_scr, g_scr):
    h = pl.program_id(1)
    half = PEER_KEYS
    s1 = _nt_dot(sk_ref[0, 0], pq_ref[:, :half])
    s2 = _nt_dot(sk_ref[0, 1], pq_ref[:, half:])
    v1, i1 = _top_rows(s1, PEER_KEYS)
    v2, i2 = _top_rows(s2, PEER_KEYS)
    cand = jnp.concatenate([v1[r:r + 1] + v2 for r in range(PEER_TOPK)], axis=0)
    cid = jnp.concatenate([i1[r:r + 1] * PEER_KEYS + i2 for r in range(PEER_TOPK)], axis=0)
    ts, e = _top_rows(cand, PEER_TOPK * PEER_TOPK, cid)
    ex = jnp.exp(ts - ts[0:1])
    g = ex / jnp.sum(ex, axis=0, keepdims=True)
    rows = pl.ds(pl.multiple_of(h * PEER_TOPK, PEER_TOPK), PEER_TOPK)
    e_scr[rows, :] = e
    g_scr[rows, :] = g

    @pl.when(h == PEER_HEADS - 1)
    def _():
        e_all = e_scr[...]
        a_ref[...] = jnp.transpose(jnp.right_shift(e_all, 7).astype(F32))
        b_ref[...] = jnp.transpose(jnp.bitwise_and(e_all, PEER_KEYS - 1).astype(F32))
        g_ref[...] = jnp.transpose(g_scr[...])


def _route(pq, sk, tr):
    n = pq.shape[0]
    qd = 2 * PEER_KEYS
    out = jax.ShapeDtypeStruct((n, PEER_PAIRS), F32)
    oblk = pl.BlockSpec((tr, PEER_PAIRS), lambda i, h: (i, 0))
    return pl.pallas_call(
        _route_kernel,
        grid=(n // tr, PEER_HEADS),
        in_specs=[
            pl.BlockSpec((tr, qd), lambda i, h: (i, h)),
            pl.BlockSpec((1, 2, PEER_KEYS, PEER_KEYS), lambda i, h: (h, 0, 0, 0)),
        ],
        out_specs=[oblk, oblk, oblk],
        out_shape=[out, out, out],
        scratch_shapes=[pltpu.VMEM((PEER_PAIRS, tr), I32), pltpu.VMEM((PEER_PAIRS, tr), F32)],
        compiler_params=_cparams(("arbitrary", "arbitrary")),
        name="peer_route",
    )(pq, sk)


C_PITCH = PEER_KEYS + SUBLANES


def _gelu(x):
    return 0.5 * x * (1.0 + lax.erf(x * (2.0 ** -0.5)))


def _peer_kernel(h2_ref, a_ref, b_ref, g_ref, u_ref, v_ref, x1_ref, g2_ref, o_ref, c_scr, acc_ref,
                 *, tm, ec):
    j = pl.program_id(1)
    nj = pl.num_programs(1)

    @pl.when(j == 0)
    def _():
        srow = lax.broadcasted_iota(I32, (PEER_KEYS, PEER_PAIRS), 0).astype(F32)

        def token(t, carry):
            a_row = a_ref[pl.ds(t, 1), :]
            b_row = b_ref[pl.ds(t, 1), :]
            g_row = g_ref[pl.ds(t, 1), :]
            at = jnp.where(srow == a_row, g_row, 0.0)
            bt = jnp.where(srow == b_row, 1.0, 0.0).astype(BF16)
            hi, lo = _split_hi_lo(at)
            ct = _nt_dot(jnp.concatenate([hi, lo], axis=1), jnp.concatenate([bt, bt], axis=1))
            c_scr[pl.ds(pl.multiple_of(t * C_PITCH, SUBLANES), PEER_KEYS), :] = ct
            return carry

        lax.fori_loop(0, tm, token, 0)
        acc_ref[...] = jnp.zeros(acc_ref.shape, F32)

    act = _gelu(_nt_dot(h2_ref[...], u_ref[...]))
    parts = []
    for ii in range(ec // PEER_KEYS):
        cs = c_scr[pl.ds(j * (ec // PEER_KEYS) + ii, tm, stride=C_PITCH), :]
        parts.append((act[:, ii * PEER_KEYS:(ii + 1) * PEER_KEYS] * cs).astype(BF16))
    acc_ref[...] += _dot(jnp.concatenate(parts, axis=1), v_ref[...])

    @pl.when(j == nj - 1)
    def _():
        o_ref[...] = x1_ref[...] + g2_ref[0] * acc_ref[...]


def _peer(h2, a, b, g, u, v, x1, g2, seq, tm, ec):
    n, d = x1.shape
    ne = u.shape[0]
    per_b = seq // tm
    row = lambda i, j: (i, 0)
    return pl.pallas_call(
        functools.partial(_peer_kernel, tm=tm, ec=ec),
        grid=(n // tm, ne // ec),
        in_specs=[
            pl.BlockSpec((tm, d), row),
            pl.BlockSpec((tm, PEER_PAIRS), row),
            pl.BlockSpec((tm, PEER_PAIRS), row),
            pl.BlockSpec((tm, PEER_PAIRS), row),
            pl.BlockSpec((ec, d), lambda i, j: (j, 0)),
            pl.BlockSpec((ec, d), lambda i, j: (j, 0)),
            pl.BlockSpec((tm, d), row),
            pl.BlockSpec((1, 1, d), lambda i, j: (i // per_b, 0, 0)),
        ],
        out_specs=pl.BlockSpec((tm, d), row),
        out_shape=jax.ShapeDtypeStruct((n, d), F32),
        scratch_shapes=[pltpu.VMEM((tm * C_PITCH, PEER_KEYS), F32), pltpu.VMEM((tm, d), F32)],
        compiler_params=_cparams(("arbitrary", "arbitrary")),
        name="peer_experts",
    )(h2, a, b, g, u, v, x1, g2)


def _rope_lane_tables(positions):
    half = ROPE_DIM // 2
    inv_freq = jnp.power(jnp.float32(ROPE_THETA), -jnp.arange(0, ROPE_DIM, 2, dtype=F32) / ROPE_DIM)
    ang = positions.astype(F32).reshape(-1, 1) * inv_freq
    cos, sin = jnp.cos(ang), jnp.sin(ang)
    j = jnp.arange(LANES) % HEAD_DIM
    cos_l, sin_l = cos[:, j % half], sin[:, j % half]
    rc = jnp.where(j < ROPE_DIM, cos_l, 1.0)
    rs1 = jnp.where(j < half, -sin_l, 0.0)
    rs2 = jnp.where((j >= half) & (j < ROPE_DIM), sin_l, 0.0)
    return rc, rs1, rs2


def _tile_size(total, want):
    t = min(want, total)
    assert total % t == 0
    return t


def kernel(x, c, positions, w_ada, b_ada, w_in, q_norm_g, k_norm_g, idx_k_ln_g, idx_k_ln_b,
           conv_dw, conv_b, conv_ln_g, conv_ln_b, out_g_attn, out_g_conv, w_out, peer_wq,
           peer_sub_keys, peer_u, peer_v):
    bsz, seq, d = x.shape
    depth = w_ada.shape[0]
    n = bsz * seq
    w = ATTN_WIDTH
    cw = d - w
    assert bsz <= SUBLANES and cw % LANES == 0

    tm = _tile_size(seq, 512)
    tq = _tile_size(seq, 256)
    tt = _tile_size(seq, 512)
    tr = _tile_size(seq, 256)
    tp = _tile_size(seq, 256)
    ec = 1024

    c_pad = jnp.zeros((SUBLANES, d), F32).at[:bsz].set(c)
    mod = _ada(c_pad, w_ada, b_ada)[:, :bsz]
    rc, rs1, rs2 = _rope_lane_tables(positions)
    hd_id = jnp.arange(w) // HEAD_DIM
    gsum = (hd_id[:, None] == hd_id[None, :]).astype(BF16)

    o_q, o_k, o_v, o_qi, o_ki, o_wi = 0, w, 2 * w, 3 * w, 4 * w, 4 * w + IDX_DIM
    o_cu = o_wi + IDX_HEADS

    x2 = x.reshape(n, d)
    for l in range(depth):
        sh1, sc1, g1, sh2, sc2, g2 = [mod[l, :, i * d:(i + 1) * d].reshape(bsz, 1, d) for i in range(6)]
        wl = w_in[l]
        wa = wl[:, o_q:o_ki].astype(BF16)
        wki = wl[:, o_ki:o_wi]
        wk = jnp.concatenate([wki, wki], axis=1).astype(BF16)
        ww = jnp.pad(wl[:, o_wi:o_cu], ((0, 0), (0, LANES - IDX_HEADS))).astype(BF16)
        wc = wl[:, o_cu:].astype(BF16)
        tile_h = lambda g_: jnp.tile(g_, N_HEADS).reshape(1, w)
        dup = lambda g_: jnp.concatenate([g_, g_]).reshape(1, LANES)
        q, k, v, qi, kk, wi, hg = _inproj(
            x2, sh1, sc1, wa, wk, ww, wc, tile_h(q_norm_g[l]), tile_h(k_norm_g[l]),
            dup(idx_k_ln_g[l]), dup(idx_k_ln_b[l]), rc, rs1, rs2, gsum, seq, tm)

        r3 = lambda t: t.reshape(bsz, seq, t.shape[-1])
        an = _attention(r3(q), r3(k), r3(v), r3(qi), r3(kk), r3(wi), out_g_attn[l].reshape(1, w), tq)
        w_pad = jnp.pad(conv_dw[l], ((0, CONV_HALO - CONV_KERNEL), (0, 0)))
        cn = _conv(r3(hg), w_pad, conv_b[l].reshape(1, cw), conv_ln_g[l].reshape(1, cw),
                   conv_ln_b[l].reshape(1, cw), out_g_conv[l].reshape(1, cw), tt)

        wo = w_out[l].astype(BF16)
        x1, h2, pq = _outproj(an.reshape(n, w), cn.reshape(n, cw), x2, g1, sh2, sc2,
                              wo[:w], wo[w:], peer_wq[l].astype(BF16), seq, tm)
        a, b, g = _route(pq, peer_sub_keys[l].astype(BF16), tr)
        x2 = _peer(h2, a, b, g, peer_u[l].astype(BF16), peer_v[l].astype(BF16), x1, g2, seq, tp, ec)
    return x2.reshape(bsz, seq, d)
```

```python
import functools

import jax
import jax.numpy as jnp
from jax import lax
from jax.experimental import pallas as pl
from jax.experimental.pallas import tpu as pltpu

F32 = jnp.float32
BF16 = jnp.bfloat16
I32 = jnp.int32

N_HEADS = 8
HEAD_DIM = 64
ATTN_WIDTH = N_HEADS * HEAD_DIM
CONV_KERNEL = 31
IDX_HEADS = 8
IDX_DIM = 64
TOPK_KEYS = 256
ROPE_THETA = 500000.0
ROPE_DIM = HEAD_DIM // 4
PEER_HEADS = 8
PEER_KEYS = 128
PEER_TOPK = 16
PEER_PAIRS = PEER_HEADS * PEER_TOPK
NORM_EPS = 1e-6

LANES = 128
SUBLANES = 8
VMEM_LIMIT = 56 * 1024 * 1024

NEG_BIG = -1e30
INT_MIN = -2 ** 31


def _cparams(sem):
    return pltpu.CompilerParams(dimension_semantics=sem, vmem_limit_bytes=VMEM_LIMIT)


def _nt_dot(a, b):
    return lax.dot_general(a, b, (((1,), (1,)), ((), ())), preferred_element_type=F32)


def _dot(a, b):
    return jnp.dot(a, b, preferred_element_type=F32)


def _ada_kernel(c_ref, w_ref, b_ref, o_ref):
    c = c_ref[...]
    ca = c * jax.nn.sigmoid(c)
    o_ref[0] = _dot(ca, w_ref[0]) + b_ref[0]


def _ada(c_pad, w_ada, b_ada):
    depth, d, n6 = w_ada.shape
    tn = n6 // 4
    return pl.pallas_call(
        _ada_kernel,
        grid=(depth, n6 // tn),
        in_specs=[
            pl.BlockSpec((SUBLANES, d), lambda l, j: (0, 0)),
            pl.BlockSpec((1, d, tn), lambda l, j: (l, 0, j)),
            pl.BlockSpec((1, 1, tn), lambda l, j: (l, 0, j)),
        ],
        out_specs=pl.BlockSpec((1, SUBLANES, tn), lambda l, j: (l, 0, j)),
        out_shape=jax.ShapeDtypeStruct((depth, SUBLANES, n6), F32),
        compiler_params=_cparams(("arbitrary", "arbitrary")),
        name="adaln",
    )(c_pad, w_ada, b_ada.reshape(depth, 1, n6))


def _rope(x, c, s1, s2):
    w = x.shape[-1]
    half = ROPE_DIM // 2
    return x * c + pltpu.roll(x, w - half, 1) * s1 + pltpu.roll(x, half, 1) * s2


def _split_hi_lo(x):
    hi = x.astype(BF16)
    lo = (x - hi.astype(F32)).astype(BF16)
    return hi, lo


def _inproj_kernel(x_ref, sh_ref, sc_ref, wa_ref, wk_ref, ww_ref, wc_ref, gq_ref, gk_ref,
                   lng_ref, lnb_ref, rc_ref, rs1_ref, rs2_ref, gsum_ref,
                   q_ref, k_ref, vt_ref, qi_ref, kk_ref, wit_ref, hg_ref):
    x = x_ref[...]
    ms = jnp.mean(x * x, axis=-1, keepdims=True)
    h = x * lax.rsqrt(ms + NORM_EPS) * (1.0 + sc_ref[0]) + sh_ref[0]
    hb = h.astype(BF16)

    rc1, rs11, rs21 = rc_ref[...], rs1_ref[...], rs2_ref[...]
    reps = ATTN_WIDTH // LANES
    rc = jnp.concatenate([rc1] * reps, axis=1)
    rs1 = jnp.concatenate([rs11] * reps, axis=1)
    rs2 = jnp.concatenate([rs21] * reps, axis=1)
    gsum = gsum_ref[...]

    def head_norm_rope(t, gain):
        hi, lo = _split_hi_lo(t * t)
        ssq = _dot(hi, gsum) + _dot(lo, gsum)
        tn = t * lax.rsqrt(ssq * (1.0 / HEAD_DIM) + NORM_EPS) * gain
        return _rope(tn, rc, rs1, rs2)

    lo_half = lax.broadcasted_iota(I32, (x.shape[0], LANES), 1) < HEAD_DIM

    def one_head_per_group(t):
        parts = []
        for p in range(reps):
            g = t[:, p * LANES:(p + 1) * LANES]
            parts += [jnp.where(lo_half, g, 0.0), jnp.where(lo_half, 0.0, g)]
        return jnp.concatenate(parts, axis=1).astype(BF16)

    w = ATTN_WIDTH
    pa = _dot(hb, wa_ref[...])
    q_ref[...] = one_head_per_group(head_norm_rope(pa[:, 0:w], gq_ref[...]) * (HEAD_DIM ** -0.5))
    k_ref[...] = head_norm_rope(pa[:, w:2 * w], gk_ref[...]).astype(BF16)
    vt_ref[...] = jnp.transpose(pa[:, 2 * w:3 * w]).astype(BF16)
    qi_ref[...] = one_head_per_group(_rope(pa[:, 3 * w:4 * w], rc, rs1, rs2) * (IDX_DIM ** -0.5))

    pk = _dot(hb, wk_ref[...])
    mu = jnp.mean(pk, axis=-1, keepdims=True)
    var = jnp.mean(jnp.square(pk - mu), axis=-1, keepdims=True)
    kn = (pk - mu) * lax.rsqrt(var + NORM_EPS) * lng_ref[...] + lnb_ref[...]
    kk_ref[...] = _rope(kn, rc1, rs11, rs21).astype(BF16)

    wit = jnp.transpose(_dot(hb, ww_ref[...]) * (IDX_HEADS ** -0.5))
    wit_ref[...] = wit[:IDX_HEADS]

    pc = _dot(hb, wc_ref[...])
    cw = pc.shape[1] // 2
    hg_ref[...] = pc[:, :cw] * jax.nn.sigmoid(pc[:, cw:])


def _inproj(x2, sh, sc, wa, wk, ww, wc, gq, gk, lng, lnb, rc, rs1, rs2, gsum, seq, tm):
    n, d = x2.shape
    per_b = seq // tm
    row = lambda i: (i, 0)
    const = lambda i: (0, 0)
    bat = lambda i: (i // per_b, 0, 0)
    cw = wc.shape[1] // 2
    outs = [
        jax.ShapeDtypeStruct((n, 2 * ATTN_WIDTH), BF16),
        jax.ShapeDtypeStruct((n, ATTN_WIDTH), BF16),
        jax.ShapeDtypeStruct((ATTN_WIDTH, n), BF16),
        jax.ShapeDtypeStruct((n, 2 * ATTN_WIDTH), BF16),
        jax.ShapeDtypeStruct((n, LANES), BF16),
        jax.ShapeDtypeStruct((IDX_HEADS, n), F32),
        jax.ShapeDtypeStruct((n, cw), F32),
    ]
    col = lambda i: (0, i)
    token_minor = (False, False, True, False, False, True, False)
    out_specs = [pl.BlockSpec((o.shape[0], tm), col) if tmin else pl.BlockSpec((tm, o.shape[1]), row)
                 for o, tmin in zip(outs, token_minor)]
    return pl.pallas_call(
        _inproj_kernel,
        grid=(n // tm,),
        in_specs=[
            pl.BlockSpec((tm, d), row),
            pl.BlockSpec((1, 1, d), bat),
            pl.BlockSpec((1, 1, d), bat),
            pl.BlockSpec(wa.shape, const),
            pl.BlockSpec(wk.shape, const),
            pl.BlockSpec(ww.shape, const),
            pl.BlockSpec(wc.shape, const),
            pl.BlockSpec(gq.shape, const),
            pl.BlockSpec(gk.shape, const),
            pl.BlockSpec(lng.shape, const),
            pl.BlockSpec(lnb.shape, const),
            pl.BlockSpec((tm, LANES), row),
            pl.BlockSpec((tm, LANES), row),
            pl.BlockSpec((tm, LANES), row),
            pl.BlockSpec(gsum.shape, const),
        ],
        out_specs=out_specs,
        out_shape=outs,
        compiler_params=_cparams(("arbitrary",)),
        name="inproj",
    )(x2, sh, sc, wa, wk, ww, wc, gq, gk, lng, lnb, rc, rs1, rs2, gsum)


CONV_HALO = 32
CONV_ROWS = 64


def _conv_kernel(cur_ref, halo_ref, w_ref, b_ref, lg_ref, lb_ref, og_ref, o_ref, ext_ref, *, tt):
    i = pl.program_id(1)
    ext_ref[0:CONV_HALO, :] = jnp.where(i > 0, halo_ref[0], 0.0)
    ext_ref[CONV_HALO:CONV_HALO + tt, :] = cur_ref[0]
    lead = CONV_HALO - (CONV_KERNEL - 1)
    for r0 in range(0, tt, CONV_ROWS):
        acc = jnp.zeros((CONV_ROWS, ext_ref.shape[1]), F32)
        for j in range(CONV_KERNEL):
            acc = acc + ext_ref[r0 + lead + j:r0 + lead + j + CONV_ROWS, :] * w_ref[j:j + 1, :]
        hcv = acc + b_ref[...]
        mu = jnp.mean(hcv, axis=-1, keepdims=True)
        var = jnp.mean(jnp.square(hcv - mu), axis=-1, keepdims=True)
        y = (hcv - mu) * lax.rsqrt(var + NORM_EPS) * lg_ref[...] + lb_ref[...]
        y = y * jax.nn.sigmoid(y)
        ms = jnp.mean(y * y, axis=-1, keepdims=True)
        o_ref[0, r0:r0 + CONV_ROWS, :] = (y * lax.rsqrt(ms + NORM_EPS) * og_ref[...]).astype(BF16)


def _conv(hg3, w_pad, b, lg, lb, og, tt):
    bsz, seq, cw = hg3.shape
    hb = tt // CONV_HALO
    const = lambda bb, i: (0, 0)
    return pl.pallas_call(
        functools.partial(_conv_kernel, tt=tt),
        grid=(bsz, seq // tt),
        in_specs=[
            pl.BlockSpec((1, tt, cw), lambda bb, i: (bb, i, 0)),
            pl.BlockSpec((1, CONV_HALO, cw), lambda bb, i: (bb, jnp.maximum(i * hb - 1, 0), 0)),
            pl.BlockSpec(w_pad.shape, const),
            pl.BlockSpec(b.shape, const),
            pl.BlockSpec(lg.shape, const),
            pl.BlockSpec(lb.shape, const),
            pl.BlockSpec(og.shape, const),
        ],
        out_specs=pl.BlockSpec((1, tt, cw), lambda bb, i: (bb, i, 0)),
        out_shape=jax.ShapeDtypeStruct((bsz, seq, cw), BF16),
        scratch_shapes=[pltpu.VMEM((CONV_HALO + tt, cw), F32)],
        compiler_params=_cparams(("arbitrary", "arbitrary")),
        name="conv",
    )(hg3, hg3, w_pad, b, lg, lb, og)


def _attn_kernel(q_ref, qi_ref, wit_ref, k_ref, vt_ref, kk_ref, og_ref, o_ref,
                 key_ref, bias_ref, m_ref, l_ref, a_ref, acc_ref, lm_ref, p_ref, *, tq, n_sel, seq):
    i = pl.program_id(1)
    nk = i + 1
    krow = lax.broadcasted_iota(I32, (tq, tq), 0)
    qcol = lax.broadcasted_iota(I32, (tq, tq), 1) + i * tq

    def chunk_rows(c):
        return pl.ds(pl.multiple_of(c * tq, tq), tq)

    wit = wit_ref[...]

    def score_chunk(c, carry):
        kic = kk_ref[0, chunk_rows(c), :]
        s = jnp.zeros((tq, tq), F32)
        for h in range(IDX_HEADS):
            qh = qi_ref[0, :, h * LANES:(h + 1) * LANES]
            s = s + wit[h:h + 1, :] * jnp.maximum(_nt_dot(kic, qh), 0.0)
        s = jnp.where(krow + c * tq <= qcol, s, -jnp.inf)
        bits = pltpu.bitcast(s, I32)
        key = jnp.where(bits < 0, bits ^ jnp.int32(0x7FFFFFFF), bits)
        key_ref[c] = jnp.where(s == 0.0, 0, key)
        return carry

    lax.fori_loop(0, nk, score_chunk, 0)

    def count(pred_fn):
        def body(c, acc):
            hit = jnp.where(pred_fn(c, key_ref[c]), 1, 0)
            return acc + jnp.sum(hit.reshape(tq // SUBLANES, SUBLANES, tq), axis=0)
        acc = lax.fori_loop(0, nk, body, jnp.zeros((SUBLANES, tq), I32))
        return jnp.sum(acc, axis=0, keepdims=True)

    def count_ge(thr):
        return count(lambda c, key: key >= thr)

    cand = jnp.where(count_ge(jnp.zeros((1, tq), I32)) >= n_sel, 0, INT_MIN).astype(I32)

    def bit_step(bi, cand):
        trial = cand + jnp.left_shift(jnp.int32(1), 30 - bi)
        return jnp.where(count_ge(trial) >= n_sel, trial, cand)

    tau = lax.fori_loop(0, 31, bit_step, cand)
    n_ge = count_ge(tau)
    need = n_sel - count_ge(tau + 1)

    def tie_cut():
        def step(bi, jm):
            trial = jm + jnp.left_shift(jnp.int32(1), (seq.bit_length() - 2) - bi)
            f = count(lambda c, key: jnp.where(key == tau, krow + c * tq, seq) < trial)
            return jnp.where(f < need, trial, jm)
        return lax.fori_loop(0, seq.bit_length() - 1, step, jnp.zeros((1, tq), I32))

    has_tie = jnp.max(n_ge) > n_sel
    jm = lax.cond(has_tie, tie_cut, lambda: jnp.full((1, tq), seq, I32))

    def bias_chunk(c, carry):
        key = key_ref[c]
        kidx = krow + c * tq
        tied = jnp.where(kidx <= jm, 0.0, NEG_BIG)
        above = jnp.where(key > tau, 0.0, NEG_BIG)
        bias_ref[c] = jnp.where(kidx <= qcol, jnp.where(key == tau, tied, above), NEG_BIG)
        return carry

    lax.fori_loop(0, nk, bias_chunk, 0)

    m_ref[...] = jnp.full(m_ref.shape, NEG_BIG, F32)
    l_ref[...] = jnp.zeros(l_ref.shape, F32)
    acc_ref[...] = jnp.zeros(acc_ref.shape, F32)

    def attn_chunk(c, carry):
        rows = chunk_rows(c)
        for hd in range(N_HEADS):
            p = hd // 2
            kp = k_ref[0, rows, p * LANES:(p + 1) * LANES]
            qh = q_ref[0, :, hd * LANES:(hd + 1) * LANES]
            lm_ref[hd] = _nt_dot(kp, qh) + bias_ref[c]
        for hd in range(N_HEADS):
            m_old = m_ref[hd]
            m_new = jnp.maximum(m_old, jnp.max(lm_ref[hd], axis=0, keepdims=True))
            pr = jnp.exp(lm_ref[hd] - m_new)
            a = jnp.exp(m_old - m_new)
            l_ref[hd] = a * l_ref[hd] + jnp.sum(pr, axis=0, keepdims=True)
            m_ref[hd] = m_new
            a_ref[hd] = a
            p_ref[hd] = pr.astype(BF16)
        for hd in range(N_HEADS):
            vth = vt_ref[hd * HEAD_DIM:(hd + 1) * HEAD_DIM, rows]
            acc_ref[hd] = acc_ref[hd] * a_ref[hd] + _dot(vth, p_ref[hd])
        return carry

    lax.fori_loop(0, nk, attn_chunk, 0)

    ot = jnp.concatenate([acc_ref[hd] / l_ref[hd] for hd in range(N_HEADS)], axis=0)
    ms = jnp.mean(ot * ot, axis=0, keepdims=True)
    ot = ot * lax.rsqrt(ms + NORM_EPS)
    o_ref[0] = (jnp.transpose(ot) * og_ref[...]).astype(BF16)


def _attention(q3, k3, vt, qi3, kk3, wit, og, tq):
    bsz, seq, w = k3.shape
    n_sel = min(TOPK_KEYS, seq // 4)
    nc = seq // tq
    qblk = lambda bb, i: (bb, i, 0)
    full = lambda bb, i: (bb, 0, 0)
    return pl.pallas_call(
        functools.partial(_attn_kernel, tq=tq, n_sel=n_sel, seq=seq),
        grid=(bsz, nc),
        in_specs=[
            pl.BlockSpec((1, tq, 2 * w), qblk),
            pl.BlockSpec((1, tq, 2 * w), qblk),
            pl.BlockSpec((SUBLANES, tq), lambda bb, i: (0, bb * nc + i)),
            pl.BlockSpec((1, seq, w), full),
            pl.BlockSpec((w, seq), lambda bb, i: (0, bb)),
            pl.BlockSpec((1, seq, LANES), full),
            pl.BlockSpec(og.shape, lambda bb, i: (0, 0)),
        ],
        out_specs=pl.BlockSpec((1, tq, w), qblk),
        out_shape=jax.ShapeDtypeStruct((bsz, seq, w), BF16),
        scratch_shapes=[
            pltpu.VMEM((nc, tq, tq), I32),
            pltpu.VMEM((nc, tq, tq), F32),
            pltpu.VMEM((N_HEADS, 1, tq), F32),
            pltpu.VMEM((N_HEADS, 1, tq), F32),
            pltpu.VMEM((N_HEADS, 1, tq), F32),
            pltpu.VMEM((N_HEADS, HEAD_DIM, tq), F32),
            pltpu.VMEM((N_HEADS, tq, tq), F32),
            pltpu.VMEM((N_HEADS, tq, tq), BF16),
        ],
        compiler_params=_cparams(("arbitrary", "arbitrary")),
        name="dsa_attention",
    )(q3, qi3, wit, k3, vt, kk3, og)


def _outproj_kernel(an_ref, cn_ref, x_ref, g1_ref, sh_ref, sc_ref, woa_ref, woc_ref, wq_ref,
                    x1_ref, h2_ref, pq_ref):
    y = _dot(an_ref[...], woa_ref[...]) + _dot(cn_ref[...], woc_ref[...])
    x1 = x_ref[...] + g1_ref[0] * y
    x1_ref[...] = x1
    ms = jnp.mean(x1 * x1, axis=-1, keepdims=True)
    h2 = (x1 * lax.rsqrt(ms + NORM_EPS) * (1.0 + sc_ref[0]) + sh_ref[0]).astype(BF16)
    h2_ref[...] = h2
    pq_ref[...] = _dot(h2, wq_ref[...]).astype(BF16)


def _outproj(an, cn, x2, g1, sh2, sc2, woa, woc, wq, seq, tm):
    n, d = x2.shape
    per_b = seq // tm
    row = lambda i: (i, 0)
    const = lambda i: (0, 0)
    bat = lambda i: (i // per_b, 0, 0)
    return pl.pallas_call(
        _outproj_kernel,
        grid=(n // tm,),
        in_specs=[
            pl.BlockSpec((tm, an.shape[1]), row),
            pl.BlockSpec((tm, cn.shape[1]), row),
            pl.BlockSpec((tm, d), row),
            pl.BlockSpec((1, 1, d), bat),
            pl.BlockSpec((1, 1, d), bat),
            pl.BlockSpec((1, 1, d), bat),
            pl.BlockSpec(woa.shape, const),
            pl.BlockSpec(woc.shape, const),
            pl.BlockSpec(wq.shape, const),
        ],
        out_specs=[pl.BlockSpec((tm, d), row), pl.BlockSpec((tm, d), row),
                   pl.BlockSpec((tm, wq.shape[1]), row)],
        out_shape=[jax.ShapeDtypeStruct((n, d), F32), jax.ShapeDtypeStruct((n, d), BF16),
                   jax.ShapeDtypeStruct((n, wq.shape[1]), BF16)],
        compiler_params=_cparams(("arbitrary",)),
        name="outproj",
    )(an, cn, x2, g1, sh2, sc2, woa, woc, wq)


def _top_rows(s, n_rows, ids=None):
    iota = lax.broadcasted_iota(I32, s.shape, 0)
    vals, picks = [], []
    for _ in range(PEER_TOPK):
        m = jnp.max(s, axis=0, keepdims=True)
        pos = jnp.min(jnp.where(s == m, iota, n_rows), axis=0, keepdims=True)
        hit = iota == pos
        vals.append(m)
        picks.append(pos if ids is None else jnp.max(jnp.where(hit, ids, -1), axis=0, keepdims=True))
        s = jnp.where(hit, -jnp.inf, s)
    return jnp.concatenate(vals, axis=0), jnp.concatenate(picks, axis=0)


def _route_kernel(pq_ref, sk_ref, a_ref, b_ref, g_ref, e_scr, g_scr):
    h = pl.program_id(1)
    half = PEER_KEYS
    s1 = _nt_dot(sk_ref[0, 0], pq_ref[:, :half])
    s2 = _nt_dot(sk_ref[0, 1], pq_ref[:, half:])
    v1, i1 = _top_rows(s1, PEER_KEYS)
    v2, i2 = _top_rows(s2, PEER_KEYS)
    cand = jnp.concatenate([v1[r:r + 1] + v2 for r in range(PEER_TOPK)], axis=0)
    cid = jnp.concatenate([i1[r:r + 1] * PEER_KEYS + i2 for r in range(PEER_TOPK)], axis=0)
    ts, e = _top_rows(cand, PEER_TOPK * PEER_TOPK, cid)
    ex = jnp.exp(ts - ts[0:1])
    g = ex / jnp.sum(ex, axis=0, keepdims=True)
    rows = pl.ds(pl.multiple_of(h * PEER_TOPK, PEER_TOPK), PEER_TOPK)
    e_scr[rows, :] = e
    g_scr[rows, :] = g

    @pl.when(h == PEER_HEADS - 1)
    def _():
        e_all = e_scr[...]
        a_ref[...] = jnp.transpose(jnp.right_shift(e_all, 7).astype(F32))
        b_ref[...] = jnp.transpose(jnp.bitwise_and(e_all, PEER_KEYS - 1).astype(F32))
        g_ref[...] = jnp.transpose(g_scr[...])


def _route(pq, sk, tr):
    n = pq.shape[0]
    qd = 2 * PEER_KEYS
    out = jax.ShapeDtypeStruct((n, PEER_PAIRS), F32)
    oblk = pl.BlockSpec((tr, PEER_PAIRS), lambda i, h: (i, 0))
    return pl.pallas_call(
        _route_kernel,
        grid=(n // tr, PEER_HEADS),
        in_specs=[
            pl.BlockSpec((tr, qd), lambda i, h: (i, h)),
            pl.BlockSpec((1, 2, PEER_KEYS, PEER_KEYS), lambda i, h: (h, 0, 0, 0)),
        ],
        out_specs=[oblk, oblk, oblk],
        out_shape=[out, out, out],
        scratch_shapes=[pltpu.VMEM((PEER_PAIRS, tr), I32), pltpu.VMEM((PEER_PAIRS, tr), F32)],
        compiler_params=_cparams(("arbitrary", "arbitrary")),
        name="peer_route",
    )(pq, sk)


C_PITCH = PEER_KEYS + SUBLANES
C_UNROLL = 4


def _gelu(x):
    return 0.5 * x * (1.0 + lax.erf(x * (2.0 ** -0.5)))


def _peer_kernel(h2_ref, a_ref, b_ref, g_ref, u_ref, v_ref, x1_ref, g2_ref, o_ref, c_scr, acc_ref,
                 *, tm, ec):
    j = pl.program_id(1)
    nj = pl.num_programs(1)

    @pl.when(j == 0)
    def _():
        srow = lax.broadcasted_iota(I32, (PEER_KEYS, PEER_PAIRS), 0).astype(F32)

        def tokens(tb, carry):
            for u in range(C_UNROLL):
                t = tb * C_UNROLL + u
                a_row = a_ref[pl.ds(t, 1), :]
                b_row = b_ref[pl.ds(t, 1), :]
                g_row = g_ref[pl.ds(t, 1), :]
                at = jnp.where(srow == a_row, g_row, 0.0)
                bt = jnp.where(srow == b_row, 1.0, 0.0).astype(BF16)
                hi, lo = _split_hi_lo(at)
                ct = _nt_dot(jnp.concatenate([hi, lo], axis=1), jnp.concatenate([bt, bt], axis=1))
                c_scr[pl.ds(pl.multiple_of(t * C_PITCH, SUBLANES), PEER_KEYS), :] = ct
            return carry

        lax.fori_loop(0, tm // C_UNROLL, tokens, 0)
        acc_ref[...] = jnp.zeros(acc_ref.shape, F32)

    act = _gelu(_nt_dot(h2_ref[...], u_ref[...]))
    parts = []
    for ii in range(ec // PEER_KEYS):
        cs = c_scr[pl.ds(j * (ec // PEER_KEYS) + ii, tm, stride=C_PITCH), :]
        parts.append((act[:, ii * PEER_KEYS:(ii + 1) * PEER_KEYS] * cs).astype(BF16))
    acc_ref[...] += _dot(jnp.concatenate(parts, axis=1), v_ref[...])

    @pl.when(j == nj - 1)
    def _():
        o_ref[...] = x1_ref[...] + g2_ref[0] * acc_ref[...]


def _peer(h2, a, b, g, u, v, x1, g2, seq, tm, ec):
    n, d = x1.shape
    ne = u.shape[0]
    per_b = seq // tm
    row = lambda i, j: (i, 0)
    return pl.pallas_call(
        functools.partial(_peer_kernel, tm=tm, ec=ec),
        grid=(n // tm, ne // ec),
        in_specs=[
            pl.BlockSpec((tm, d), row),
            pl.BlockSpec((tm, PEER_PAIRS), row),
            pl.BlockSpec((tm, PEER_PAIRS), row),
            pl.BlockSpec((tm, PEER_PAIRS), row),
            pl.BlockSpec((ec, d), lambda i, j: (j, 0)),
            pl.BlockSpec((ec, d), lambda i, j: (j, 0)),
            pl.BlockSpec((tm, d), row),
            pl.BlockSpec((1, 1, d), lambda i, j: (i // per_b, 0, 0)),
        ],
        out_specs=pl.BlockSpec((tm, d), row),
        out_shape=jax.ShapeDtypeStruct((n, d), F32),
        scratch_shapes=[pltpu.VMEM((tm * C_PITCH, PEER_KEYS), F32), pltpu.VMEM((tm, d), F32)],
        compiler_params=_cparams(("arbitrary", "arbitrary")),
        name="peer_experts",
    )(h2, a, b, g, u, v, x1, g2)


def _rope_lane_tables(positions):
    half = ROPE_DIM // 2
    inv_freq = jnp.power(jnp.float32(ROPE_THETA), -jnp.arange(0, ROPE_DIM, 2, dtype=F32) / ROPE_DIM)
    ang = positions.astype(F32).reshape(-1, 1) * inv_freq
    cos, sin = jnp.cos(ang), jnp.sin(ang)
    j = jnp.arange(LANES) % HEAD_DIM
    cos_l, sin_l = cos[:, j % half], sin[:, j % half]
    rc = jnp.where(j < ROPE_DIM, cos_l, 1.0)
    rs1 = jnp.where(j < half, -sin_l, 0.0)
    rs2 = jnp.where((j >= half) & (j < ROPE_DIM), sin_l, 0.0)
    return rc, rs1, rs2


def _tile_size(total, want):
    t = min(want, total)
    assert total % t == 0
    return t


def kernel(x, c, positions, w_ada, b_ada, w_in, q_norm_g, k_norm_g, idx_k_ln_g, idx_k_ln_b,
           conv_dw, conv_b, conv_ln_g, conv_ln_b, out_g_attn, out_g_conv, w_out, peer_wq,
           peer_sub_keys, peer_u, peer_v):
    bsz, seq, d = x.shape
    depth = w_ada.shape[0]
    n = bsz * seq
    w = ATTN_WIDTH
    cw = d - w
    assert bsz <= SUBLANES and cw % LANES == 0

    tm = _tile_size(seq, 512)
    tq = _tile_size(seq, 256)
    tt = _tile_size(seq, 512)
    tr = _tile_size(seq, 256)
    tp = _tile_size(seq, 256)
    ec = 1024

    c_pad = jnp.zeros((SUBLANES, d), F32).at[:bsz].set(c)
    mod = _ada(c_pad, w_ada, b_ada)[:, :bsz]
    rc, rs1, rs2 = _rope_lane_tables(positions)
    hd_id = jnp.arange(w) // HEAD_DIM
    gsum = (hd_id[:, None] == hd_id[None, :]).astype(BF16)

    o_q, o_k, o_v, o_qi, o_ki, o_wi = 0, w, 2 * w, 3 * w, 4 * w, 4 * w + IDX_DIM
    o_cu = o_wi + IDX_HEADS

    x2 = x.reshape(n, d)
    for l in range(depth):
        sh1, sc1, g1, sh2, sc2, g2 = [mod[l, :, i * d:(i + 1) * d].reshape(bsz, 1, d) for i in range(6)]
        wl = w_in[l]
        wa = wl[:, o_q:o_ki].astype(BF16)
        wki = wl[:, o_ki:o_wi]
        wk = jnp.concatenate([wki, wki], axis=1).astype(BF16)
        ww = jnp.pad(wl[:, o_wi:o_cu], ((0, 0), (0, LANES - IDX_HEADS))).astype(BF16)
        wc = wl[:, o_cu:].astype(BF16)
        tile_h = lambda g_: jnp.tile(g_, N_HEADS).reshape(1, w)
        dup = lambda g_: jnp.concatenate([g_, g_]).reshape(1, LANES)
        q, k, vt, qi, kk, wit, hg = _inproj(
            x2, sh1, sc1, wa, wk, ww, wc, tile_h(q_norm_g[l]), tile_h(k_norm_g[l]),
            dup(idx_k_ln_g[l]), dup(idx_k_ln_b[l]), rc, rs1, rs2, gsum, seq, tm)

        r3 = lambda t: t.reshape(bsz, seq, t.shape[-1])
        an = _attention(r3(q), r3(k), vt, r3(qi), r3(kk), wit, out_g_attn[l].reshape(1, w), tq)
        w_pad = jnp.pad(conv_dw[l], ((0, CONV_HALO - CONV_KERNEL), (0, 0)))
        cn = _conv(r3(hg), w_pad, conv_b[l].reshape(1, cw), conv_ln_g[l].reshape(1, cw),
                   conv_ln_b[l].reshape(1, cw), out_g_conv[l].reshape(1, cw), tt)

        wo = w_out[l].astype(BF16)
        x1, h2, pq = _outproj(an.reshape(n, w), cn.reshape(n, cw), x2, g1, sh2, sc2,
                              wo[:w], wo[w:], peer_wq[l].astype(BF16), seq, tm)
        a, b, g = _route(pq, peer_sub_keys[l].astype(BF16), tr)
        x2 = _peer(h2, a, b, g, peer_u[l].astype(BF16), peer_v[l].astype(BF16), x1, g2, seq, tp, ec)
    return x2.reshape(bsz, seq, d)
```

```python
import functools

import jax
import jax.numpy as jnp
from jax import lax
from jax.experimental import pallas as pl
from jax.experimental.pallas import tpu as pltpu

F32 = jnp.float32
BF16 = jnp.bfloat16
I32 = jnp.int32

N_HEADS = 8
HEAD_DIM = 64
ATTN_WIDTH = N_HEADS * HEAD_DIM
CONV_KERNEL = 31
IDX_HEADS = 8
IDX_DIM = 64
TOPK_KEYS = 256
ROPE_THETA = 500000.0
ROPE_DIM = HEAD_DIM // 4
PEER_HEADS = 8
PEER_KEYS = 128
PEER_TOPK = 16
PEER_PAIRS = PEER_HEADS * PEER_TOPK
NORM_EPS = 1e-6

LANES = 128
SUBLANES = 8
VMEM_LIMIT = 56 * 1024 * 1024

NEG_BIG = -1e30
INT_MIN = -2 ** 31


def _cparams(sem):
    return pltpu.CompilerParams(dimension_semantics=sem, vmem_limit_bytes=VMEM_LIMIT)


def _nt_dot(a, b):
    return lax.dot_general(a, b, (((1,), (1,)), ((), ())), preferred_element_type=F32)


def _dot(a, b):
    return jnp.dot(a, b, preferred_element_type=F32)


def _ada_kernel(c_ref, w_ref, b_ref, o_ref):
    c = c_ref[...]
    ca = c * jax.nn.sigmoid(c)
    o_ref[0] = _dot(ca, w_ref[0]) + b_ref[0]


def _ada(c_pad, w_ada, b_ada):
    depth, d, n6 = w_ada.shape
    tn = n6 // 4
    return pl.pallas_call(
        _ada_kernel,
        grid=(depth, n6 // tn),
        in_specs=[
            pl.BlockSpec((SUBLANES, d), lambda l, j: (0, 0)),
            pl.BlockSpec((1, d, tn), lambda l, j: (l, 0, j)),
            pl.BlockSpec((1, 1, tn), lambda l, j: (l, 0, j)),
        ],
        out_specs=pl.BlockSpec((1, SUBLANES, tn), lambda l, j: (l, 0, j)),
        out_shape=jax.ShapeDtypeStruct((depth, SUBLANES, n6), F32),
        compiler_params=_cparams(("arbitrary", "arbitrary")),
        name="adaln",
    )(c_pad, w_ada, b_ada.reshape(depth, 1, n6))


def _rope(x, c, s1, s2):
    w = x.shape[-1]
    half = ROPE_DIM // 2
    return x * c + pltpu.roll(x, w - half, 1) * s1 + pltpu.roll(x, half, 1) * s2


def _split_hi_lo(x):
    hi = x.astype(BF16)
    lo = (x - hi.astype(F32)).astype(BF16)
    return hi, lo


def _inproj_kernel(x_ref, sh_ref, sc_ref, wa_ref, wk_ref, ww_ref, wc_ref, gq_ref, gk_ref,
                   lng_ref, lnb_ref, rc_ref, rs1_ref, rs2_ref, gsum_ref,
                   q_ref, k_ref, vt_ref, qi_ref, kk_ref, wit_ref, hg_ref):
    x = x_ref[...]
    ms = jnp.mean(x * x, axis=-1, keepdims=True)
    h = x * lax.rsqrt(ms + NORM_EPS) * (1.0 + sc_ref[0]) + sh_ref[0]
    hb = h.astype(BF16)

    rc1, rs11, rs21 = rc_ref[...], rs1_ref[...], rs2_ref[...]
    reps = ATTN_WIDTH // LANES
    rc = jnp.concatenate([rc1] * reps, axis=1)
    rs1 = jnp.concatenate([rs11] * reps, axis=1)
    rs2 = jnp.concatenate([rs21] * reps, axis=1)
    gsum = gsum_ref[...]

    def head_norm_rope(t, gain):
        hi, lo = _split_hi_lo(t * t)
        ssq = _dot(hi, gsum) + _dot(lo, gsum)
        tn = t * lax.rsqrt(ssq * (1.0 / HEAD_DIM) + NORM_EPS) * gain
        return _rope(tn, rc, rs1, rs2)

    lo_half = lax.broadcasted_iota(I32, (x.shape[0], LANES), 1) < HEAD_DIM

    def one_head_per_group(t):
        parts = []
        for p in range(reps):
            g = t[:, p * LANES:(p + 1) * LANES]
            parts += [jnp.where(lo_half, g, 0.0), jnp.where(lo_half, 0.0, g)]
        return jnp.concatenate(parts, axis=1).astype(BF16)

    w = ATTN_WIDTH
    pa = _dot(hb, wa_ref[...])
    q_ref[...] = one_head_per_group(head_norm_rope(pa[:, 0:w], gq_ref[...]) * (HEAD_DIM ** -0.5))
    k_ref[...] = head_norm_rope(pa[:, w:2 * w], gk_ref[...]).astype(BF16)
    vt_ref[...] = jnp.transpose(pa[:, 2 * w:3 * w]).astype(BF16)
    qi_ref[...] = one_head_per_group(_rope(pa[:, 3 * w:4 * w], rc, rs1, rs2) * (IDX_DIM ** -0.5))

    pk = _dot(hb, wk_ref[...])
    mu = jnp.mean(pk, axis=-1, keepdims=True)
    var = jnp.mean(jnp.square(pk - mu), axis=-1, keepdims=True)
    kn = (pk - mu) * lax.rsqrt(var + NORM_EPS) * lng_ref[...] + lnb_ref[...]
    kk_ref[...] = _rope(kn, rc1, rs11, rs21).astype(BF16)

    wit = jnp.transpose(_dot(hb, ww_ref[...]) * (IDX_HEADS ** -0.5))
    wit_ref[...] = wit[:IDX_HEADS]

    pc = _dot(hb, wc_ref[...])
    cw = pc.shape[1] // 2
    hg_ref[...] = pc[:, :cw] * jax.nn.sigmoid(pc[:, cw:])


def _inproj(x2, sh, sc, wa, wk, ww, wc, gq, gk, lng, lnb, rc, rs1, rs2, gsum, seq, tm):
    n, d = x2.shape
    per_b = seq // tm
    row = lambda i: (i, 0)
    const = lambda i: (0, 0)
    bat = lambda i: (i // per_b, 0, 0)
    cw = wc.shape[1] // 2
    outs = [
        jax.ShapeDtypeStruct((n, 2 * ATTN_WIDTH), BF16),
        jax.ShapeDtypeStruct((n, ATTN_WIDTH), BF16),
        jax.ShapeDtypeStruct((ATTN_WIDTH, n), BF16),
        jax.ShapeDtypeStruct((n, 2 * ATTN_WIDTH), BF16),
        jax.ShapeDtypeStruct((n, LANES), BF16),
        jax.ShapeDtypeStruct((IDX_HEADS, n), F32),
        jax.ShapeDtypeStruct((n, cw), F32),
    ]
    col = lambda i: (0, i)
    token_minor = (False, False, True, False, False, True, False)
    out_specs = [pl.BlockSpec((o.shape[0], tm), col) if tmin else pl.BlockSpec((tm, o.shape[1]), row)
                 for o, tmin in zip(outs, token_minor)]
    return pl.pallas_call(
        _inproj_kernel,
        grid=(n // tm,),
        in_specs=[
            pl.BlockSpec((tm, d), row),
            pl.BlockSpec((1, 1, d), bat),
            pl.BlockSpec((1, 1, d), bat),
            pl.BlockSpec(wa.shape, const),
            pl.BlockSpec(wk.shape, const),
            pl.BlockSpec(ww.shape, const),
            pl.BlockSpec(wc.shape, const),
            pl.BlockSpec(gq.shape, const),
            pl.BlockSpec(gk.shape, const),
            pl.BlockSpec(lng.shape, const),
            pl.BlockSpec(lnb.shape, const),
            pl.BlockSpec((tm, LANES), row),
            pl.BlockSpec((tm, LANES), row),
            pl.BlockSpec((tm, LANES), row),
            pl.BlockSpec(gsum.shape, const),
        ],
        out_specs=out_specs,
        out_shape=outs,
        compiler_params=_cparams(("arbitrary",)),
        name="inproj",
    )(x2, sh, sc, wa, wk, ww, wc, gq, gk, lng, lnb, rc, rs1, rs2, gsum)


CONV_HALO = 32
CONV_ROWS = 64


def _conv_kernel(cur_ref, halo_ref, w_ref, b_ref, lg_ref, lb_ref, og_ref, o_ref, ext_ref, *, tt):
    i = pl.program_id(1)
    ext_ref[0:CONV_HALO, :] = jnp.where(i > 0, halo_ref[0], 0.0)
    ext_ref[CONV_HALO:CONV_HALO + tt, :] = cur_ref[0]
    lead = CONV_HALO - (CONV_KERNEL - 1)
    for r0 in range(0, tt, CONV_ROWS):
        acc = jnp.zeros((CONV_ROWS, ext_ref.shape[1]), F32)
        for j in range(CONV_KERNEL):
            acc = acc + ext_ref[r0 + lead + j:r0 + lead + j + CONV_ROWS, :] * w_ref[j:j + 1, :]
        hcv = acc + b_ref[...]
        mu = jnp.mean(hcv, axis=-1, keepdims=True)
        var = jnp.mean(jnp.square(hcv - mu), axis=-1, keepdims=True)
        y = (hcv - mu) * lax.rsqrt(var + NORM_EPS) * lg_ref[...] + lb_ref[...]
        y = y * jax.nn.sigmoid(y)
        ms = jnp.mean(y * y, axis=-1, keepdims=True)
        o_ref[0, r0:r0 + CONV_ROWS, :] = (y * lax.rsqrt(ms + NORM_EPS) * og_ref[...]).astype(BF16)


def _conv(hg3, w_pad, b, lg, lb, og, tt):
    bsz, seq, cw = hg3.shape
    hb = tt // CONV_HALO
    const = lambda bb, i: (0, 0)
    return pl.pallas_call(
        functools.partial(_conv_kernel, tt=tt),
        grid=(bsz, seq // tt),
        in_specs=[
            pl.BlockSpec((1, tt, cw), lambda bb, i: (bb, i, 0)),
            pl.BlockSpec((1, CONV_HALO, cw), lambda bb, i: (bb, jnp.maximum(i * hb - 1, 0), 0)),
            pl.BlockSpec(w_pad.shape, const),
            pl.BlockSpec(b.shape, const),
            pl.BlockSpec(lg.shape, const),
            pl.BlockSpec(lb.shape, const),
            pl.BlockSpec(og.shape, const),
        ],
        out_specs=pl.BlockSpec((1, tt, cw), lambda bb, i: (bb, i, 0)),
        out_shape=jax.ShapeDtypeStruct((bsz, seq, cw), BF16),
        scratch_shapes=[pltpu.VMEM((CONV_HALO + tt, cw), F32)],
        compiler_params=_cparams(("arbitrary", "arbitrary")),
        name="conv",
    )(hg3, hg3, w_pad, b, lg, lb, og)


def _attn_kernel(q_ref, qi_ref, wit_ref, k_ref, vt_ref, kk_ref, og_ref, o_ref,
                 key_ref, bias_ref, m_ref, l_ref, a_ref, acc_ref, lm_ref, p_ref, *, tq, n_sel, seq):
    i = pl.program_id(1)
    nk = i + 1
    krow = lax.broadcasted_iota(I32, (tq, tq), 0)
    qcol = lax.broadcasted_iota(I32, (tq, tq), 1) + i * tq

    def chunk_rows(c):
        return pl.ds(pl.multiple_of(c * tq, tq), tq)

    wit = wit_ref[...]

    def score_chunk(c, carry):
        kic = kk_ref[0, chunk_rows(c), :]
        s = jnp.zeros((tq, tq), F32)
        for h in range(IDX_HEADS):
            qh = qi_ref[0, :, h * LANES:(h + 1) * LANES]
            s = s + wit[h:h + 1, :] * jnp.maximum(_nt_dot(kic, qh), 0.0)
        s = jnp.where(krow + c * tq <= qcol, s, -jnp.inf)
        bits = pltpu.bitcast(s, I32)
        key = jnp.where(bits < 0, bits ^ jnp.int32(0x7FFFFFFF), bits)
        key_ref[c] = jnp.where(s == 0.0, 0, key)
        return carry

    lax.fori_loop(0, nk, score_chunk, 0)

    def count(pred_fn):
        def body(c, acc):
            hit = jnp.where(pred_fn(c, key_ref[c]), 1, 0)
            return acc + jnp.sum(hit.reshape(tq // SUBLANES, SUBLANES, tq), axis=0)
        acc = lax.fori_loop(0, nk, body, jnp.zeros((SUBLANES, tq), I32))
        return jnp.sum(acc, axis=0, keepdims=True)

    def count_ge(thr):
        return count(lambda c, key: key >= thr)

    cand = jnp.where(count_ge(jnp.zeros((1, tq), I32)) >= n_sel, 0, INT_MIN).astype(I32)

    def bit_step(bi, cand):
        trial = cand + jnp.left_shift(jnp.int32(1), 30 - bi)
        return jnp.where(count_ge(trial) >= n_sel, trial, cand)

    tau = lax.fori_loop(0, 31, bit_step, cand)
    n_ge = count_ge(tau)
    need = n_sel - count_ge(tau + 1)

    def tie_cut():
        def step(bi, jm):
            trial = jm + jnp.left_shift(jnp.int32(1), (seq.bit_length() - 2) - bi)
            f = count(lambda c, key: jnp.where(key == tau, krow + c * tq, seq) < trial)
            return jnp.where(f < need, trial, jm)
        return lax.fori_loop(0, seq.bit_length() - 1, step, jnp.zeros((1, tq), I32))

    has_tie = jnp.max(n_ge) > n_sel
    jm = lax.cond(has_tie, tie_cut, lambda: jnp.full((1, tq), seq, I32))

    def bias_chunk(c, carry):
        key = key_ref[c]
        kidx = krow + c * tq
        tied = jnp.where(kidx <= jm, 0.0, NEG_BIG)
        above = jnp.where(key > tau, 0.0, NEG_BIG)
        bias_ref[c] = jnp.where(kidx <= qcol, jnp.where(key == tau, tied, above), NEG_BIG)
        return carry

    lax.fori_loop(0, nk, bias_chunk, 0)

    m_ref[...] = jnp.full(m_ref.shape, NEG_BIG, F32)
    l_ref[...] = jnp.zeros(l_ref.shape, F32)
    acc_ref[...] = jnp.zeros(acc_ref.shape, F32)

    def attn_chunk(c, carry):
        rows = chunk_rows(c)
        for hd in range(N_HEADS):
            p = hd // 2
            kp = k_ref[0, rows, p * LANES:(p + 1) * LANES]
            qh = q_ref[0, :, hd * LANES:(hd + 1) * LANES]
            lm_ref[hd] = _nt_dot(kp, qh) + bias_ref[c]
        for hd in range(N_HEADS):
            m_old = m_ref[hd]
            m_new = jnp.maximum(m_old, jnp.max(lm_ref[hd], axis=0, keepdims=True))
            pr = jnp.exp(lm_ref[hd] - m_new)
            a = jnp.exp(m_old - m_new)
            l_ref[hd] = a * l_ref[hd] + jnp.sum(pr, axis=0, keepdims=True)
            m_ref[hd] = m_new
            a_ref[hd] = a
            p_ref[hd] = pr.astype(BF16)
        for hd in range(N_HEADS):
            vth = vt_ref[hd * HEAD_DIM:(hd + 1) * HEAD_DIM, rows]
            acc_ref[hd] = acc_ref[hd] * a_ref[hd] + _dot(vth, p_ref[hd])
        return carry

    lax.fori_loop(0, nk, attn_chunk, 0)

    ot = jnp.concatenate([acc_ref[hd] / l_ref[hd] for hd in range(N_HEADS)], axis=0)
    ms = jnp.mean(ot * ot, axis=0, keepdims=True)
    ot = ot * lax.rsqrt(ms + NORM_EPS)
    o_ref[0] = (jnp.transpose(ot) * og_ref[...]).astype(BF16)


def _attention(q3, k3, vt, qi3, kk3, wit, og, tq):
    bsz, seq, w = k3.shape
    n_sel = min(TOPK_KEYS, seq // 4)
    nc = seq // tq
    qblk = lambda bb, i: (bb, i, 0)
    full = lambda bb, i: (bb, 0, 0)
    return pl.pallas_call(
        functools.partial(_attn_kernel, tq=tq, n_sel=n_sel, seq=seq),
        grid=(bsz, nc),
        in_specs=[
            pl.BlockSpec((1, tq, 2 * w), qblk),
            pl.BlockSpec((1, tq, 2 * w), qblk),
            pl.BlockSpec((SUBLANES, tq), lambda bb, i: (0, bb * nc + i)),
            pl.BlockSpec((1, seq, w), full),
            pl.BlockSpec((w, seq), lambda bb, i: (0, bb)),
            pl.BlockSpec((1, seq, LANES), full),
            pl.BlockSpec(og.shape, lambda bb, i: (0, 0)),
        ],
        out_specs=pl.BlockSpec((1, tq, w), qblk),
        out_shape=jax.ShapeDtypeStruct((bsz, seq, w), BF16),
        scratch_shapes=[
            pltpu.VMEM((nc, tq, tq), I32),
            pltpu.VMEM((nc, tq, tq), F32),
            pltpu.VMEM((N_HEADS, 1, tq), F32),
            pltpu.VMEM((N_HEADS, 1, tq), F32),
            pltpu.VMEM((N_HEADS, 1, tq), F32),
            pltpu.VMEM((N_HEADS, HEAD_DIM, tq), F32),
            pltpu.VMEM((N_HEADS, tq, tq), F32),
            pltpu.VMEM((N_HEADS, tq, tq), BF16),
        ],
        compiler_params=_cparams(("arbitrary", "arbitrary")),
        name="dsa_attention",
    )(q3, qi3, wit, k3, vt, kk3, og)


def _outproj_kernel(an_ref, cn_ref, x_ref, g1_ref, sh_ref, sc_ref, woa_ref, woc_ref, wq_ref,
                    x1_ref, h2_ref, pq_ref):
    y = _dot(an_ref[...], woa_ref[...]) + _dot(cn_ref[...], woc_ref[...])
    x1 = x_ref[...] + g1_ref[0] * y
    x1_ref[...] = x1
    ms = jnp.mean(x1 * x1, axis=-1, keepdims=True)
    h2 = (x1 * lax.rsqrt(ms + NORM_EPS) * (1.0 + sc_ref[0]) + sh_ref[0]).astype(BF16)
    h2_ref[...] = h2
    pq_ref[...] = _dot(h2, wq_ref[...]).astype(BF16)


def _outproj(an, cn, x2, g1, sh2, sc2, woa, woc, wq, seq, tm):
    n, d = x2.shape
    per_b = seq // tm
    row = lambda i: (i, 0)
    const = lambda i: (0, 0)
    bat = lambda i: (i // per_b, 0, 0)
    return pl.pallas_call(
        _outproj_kernel,
        grid=(n // tm,),
        in_specs=[
            pl.BlockSpec((tm, an.shape[1]), row),
            pl.BlockSpec((tm, cn.shape[1]), row),
            pl.BlockSpec((tm, d), row),
            pl.BlockSpec((1, 1, d), bat),
            pl.BlockSpec((1, 1, d), bat),
            pl.BlockSpec((1, 1, d), bat),
            pl.BlockSpec(woa.shape, const),
            pl.BlockSpec(woc.shape, const),
            pl.BlockSpec(wq.shape, const),
        ],
        out_specs=[pl.BlockSpec((tm, d), row), pl.BlockSpec((tm, d), row),
                   pl.BlockSpec((tm, wq.shape[1]), row)],
        out_shape=[jax.ShapeDtypeStruct((n, d), F32), jax.ShapeDtypeStruct((n, d), BF16),
                   jax.ShapeDtypeStruct((n, wq.shape[1]), BF16)],
        compiler_params=_cparams(("arbitrary",)),
        name="outproj",
    )(an, cn, x2, g1, sh2, sc2, woa, woc, wq)


def _top_rows(s, rank=None, ids=None):
    if rank is None:
        rank = lax.broadcasted_iota(I32, s.shape, 0)
    vals, picks = [], []
    for _ in range(PEER_TOPK):
        m = jnp.max(s, axis=0, keepdims=True)
        pos = jnp.min(jnp.where(s == m, rank, 2 ** 30), axis=0, keepdims=True)
        hit = rank == pos
        vals.append(m)
        picks.append(pos if ids is None else jnp.max(jnp.where(hit, ids, -1), axis=0, keepdims=True))
        s = jnp.where(hit, -jnp.inf, s)
    return jnp.concatenate(vals, axis=0), jnp.concatenate(picks, axis=0)


def _pair_candidates(v1, i1, v2, i2):
    tr = v1.shape[1]
    k = PEER_TOPK
    r8 = lax.broadcasted_iota(I32, (SUBLANES, tr), 0)
    rk = lax.broadcasted_iota(I32, (k, tr), 0)
    sums = [v1 + v2[0:1]]
    eids = [i1 * PEER_KEYS + i2[0:1]]
    order = [rk * k]
    for b in range(1, SUBLANES):
        ok = r8 < k // (b + 1)
        sums.append(jnp.where(ok, v1[0:SUBLANES] + v2[b:b + 1], -jnp.inf))
        eids.append(i1[0:SUBLANES] * PEER_KEYS + i2[b:b + 1])
        order.append(r8 * k + b)
    sums.append(v1[0:1] + v2[SUBLANES:k])
    eids.append(i1[0:1] * PEER_KEYS + i2[SUBLANES:k])
    order.append(r8 + SUBLANES)
    return jnp.concatenate(sums, axis=0), jnp.concatenate(eids, axis=0), jnp.concatenate(order, axis=0)


def _route_kernel(pq_ref, sk_ref, a_ref, b_ref, g_ref, e_scr, g_scr):
    h = pl.program_id(1)
    half = PEER_KEYS
    s1 = _nt_dot(sk_ref[0, 0], pq_ref[:, :half])
    s2 = _nt_dot(sk_ref[0, 1], pq_ref[:, half:])
    v1, i1 = _top_rows(s1)
    v2, i2 = _top_rows(s2)
    cand, cid, order = _pair_candidates(v1, i1, v2, i2)
    ts, e = _top_rows(cand, order, cid)
    ex = jnp.exp(ts - ts[0:1])
    g = ex / jnp.sum(ex, axis=0, keepdims=True)
    rows = pl.ds(pl.multiple_of(h * PEER_TOPK, PEER_TOPK), PEER_TOPK)
    e_scr[rows, :] = e
    g_scr[rows, :] = g

    @pl.when(h == PEER_HEADS - 1)
    def _():
        e_all = e_scr[...]
        a_ref[...] = jnp.transpose(jnp.right_shift(e_all, 7).astype(F32))
        b_ref[...] = jnp.transpose(jnp.bitwise_and(e_all, PEER_KEYS - 1).astype(F32))
        g_ref[...] = jnp.transpose(g_scr[...])


def _route(pq, sk, tr):
    n = pq.shape[0]
    qd = 2 * PEER_KEYS
    out = jax.ShapeDtypeStruct((n, PEER_PAIRS), F32)
    oblk = pl.BlockSpec((tr, PEER_PAIRS), lambda i, h: (i, 0))
    return pl.pallas_call(
        _route_kernel,
        grid=(n // tr, PEER_HEADS),
        in_specs=[
            pl.BlockSpec((tr, qd), lambda i, h: (i, h)),
            pl.BlockSpec((1, 2, PEER_KEYS, PEER_KEYS), lambda i, h: (h, 0, 0, 0)),
        ],
        out_specs=[oblk, oblk, oblk],
        out_shape=[out, out, out],
        scratch_shapes=[pltpu.VMEM((PEER_PAIRS, tr), I32), pltpu.VMEM((PEER_PAIRS, tr), F32)],
        compiler_params=_cparams(("arbitrary", "arbitrary")),
        name="peer_route",
    )(pq, sk)


C_HALF = PEER_KEYS // 2
C_PITCH = C_HALF + SUBLANES
C_UNROLL = 16
HI16 = -65536


def _gelu(x):
    return 0.5 * x * (1.0 + lax.erf(x * (2.0 ** -0.5)))


def _bf16_bits(x):
    return pltpu.bitcast(x.astype(BF16).astype(F32), I32)


def _peer_kernel(h2_ref, a_ref, b_ref, g_ref, ulo_ref, uhi_ref, vlo_ref, vhi_ref, x1_ref, g2_ref, o_ref,
                 c_scr, acc_ref, *, tm, ec):
    j = pl.program_id(1)
    nj = pl.num_programs(1)
    k1 = ec // 2 // PEER_KEYS

    @pl.when(j == 0)
    def _():
        srow = lax.broadcasted_iota(I32, (PEER_KEYS, PEER_PAIRS), 0).astype(F32)

        def tokens(tb, carry):
            for u in range(C_UNROLL):
                t = tb * C_UNROLL + u
                a_row = a_ref[pl.ds(t, 1), :]
                b_row = b_ref[pl.ds(t, 1), :]
                g_row = g_ref[pl.ds(t, 1), :]
                at = jnp.where(srow == a_row, g_row, 0.0).astype(BF16)
                bt = jnp.where(srow == b_row, 1.0, 0.0).astype(BF16)
                ct = _nt_dot(at, bt)
                word = (lax.shift_right_logical(_bf16_bits(ct[:C_HALF]), 16)
                        | (_bf16_bits(ct[C_HALF:]) & HI16))
                c_scr[pl.ds(pl.multiple_of(t * C_PITCH, SUBLANES), C_HALF), :] = word
            return carry

        lax.fori_loop(0, tm // C_UNROLL, tokens, 0)
        acc_ref[...] = jnp.zeros(acc_ref.shape, F32)

    h2 = h2_ref[...]
    act_lo = _gelu(_nt_dot(h2, ulo_ref[...]))
    act_hi = _gelu(_nt_dot(h2, uhi_ref[...]))
    p_lo, p_hi = [], []
    for ii in range(k1):
        word = c_scr[pl.ds(j * k1 + ii, tm, stride=C_PITCH), :]
        c_lo = pltpu.bitcast(lax.shift_left(word, 16), F32)
        c_hi = pltpu.bitcast(word & HI16, F32)
        cols = slice(ii * PEER_KEYS, (ii + 1) * PEER_KEYS)
        p_lo.append((act_lo[:, cols] * c_lo).astype(BF16))
        p_hi.append((act_hi[:, cols] * c_hi).astype(BF16))
    acc_ref[...] += (_dot(jnp.concatenate(p_lo, axis=1), vlo_ref[...])
                     + _dot(jnp.concatenate(p_hi, axis=1), vhi_ref[...]))

    @pl.when(j == nj - 1)
    def _():
        o_ref[...] = x1_ref[...] + g2_ref[0] * acc_ref[...]


def _peer(h2, a, b, g, u, v, x1, g2, seq, tm, ec):
    n, d = x1.shape
    ne = u.shape[0]
    per_b = seq // tm
    eb = ec // 2
    nhalf = ne // 2 // eb
    row = lambda i, j: (i, 0)
    lo = lambda i, j: (j, 0)
    hi = lambda i, j: (j + nhalf, 0)
    return pl.pallas_call(
        functools.partial(_peer_kernel, tm=tm, ec=ec),
        grid=(n // tm, ne // ec),
        in_specs=[
            pl.BlockSpec((tm, d), row),
            pl.BlockSpec((tm, PEER_PAIRS), row),
            pl.BlockSpec((tm, PEER_PAIRS), row),
            pl.BlockSpec((tm, PEER_PAIRS), row),
            pl.BlockSpec((eb, d), lo),
            pl.BlockSpec((eb, d), hi),
            pl.BlockSpec((eb, d), lo),
            pl.BlockSpec((eb, d), hi),
            pl.BlockSpec((tm, d), row),
            pl.BlockSpec((1, 1, d), lambda i, j: (i // per_b, 0, 0)),
        ],
        out_specs=pl.BlockSpec((tm, d), row),
        out_shape=jax.ShapeDtypeStruct((n, d), F32),
        scratch_shapes=[pltpu.VMEM((tm * C_PITCH, PEER_KEYS), I32), pltpu.VMEM((tm, d), F32)],
        compiler_params=_cparams(("arbitrary", "arbitrary")),
        name="peer_experts",
    )(h2, a, b, g, u, u, v, v, x1, g2)


def _rope_lane_tables(positions):
    half = ROPE_DIM // 2
    inv_freq = jnp.power(jnp.float32(ROPE_THETA), -jnp.arange(0, ROPE_DIM, 2, dtype=F32) / ROPE_DIM)
    ang = positions.astype(F32).reshape(-1, 1) * inv_freq
    cos, sin = jnp.cos(ang), jnp.sin(ang)
    j = jnp.arange(LANES) % HEAD_DIM
    cos_l, sin_l = cos[:, j % half], sin[:, j % half]
    rc = jnp.where(j < ROPE_DIM, cos_l, 1.0)
    rs1 = jnp.where(j < half, -sin_l, 0.0)
    rs2 = jnp.where((j >= half) & (j < ROPE_DIM), sin_l, 0.0)
    return rc, rs1, rs2


def _tile_size(total, want):
    t = min(want, total)
    assert total % t == 0
    return t


def kernel(x, c, positions, w_ada, b_ada, w_in, q_norm_g, k_norm_g, idx_k_ln_g, idx_k_ln_b,
           conv_dw, conv_b, conv_ln_g, conv_ln_b, out_g_attn, out_g_conv, w_out, peer_wq,
           peer_sub_keys, peer_u, peer_v):
    bsz, seq, d = x.shape
    depth = w_ada.shape[0]
    n = bsz * seq
    w = ATTN_WIDTH
    cw = d - w
    assert bsz <= SUBLANES and cw % LANES == 0

    tm = _tile_size(seq, 512)
    tq = _tile_size(seq, 256)
    tt = _tile_size(seq, 512)
    tr = _tile_size(seq, 256)
    tp = _tile_size(seq, 512)
    ec = 1024

    c_pad = jnp.zeros((SUBLANES, d), F32).at[:bsz].set(c)
    mod = _ada(c_pad, w_ada, b_ada)[:, :bsz]
    rc, rs1, rs2 = _rope_lane_tables(positions)
    hd_id = jnp.arange(w) // HEAD_DIM
    gsum = (hd_id[:, None] == hd_id[None, :]).astype(BF16)

    o_q, o_k, o_v, o_qi, o_ki, o_wi = 0, w, 2 * w, 3 * w, 4 * w, 4 * w + IDX_DIM
    o_cu = o_wi + IDX_HEADS

    x2 = x.reshape(n, d)
    for l in range(depth):
        sh1, sc1, g1, sh2, sc2, g2 = [mod[l, :, i * d:(i + 1) * d].reshape(bsz, 1, d) for i in range(6)]
        wl = w_in[l]
        wa = wl[:, o_q:o_ki].astype(BF16)
        wki = wl[:, o_ki:o_wi]
        wk = jnp.concatenate([wki, wki], axis=1).astype(BF16)
        ww = jnp.pad(wl[:, o_wi:o_cu], ((0, 0), (0, LANES - IDX_HEADS))).astype(BF16)
        wc = wl[:, o_cu:].astype(BF16)
        tile_h = lambda g_: jnp.tile(g_, N_HEADS).reshape(1, w)
        dup = lambda g_: jnp.concatenate([g_, g_]).reshape(1, LANES)
        q, k, vt, qi, kk, wit, hg = _inproj(
            x2, sh1, sc1, wa, wk, ww, wc, tile_h(q_norm_g[l]), tile_h(k_norm_g[l]),
            dup(idx_k_ln_g[l]), dup(idx_k_ln_b[l]), rc, rs1, rs2, gsum, seq, tm)

        r3 = lambda t: t.reshape(bsz, seq, t.shape[-1])
        an = _attention(r3(q), r3(k), vt, r3(qi), r3(kk), wit, out_g_attn[l].reshape(1, w), tq)
        w_pad = jnp.pad(conv_dw[l], ((0, CONV_HALO - CONV_KERNEL), (0, 0)))
        cn = _conv(r3(hg), w_pad, conv_b[l].reshape(1, cw), conv_ln_g[l].reshape(1, cw),
                   conv_ln_b[l].reshape(1, cw), out_g_conv[l].reshape(1, cw), tt)

        wo = w_out[l].astype(BF16)
        x1, h2, pq = _outproj(an.reshape(n, w), cn.reshape(n, cw), x2, g1, sh2, sc2,
                              wo[:w], wo[w:], peer_wq[l].astype(BF16), seq, tm)
        a, b, g = _route(pq, peer_sub_keys[l].astype(BF16), tr)
        x2 = _peer(h2, a, b, g, peer_u[l].astype(BF16), peer_v[l].astype(BF16), x1, g2, seq, tp, ec)
    return x2.reshape(bsz, seq, d)
```

```python
import functools

import jax
import jax.numpy as jnp
from jax import lax
from jax.experimental import pallas as pl
from jax.experimental.pallas import tpu as pltpu

F32 = jnp.float32
BF16 = jnp.bfloat16
I32 = jnp.int32

N_HEADS = 8
HEAD_DIM = 64
ATTN_WIDTH = N_HEADS * HEAD_DIM
CONV_KERNEL = 31
IDX_HEADS = 8
IDX_DIM = 64
TOPK_KEYS = 256
ROPE_THETA = 500000.0
ROPE_DIM = HEAD_DIM // 4
PEER_HEADS = 8
PEER_KEYS = 128
PEER_TOPK = 16
PEER_PAIRS = PEER_HEADS * PEER_TOPK
NORM_EPS = 1e-6
LOG2_E = 1.4426950408889634
Q_SCALE = HEAD_DIM ** -0.5 * LOG2_E

LANES = 128
SUBLANES = 8
VMEM_LIMIT = 56 * 1024 * 1024

PACK16 = 2 * SUBLANES

NEG_BIG = -1e30
INT_MIN = -2 ** 31
INT16_MIN = -2 ** 15


def _cparams(sem):
    return pltpu.CompilerParams(dimension_semantics=sem, vmem_limit_bytes=VMEM_LIMIT)


def _nt_dot(a, b):
    return lax.dot_general(a, b, (((1,), (1,)), ((), ())), preferred_element_type=F32)


def _dot(a, b):
    return jnp.dot(a, b, preferred_element_type=F32)


def _ada_kernel(c_ref, w_ref, b_ref, o_ref):
    c = c_ref[...]
    ca = c * jax.nn.sigmoid(c)
    o_ref[0] = _dot(ca, w_ref[0]) + b_ref[0]


def _ada(c_pad, w_ada, b_ada):
    depth, d, n6 = w_ada.shape
    tn = n6 // 4
    return pl.pallas_call(
        _ada_kernel,
        grid=(depth, n6 // tn),
        in_specs=[
            pl.BlockSpec((SUBLANES, d), lambda l, j: (0, 0)),
            pl.BlockSpec((1, d, tn), lambda l, j: (l, 0, j)),
            pl.BlockSpec((1, 1, tn), lambda l, j: (l, 0, j)),
        ],
        out_specs=pl.BlockSpec((1, SUBLANES, tn), lambda l, j: (l, 0, j)),
        out_shape=jax.ShapeDtypeStruct((depth, SUBLANES, n6), F32),
        compiler_params=_cparams(("arbitrary", "arbitrary")),
        name="adaln",
    )(c_pad, w_ada, b_ada.reshape(depth, 1, n6))


def _rope(x, c, s1, s2):
    w = x.shape[-1]
    half = ROPE_DIM // 2
    return x * c + pltpu.roll(x, w - half, 1) * s1 + pltpu.roll(x, half, 1) * s2


def _split_hi_lo(x):
    hi = x.astype(BF16)
    lo = (x - hi.astype(F32)).astype(BF16)
    return hi, lo


def _inproj_kernel(x_ref, sh_ref, sc_ref, wa_ref, wk_ref, ww_ref, wc_ref, gq_ref, gk_ref,
                   lng_ref, lnb_ref, rc_ref, rs1_ref, rs2_ref, gsum_ref,
                   q_ref, k_ref, vt_ref, qi_ref, kk_ref, wit_ref, hg_ref):
    x = x_ref[...]
    ms = jnp.mean(x * x, axis=-1, keepdims=True)
    h = x * lax.rsqrt(ms + NORM_EPS) * (1.0 + sc_ref[0]) + sh_ref[0]
    hb = h.astype(BF16)

    rc1, rs11, rs21 = rc_ref[...], rs1_ref[...], rs2_ref[...]
    reps = ATTN_WIDTH // LANES
    rc = jnp.concatenate([rc1] * reps, axis=1)
    rs1 = jnp.concatenate([rs11] * reps, axis=1)
    rs2 = jnp.concatenate([rs21] * reps, axis=1)
    gsum = gsum_ref[...]

    def head_norm_rope(t, gain):
        hi, lo = _split_hi_lo(t * t)
        ssq = _dot(hi, gsum) + _dot(lo, gsum)
        tn = t * lax.rsqrt(ssq * (1.0 / HEAD_DIM) + NORM_EPS) * gain
        return _rope(tn, rc, rs1, rs2)

    lo_half = lax.broadcasted_iota(I32, (x.shape[0], LANES), 1) < HEAD_DIM

    def one_head_per_group(t):
        parts = []
        for p in range(reps):
            g = t[:, p * LANES:(p + 1) * LANES]
            parts += [jnp.where(lo_half, g, 0.0), jnp.where(lo_half, 0.0, g)]
        return jnp.concatenate(parts, axis=1).astype(BF16)

    w = ATTN_WIDTH
    pa = _dot(hb, wa_ref[...])
    q_ref[...] = one_head_per_group(head_norm_rope(pa[:, 0:w], gq_ref[...]) * Q_SCALE)
    k_ref[...] = head_norm_rope(pa[:, w:2 * w], gk_ref[...]).astype(BF16)
    vt_ref[...] = jnp.transpose(pa[:, 2 * w:3 * w]).astype(BF16)
    qi_ref[...] = one_head_per_group(_rope(pa[:, 3 * w:4 * w], rc, rs1, rs2) * (IDX_DIM ** -0.5))

    pk = _dot(hb, wk_ref[...])
    mu = jnp.mean(pk, axis=-1, keepdims=True)
    var = jnp.mean(jnp.square(pk - mu), axis=-1, keepdims=True)
    kn = (pk - mu) * lax.rsqrt(var + NORM_EPS) * lng_ref[...] + lnb_ref[...]
    kk_ref[...] = _rope(kn, rc1, rs11, rs21).astype(BF16)

    wit = jnp.transpose(_dot(hb, ww_ref[...]) * (IDX_HEADS ** -0.5))
    wit_ref[...] = wit[:IDX_HEADS]

    pc = _dot(hb, wc_ref[...])
    cw = pc.shape[1] // 2
    hg_ref[...] = pc[:, :cw] * jax.nn.sigmoid(pc[:, cw:])


def _inproj(x2, sh, sc, wa, wk, ww, wc, gq, gk, lng, lnb, rc, rs1, rs2, gsum, seq, tm):
    n, d = x2.shape
    per_b = seq // tm
    row = lambda i: (i, 0)
    const = lambda i: (0, 0)
    bat = lambda i: (i // per_b, 0, 0)
    cw = wc.shape[1] // 2
    outs = [
        jax.ShapeDtypeStruct((n, 2 * ATTN_WIDTH), BF16),
        jax.ShapeDtypeStruct((n, ATTN_WIDTH), BF16),
        jax.ShapeDtypeStruct((ATTN_WIDTH, n), BF16),
        jax.ShapeDtypeStruct((n, 2 * ATTN_WIDTH), BF16),
        jax.ShapeDtypeStruct((n, LANES), BF16),
        jax.ShapeDtypeStruct((IDX_HEADS, n), F32),
        jax.ShapeDtypeStruct((n, cw), F32),
    ]
    col = lambda i: (0, i)
    token_minor = (False, False, True, False, False, True, False)
    out_specs = [pl.BlockSpec((o.shape[0], tm), col) if tmin else pl.BlockSpec((tm, o.shape[1]), row)
                 for o, tmin in zip(outs, token_minor)]
    return pl.pallas_call(
        _inproj_kernel,
        grid=(n // tm,),
        in_specs=[
            pl.BlockSpec((tm, d), row),
            pl.BlockSpec((1, 1, d), bat),
            pl.BlockSpec((1, 1, d), bat),
            pl.BlockSpec(wa.shape, const),
            pl.BlockSpec(wk.shape, const),
            pl.BlockSpec(ww.shape, const),
            pl.BlockSpec(wc.shape, const),
            pl.BlockSpec(gq.shape, const),
            pl.BlockSpec(gk.shape, const),
            pl.BlockSpec(lng.shape, const),
            pl.BlockSpec(lnb.shape, const),
            pl.BlockSpec((tm, LANES), row),
            pl.BlockSpec((tm, LANES), row),
            pl.BlockSpec((tm, LANES), row),
            pl.BlockSpec(gsum.shape, const),
        ],
        out_specs=out_specs,
        out_shape=outs,
        compiler_params=_cparams(("arbitrary",)),
        name="inproj",
    )(x2, sh, sc, wa, wk, ww, wc, gq, gk, lng, lnb, rc, rs1, rs2, gsum)


CONV_HALO = 32
CONV_ROWS = 64


def _conv_kernel(cur_ref, halo_ref, w_ref, b_ref, lg_ref, lb_ref, og_ref, o_ref, ext_ref, *, tt):
    i = pl.program_id(1)
    ext_ref[0:CONV_HALO, :] = jnp.where(i > 0, halo_ref[0], 0.0)
    ext_ref[CONV_HALO:CONV_HALO + tt, :] = cur_ref[0]
    lead = CONV_HALO - (CONV_KERNEL - 1)
    for r0 in range(0, tt, CONV_ROWS):
        acc = jnp.zeros((CONV_ROWS, ext_ref.shape[1]), F32)
        for j in range(CONV_KERNEL):
            acc = acc + ext_ref[r0 + lead + j:r0 + lead + j + CONV_ROWS, :] * w_ref[j:j + 1, :]
        hcv = acc + b_ref[...]
        mu = jnp.mean(hcv, axis=-1, keepdims=True)
        var = jnp.mean(jnp.square(hcv - mu), axis=-1, keepdims=True)
        y = (hcv - mu) * lax.rsqrt(var + NORM_EPS) * lg_ref[...] + lb_ref[...]
        y = y * jax.nn.sigmoid(y)
        ms = jnp.mean(y * y, axis=-1, keepdims=True)
        o_ref[0, r0:r0 + CONV_ROWS, :] = (y * lax.rsqrt(ms + NORM_EPS) * og_ref[...]).astype(BF16)


def _conv(hg3, w_pad, b, lg, lb, og, tt):
    bsz, seq, cw = hg3.shape
    hb = tt // CONV_HALO
    const = lambda bb, i: (0, 0)
    return pl.pallas_call(
        functools.partial(_conv_kernel, tt=tt),
        grid=(bsz, seq // tt),
        in_specs=[
            pl.BlockSpec((1, tt, cw), lambda bb, i: (bb, i, 0)),
            pl.BlockSpec((1, CONV_HALO, cw), lambda bb, i: (bb, jnp.maximum(i * hb - 1, 0), 0)),
            pl.BlockSpec(w_pad.shape, const),
            pl.BlockSpec(b.shape, const),
            pl.BlockSpec(lg.shape, const),
            pl.BlockSpec(lb.shape, const),
            pl.BlockSpec(og.shape, const),
        ],
        out_specs=pl.BlockSpec((1, tt, cw), lambda bb, i: (bb, i, 0)),
        out_shape=jax.ShapeDtypeStruct((bsz, seq, cw), BF16),
        scratch_shapes=[pltpu.VMEM((CONV_HALO + tt, cw), F32)],
        compiler_params=_cparams(("arbitrary", "arbitrary")),
        name="conv",
    )(hg3, hg3, w_pad, b, lg, lb, og)


def _attn_kernel(q_ref, qi_ref, wit_ref, k_ref, vt_ref, kk_ref, og_ref, o_ref,
                 key_ref, k16_ref, bias_ref, m_ref, l_ref, a_ref, acc_ref, lm_ref, p_ref, *, tq, n_sel, seq):
    i = pl.program_id(1)
    nk = i + 1
    krow = lax.broadcasted_iota(I32, (tq, tq), 0)
    qcol = lax.broadcasted_iota(I32, (tq, tq), 1) + i * tq

    def chunk_rows(c):
        return pl.ds(pl.multiple_of(c * tq, tq), tq)

    wit = wit_ref[...]

    def score_chunk(c, carry):
        kic = kk_ref[0, chunk_rows(c), :]
        s = jnp.zeros((tq, tq), F32)
        for h in range(IDX_HEADS):
            qh = qi_ref[0, :, h * LANES:(h + 1) * LANES]
            s = s + wit[h:h + 1, :] * jnp.maximum(_nt_dot(kic, qh), 0.0)
        s = jnp.where(krow + c * tq <= qcol, s, -jnp.inf)
        bits = pltpu.bitcast(s, I32)
        key = jnp.where(bits < 0, bits ^ jnp.int32(0x7FFFFFFF), bits)
        key_ref[c] = jnp.where(s == 0.0, 0, key)
        return carry

    lax.fori_loop(0, nk, score_chunk, 0)

    def count(pred_fn):
        def body(c, acc):
            hit = jnp.where(pred_fn(c, key_ref[c]), 1, 0)
            return acc + jnp.sum(hit.reshape(tq // SUBLANES, SUBLANES, tq), axis=0)
        acc = lax.fori_loop(0, nk, body, jnp.zeros((SUBLANES, tq), I32))
        return jnp.sum(acc, axis=0, keepdims=True)

    def count_ge(thr):
        return count(lambda c, key: key >= thr)

    grp = tq // PACK16
    one_h, zero_h = jnp.ones((), BF16), jnp.zeros((), BF16)

    def count16(thr):
        thr16 = jnp.broadcast_to(thr, (PACK16, tq)).astype(jnp.int16)

        def body(c, acc):
            hit = jnp.where(k16_ref[c].reshape(grp, PACK16, tq) >= thr16[None], one_h, zero_h)
            parts = [hit[r] for r in range(grp)]
            while len(parts) > 1:
                parts = [parts[r] + parts[r + 1] for r in range(0, len(parts), 2)]
            return acc + parts[0].astype(F32)

        acc = lax.fori_loop(0, nk, body, jnp.zeros((PACK16, tq), F32))
        return jnp.sum(acc, axis=0, keepdims=True).astype(I32)

    def search16():
        def step(bi, cand):
            trial = cand + jnp.left_shift(jnp.int32(1), 15 - bi)
            return jnp.where(count16(trial) >= n_sel, trial, cand)
        return lax.fori_loop(0, 16, step, jnp.full((1, tq), INT16_MIN, I32))

    def store_upper(c, carry):
        k16_ref[c] = lax.shift_right_arithmetic(key_ref[c], 16).astype(jnp.int16)
        return carry

    lax.fori_loop(0, nk, store_upper, 0)
    tau_hi = search16()

    def store_lower(c, carry):
        key = key_ref[c]
        hi = lax.shift_right_arithmetic(key, 16)
        lo = (key & 0xFFFF) + INT16_MIN
        k16_ref[c] = jnp.where(hi == tau_hi, lo, jnp.where(hi > tau_hi, -INT16_MIN - 1, INT16_MIN)
                               ).astype(jnp.int16)
        return carry

    lax.fori_loop(0, nk, store_lower, 0)
    tau = jnp.left_shift(tau_hi, 16) | (search16() - INT16_MIN)
    n_ge = count_ge(tau)
    need = n_sel - count_ge(tau + 1)

    def tie_cut():
        def step(bi, jm):
            trial = jm + jnp.left_shift(jnp.int32(1), (seq.bit_length() - 2) - bi)
            f = count(lambda c, key: jnp.where(key == tau, krow + c * tq, seq) < trial)
            return jnp.where(f < need, trial, jm)
        return lax.fori_loop(0, seq.bit_length() - 1, step, jnp.zeros((1, tq), I32))

    has_tie = jnp.max(n_ge) > n_sel
    jm = lax.cond(has_tie, tie_cut, lambda: jnp.full((1, tq), seq, I32))

    def bias_chunk(c, carry):
        key = key_ref[c]
        kidx = krow + c * tq
        tied = jnp.where(kidx <= jm, 0.0, NEG_BIG)
        above = jnp.where(key > tau, 0.0, NEG_BIG)
        bias_ref[c] = jnp.where(kidx <= qcol, jnp.where(key == tau, tied, above), NEG_BIG)
        return carry

    lax.fori_loop(0, nk, bias_chunk, 0)

    m_ref[...] = jnp.full(m_ref.shape, NEG_BIG, F32)
    l_ref[...] = jnp.zeros(l_ref.shape, F32)
    acc_ref[...] = jnp.zeros(acc_ref.shape, F32)

    def attn_chunk(c, carry):
        rows = chunk_rows(c)
        for hd in range(N_HEADS):
            p = hd // 2
            kp = k_ref[0, rows, p * LANES:(p + 1) * LANES]
            qh = q_ref[0, :, hd * LANES:(hd + 1) * LANES]
            lm_ref[hd] = _nt_dot(kp, qh) + bias_ref[c]
        for hd in range(N_HEADS):
            m_old = m_ref[hd]
            m_new = jnp.maximum(m_old, jnp.max(lm_ref[hd], axis=0, keepdims=True))
            pr = jnp.exp2(lm_ref[hd] - m_new)
            a = jnp.exp2(m_old - m_new)
            l_ref[hd] = a * l_ref[hd] + jnp.sum(pr, axis=0, keepdims=True)
            m_ref[hd] = m_new
            a_ref[hd] = a
            p_ref[hd] = pr.astype(BF16)
        for hd in range(N_HEADS):
            vth = vt_ref[hd * HEAD_DIM:(hd + 1) * HEAD_DIM, rows]
            acc_ref[hd] = acc_ref[hd] * a_ref[hd] + _dot(vth, p_ref[hd])
        return carry

    lax.fori_loop(0, nk, attn_chunk, 0)

    ot = jnp.concatenate([acc_ref[hd] / l_ref[hd] for hd in range(N_HEADS)], axis=0)
    ms = jnp.mean(ot * ot, axis=0, keepdims=True)
    ot = ot * lax.rsqrt(ms + NORM_EPS)
    o_ref[0] = (jnp.transpose(ot) * og_ref[...]).astype(BF16)


def _attention(q3, k3, vt, qi3, kk3, wit, og, tq):
    bsz, seq, w = k3.shape
    n_sel = min(TOPK_KEYS, seq // 4)
    nc = seq // tq
    qblk = lambda bb, i: (bb, i, 0)
    full = lambda bb, i: (bb, 0, 0)
    return pl.pallas_call(
        functools.partial(_attn_kernel, tq=tq, n_sel=n_sel, seq=seq),
        grid=(bsz, nc),
        in_specs=[
            pl.BlockSpec((1, tq, 2 * w), qblk),
            pl.BlockSpec((1, tq, 2 * w), qblk),
            pl.BlockSpec((SUBLANES, tq), lambda bb, i: (0, bb * nc + i)),
            pl.BlockSpec((1, seq, w), full),
            pl.BlockSpec((w, seq), lambda bb, i: (0, bb)),
            pl.BlockSpec((1, seq, LANES), full),
            pl.BlockSpec(og.shape, lambda bb, i: (0, 0)),
        ],
        out_specs=pl.BlockSpec((1, tq, w), qblk),
        out_shape=jax.ShapeDtypeStruct((bsz, seq, w), BF16),
        scratch_shapes=[
            pltpu.VMEM((nc, tq, tq), I32),
            pltpu.VMEM((nc, tq, tq), jnp.int16),
            pltpu.VMEM((nc, tq, tq), F32),
            pltpu.VMEM((N_HEADS, 1, tq), F32),
            pltpu.VMEM((N_HEADS, 1, tq), F32),
            pltpu.VMEM((N_HEADS, 1, tq), F32),
            pltpu.VMEM((N_HEADS, HEAD_DIM, tq), F32),
            pltpu.VMEM((N_HEADS, tq, tq), F32),
            pltpu.VMEM((N_HEADS, tq, tq), BF16),
        ],
        compiler_params=_cparams(("arbitrary", "arbitrary")),
        name="dsa_attention",
    )(q3, qi3, wit, k3, vt, kk3, og)


def _outproj_kernel(an_ref, cn_ref, x_ref, g1_ref, sh_ref, sc_ref, woa_ref, woc_ref, wq_ref,
                    x1_ref, h2_ref, pq_ref):
    y = _dot(an_ref[...], woa_ref[...]) + _dot(cn_ref[...], woc_ref[...])
    x1 = x_ref[...] + g1_ref[0] * y
    x1_ref[...] = x1
    ms = jnp.mean(x1 * x1, axis=-1, keepdims=True)
    h2 = (x1 * lax.rsqrt(ms + NORM_EPS) * (1.0 + sc_ref[0]) + sh_ref[0]).astype(BF16)
    h2_ref[...] = h2
    pq_ref[...] = _dot(h2, wq_ref[...]).astype(BF16)


def _outproj(an, cn, x2, g1, sh2, sc2, woa, woc, wq, seq, tm):
    n, d = x2.shape
    per_b = seq // tm
    row = lambda i: (i, 0)
    const = lambda i: (0, 0)
    bat = lambda i: (i // per_b, 0, 0)
    return pl.pallas_call(
        _outproj_kernel,
        grid=(n // tm,),
        in_specs=[
            pl.BlockSpec((tm, an.shape[1]), row),
            pl.BlockSpec((tm, cn.shape[1]), row),
            pl.BlockSpec((tm, d), row),
            pl.BlockSpec((1, 1, d), bat),
            pl.BlockSpec((1, 1, d), bat),
            pl.BlockSpec((1, 1, d), bat),
            pl.BlockSpec(woa.shape, const),
            pl.BlockSpec(woc.shape, const),
            pl.BlockSpec(wq.shape, const),
        ],
        out_specs=[pl.BlockSpec((tm, d), row), pl.BlockSpec((tm, d), row),
                   pl.BlockSpec((tm, wq.shape[1]), row)],
        out_shape=[jax.ShapeDtypeStruct((n, d), F32), jax.ShapeDtypeStruct((n, d), BF16),
                   jax.ShapeDtypeStruct((n, wq.shape[1]), BF16)],
        compiler_params=_cparams(("arbitrary",)),
        name="outproj",
    )(an, cn, x2, g1, sh2, sc2, woa, woc, wq)


def _top_rows(s, rank=None, ids=None):
    if rank is None:
        rank = lax.broadcasted_iota(I32, s.shape, 0)
    vals, picks = [], []
    for _ in range(PEER_TOPK):
        m = jnp.max(s, axis=0, keepdims=True)
        pos = jnp.min(jnp.where(s == m, rank, 2 ** 30), axis=0, keepdims=True)
        hit = rank == pos
        vals.append(m)
        picks.append(pos if ids is None else jnp.max(jnp.where(hit, ids, -1), axis=0, keepdims=True))
        s = jnp.where(hit, -jnp.inf, s)
    return jnp.concatenate(vals, axis=0), jnp.concatenate(picks, axis=0)


def _pair_candidates(v1, i1, v2, i2):
    tr = v1.shape[1]
    k = PEER_TOPK
    r8 = lax.broadcasted_iota(I32, (SUBLANES, tr), 0)
    rk = lax.broadcasted_iota(I32, (k, tr), 0)
    sums = [v1 + v2[0:1]]
    eids = [i1 * PEER_KEYS + i2[0:1]]
    order = [rk * k]
    for b in range(1, SUBLANES):
        ok = r8 < k // (b + 1)
        sums.append(jnp.where(ok, v1[0:SUBLANES] + v2[b:b + 1], -jnp.inf))
        eids.append(i1[0:SUBLANES] * PEER_KEYS + i2[b:b + 1])
        order.append(r8 * k + b)
    sums.append(v1[0:1] + v2[SUBLANES:k])
    eids.append(i1[0:1] * PEER_KEYS + i2[SUBLANES:k])
    order.append(r8 + SUBLANES)
    return jnp.concatenate(sums, axis=0), jnp.concatenate(eids, axis=0), jnp.concatenate(order, axis=0)


def _route_kernel(pq_ref, sk_ref, a_ref, b_ref, g_ref, e_scr, g_scr):
    h = pl.program_id(1)
    half = PEER_KEYS
    s1 = _nt_dot(sk_ref[0, 0], pq_ref[:, :half])
    s2 = _nt_dot(sk_ref[0, 1], pq_ref[:, half:])
    v1, i1 = _top_rows(s1)
    v2, i2 = _top_rows(s2)
    cand, cid, order = _pair_candidates(v1, i1, v2, i2)
    ts, e = _top_rows(cand, order, cid)
    ex = jnp.exp(ts - ts[0:1])
    g = ex / jnp.sum(ex, axis=0, keepdims=True)
    rows = pl.ds(pl.multiple_of(h * PEER_TOPK, PEER_TOPK), PEER_TOPK)
    e_scr[rows, :] = e
    g_scr[rows, :] = g

    @pl.when(h == PEER_HEADS - 1)
    def _():
        e_all = e_scr[...]
        a_ref[...] = jnp.transpose(jnp.right_shift(e_all, 7).astype(F32))
        b_ref[...] = jnp.transpose(jnp.bitwise_and(e_all, PEER_KEYS - 1).astype(F32))
        g_ref[...] = jnp.transpose(g_scr[...])


def _route(pq, sk, tr):
    n = pq.shape[0]
    qd = 2 * PEER_KEYS
    out = jax.ShapeDtypeStruct((n, PEER_PAIRS), F32)
    oblk = pl.BlockSpec((tr, PEER_PAIRS), lambda i, h: (i, 0))
    return pl.pallas_call(
        _route_kernel,
        grid=(n // tr, PEER_HEADS),
        in_specs=[
            pl.BlockSpec((tr, qd), lambda i, h: (i, h)),
            pl.BlockSpec((1, 2, PEER_KEYS, PEER_KEYS), lambda i, h: (h, 0, 0, 0)),
        ],
        out_specs=[oblk, oblk, oblk],
        out_shape=[out, out, out],
        scratch_shapes=[pltpu.VMEM((PEER_PAIRS, tr), I32), pltpu.VMEM((PEER_PAIRS, tr), F32)],
        compiler_params=_cparams(("arbitrary", "arbitrary")),
        name="peer_route",
    )(pq, sk)


C_HALF = PEER_KEYS // 2
C_PITCH = C_HALF + SUBLANES
C_UNROLL = 16
HI16 = -65536


def _gelu(x):
    return 0.5 * x * (1.0 + lax.erf(x * (2.0 ** -0.5)))


def _bf16_bits(x):
    return pltpu.bitcast(x.astype(BF16).astype(F32), I32)


def _peer_kernel(h2_ref, a_ref, b_ref, g_ref, ulo_ref, uhi_ref, vlo_ref, vhi_ref, x1_ref, g2_ref, o_ref,
                 c_scr, acc_ref, *, tm, ec):
    j = pl.program_id(1)
    nj = pl.num_programs(1)
    k1 = ec // 2 // PEER_KEYS

    @pl.when(j == 0)
    def _():
        srow = lax.broadcasted_iota(I32, (PEER_KEYS, PEER_PAIRS), 0).astype(F32)

        def tokens(tb, carry):
            for u in range(C_UNROLL):
                t = tb * C_UNROLL + u
                a_row = a_ref[pl.ds(t, 1), :]
                b_row = b_ref[pl.ds(t, 1), :]
                g_row = g_ref[pl.ds(t, 1), :]
                at = jnp.where(srow == a_row, g_row, 0.0).astype(BF16)
                bt = jnp.where(srow == b_row, 1.0, 0.0).astype(BF16)
                ct = _nt_dot(at, bt)
                word = (lax.shift_right_logical(_bf16_bits(ct[:C_HALF]), 16)
                        | (_bf16_bits(ct[C_HALF:]) & HI16))
                c_scr[pl.ds(pl.multiple_of(t * C_PITCH, SUBLANES), C_HALF), :] = word
            return carry

        lax.fori_loop(0, tm // C_UNROLL, tokens, 0)
        acc_ref[...] = jnp.zeros(acc_ref.shape, F32)

    h2 = h2_ref[...]
    act_lo = _gelu(_nt_dot(h2, ulo_ref[...]))
    act_hi = _gelu(_nt_dot(h2, uhi_ref[...]))
    p_lo, p_hi = [], []
    for ii in range(k1):
        word = c_scr[pl.ds(j * k1 + ii, tm, stride=C_PITCH), :]
        c_lo = pltpu.bitcast(lax.shift_left(word, 16), F32)
        c_hi = pltpu.bitcast(word & HI16, F32)
        cols = slice(ii * PEER_KEYS, (ii + 1) * PEER_KEYS)
        p_lo.append((act_lo[:, cols] * c_lo).astype(BF16))
        p_hi.append((act_hi[:, cols] * c_hi).astype(BF16))
    acc_ref[...] += (_dot(jnp.concatenate(p_lo, axis=1), vlo_ref[...])
                     + _dot(jnp.concatenate(p_hi, axis=1), vhi_ref[...]))

    @pl.when(j == nj - 1)
    def _():
        o_ref[...] = x1_ref[...] + g2_ref[0] * acc_ref[...]


def _peer(h2, a, b, g, u, v, x1, g2, seq, tm, ec):
    n, d = x1.shape
    ne = u.shape[0]
    per_b = seq // tm
    eb = ec // 2
    nhalf = ne // 2 // eb
    row = lambda i, j: (i, 0)
    lo = lambda i, j: (j, 0)
    hi = lambda i, j: (j + nhalf, 0)
    return pl.pallas_call(
        functools.partial(_peer_kernel, tm=tm, ec=ec),
        grid=(n // tm, ne // ec),
        in_specs=[
            pl.BlockSpec((tm, d), row),
            pl.BlockSpec((tm, PEER_PAIRS), row),
            pl.BlockSpec((tm, PEER_PAIRS), row),
            pl.BlockSpec((tm, PEER_PAIRS), row),
            pl.BlockSpec((eb, d), lo),
            pl.BlockSpec((eb, d), hi),
            pl.BlockSpec((eb, d), lo),
            pl.BlockSpec((eb, d), hi),
            pl.BlockSpec((tm, d), row),
            pl.BlockSpec((1, 1, d), lambda i, j: (i // per_b, 0, 0)),
        ],
        out_specs=pl.BlockSpec((tm, d), row),
        out_shape=jax.ShapeDtypeStruct((n, d), F32),
        scratch_shapes=[pltpu.VMEM((tm * C_PITCH, PEER_KEYS), I32), pltpu.VMEM((tm, d), F32)],
        compiler_params=_cparams(("arbitrary", "arbitrary")),
        name="peer_experts",
    )(h2, a, b, g, u, u, v, v, x1, g2)


def _rope_lane_tables(positions):
    half = ROPE_DIM // 2
    inv_freq = jnp.power(jnp.float32(ROPE_THETA), -jnp.arange(0, ROPE_DIM, 2, dtype=F32) / ROPE_DIM)
    ang = positions.astype(F32).reshape(-1, 1) * inv_freq
    cos, sin = jnp.cos(ang), jnp.sin(ang)
    j = jnp.arange(LANES) % HEAD_DIM
    cos_l, sin_l = cos[:, j % half], sin[:, j % half]
    rc = jnp.where(j < ROPE_DIM, cos_l, 1.0)
    rs1 = jnp.where(j < half, -sin_l, 0.0)
    rs2 = jnp.where((j >= half) & (j < ROPE_DIM), sin_l, 0.0)
    return rc, rs1, rs2


def _tile_size(total, want):
    t = min(want, total)
    assert total % t == 0
    return t


def kernel(x, c, positions, w_ada, b_ada, w_in, q_norm_g, k_norm_g, idx_k_ln_g, idx_k_ln_b,
           conv_dw, conv_b, conv_ln_g, conv_ln_b, out_g_attn, out_g_conv, w_out, peer_wq,
           peer_sub_keys, peer_u, peer_v):
    bsz, seq, d = x.shape
    depth = w_ada.shape[0]
    n = bsz * seq
    w = ATTN_WIDTH
    cw = d - w
    assert bsz <= SUBLANES and cw % LANES == 0

    tm = _tile_size(seq, 512)
    tq = _tile_size(seq, 256)
    tt = _tile_size(seq, 512)
    tr = _tile_size(seq, 256)
    tp = _tile_size(seq, 512)
    ec = 1024

    c_pad = jnp.zeros((SUBLANES, d), F32).at[:bsz].set(c)
    mod = _ada(c_pad, w_ada, b_ada)[:, :bsz]
    rc, rs1, rs2 = _rope_lane_tables(positions)
    hd_id = jnp.arange(w) // HEAD_DIM
    gsum = (hd_id[:, None] == hd_id[None, :]).astype(BF16)

    o_q, o_k, o_v, o_qi, o_ki, o_wi = 0, w, 2 * w, 3 * w, 4 * w, 4 * w + IDX_DIM
    o_cu = o_wi + IDX_HEADS

    x2 = x.reshape(n, d)
    for l in range(depth):
        sh1, sc1, g1, sh2, sc2, g2 = [mod[l, :, i * d:(i + 1) * d].reshape(bsz, 1, d) for i in range(6)]
        wl = w_in[l]
        wa = wl[:, o_q:o_ki].astype(BF16)
        wki = wl[:, o_ki:o_wi]
        wk = jnp.concatenate([wki, wki], axis=1).astype(BF16)
        ww = jnp.pad(wl[:, o_wi:o_cu], ((0, 0), (0, LANES - IDX_HEADS))).astype(BF16)
        wc = wl[:, o_cu:].astype(BF16)
        tile_h = lambda g_: jnp.tile(g_, N_HEADS).reshape(1, w)
        dup = lambda g_: jnp.concatenate([g_, g_]).reshape(1, LANES)
        q, k, vt, qi, kk, wit, hg = _inproj(
            x2, sh1, sc1, wa, wk, ww, wc, tile_h(q_norm_g[l]), tile_h(k_norm_g[l]),
            dup(idx_k_ln_g[l]), dup(idx_k_ln_b[l]), rc, rs1, rs2, gsum, seq, tm)

        r3 = lambda t: t.reshape(bsz, seq, t.shape[-1])
        an = _attention(r3(q), r3(k), vt, r3(qi), r3(kk), wit, out_g_attn[l].reshape(1, w), tq)
        w_pad = jnp.pad(conv_dw[l], ((0, CONV_HALO - CONV_KERNEL), (0, 0)))
        cn = _conv(r3(hg), w_pad, conv_b[l].reshape(1, cw), conv_ln_g[l].reshape(1, cw),
                   conv_ln_b[l].reshape(1, cw), out_g_conv[l].reshape(1, cw), tt)

        wo = w_out[l].astype(BF16)
        x1, h2, pq = _outproj(an.reshape(n, w), cn.reshape(n, cw), x2, g1, sh2, sc2,
                              wo[:w], wo[w:], peer_wq[l].astype(BF16), seq, tm)
        a, b, g = _route(pq, peer_sub_keys[l].astype(BF16), tr)
        x2 = _peer(h2, a, b, g, peer_u[l].astype(BF16), peer_v[l].astype(BF16), x1, g2, seq, tp, ec)
    return x2.reshape(bsz, seq, d)
```

```python
import functools

import jax
import jax.numpy as jnp
from jax import lax
from jax.experimental import pallas as pl
from jax.experimental.pallas import tpu as pltpu

F32 = jnp.float32
BF16 = jnp.bfloat16
I32 = jnp.int32

N_HEADS = 8
HEAD_DIM = 64
ATTN_WIDTH = N_HEADS * HEAD_DIM
CONV_KERNEL = 31
IDX_HEADS = 8
IDX_DIM = 64
TOPK_KEYS = 256
ROPE_THETA = 500000.0
ROPE_DIM = HEAD_DIM // 4
PEER_HEADS = 8
PEER_KEYS = 128
PEER_TOPK = 16
PEER_PAIRS = PEER_HEADS * PEER_TOPK
NORM_EPS = 1e-6
LOG2_E = 1.4426950408889634
Q_SCALE = HEAD_DIM ** -0.5 * LOG2_E

LANES = 128
SUBLANES = 8
VMEM_LIMIT = 56 * 1024 * 1024

PACK16 = 2 * SUBLANES

NEG_BIG = -1e30
INT_MIN = -2 ** 31
INT16_MIN = -2 ** 15


def _cparams(sem):
    return pltpu.CompilerParams(dimension_semantics=sem, vmem_limit_bytes=VMEM_LIMIT)


def _nt_dot(a, b):
    return lax.dot_general(a, b, (((1,), (1,)), ((), ())), preferred_element_type=F32)


def _dot(a, b):
    return jnp.dot(a, b, preferred_element_type=F32)


def _ada_kernel(c_ref, w_ref, b_ref, o_ref):
    c = c_ref[...]
    ca = c * jax.nn.sigmoid(c)
    o_ref[0] = _dot(ca, w_ref[0]) + b_ref[0]


def _ada(c_pad, w_ada, b_ada):
    depth, d, n6 = w_ada.shape
    tn = n6 // 4
    return pl.pallas_call(
        _ada_kernel,
        grid=(depth, n6 // tn),
        in_specs=[
            pl.BlockSpec((SUBLANES, d), lambda l, j: (0, 0)),
            pl.BlockSpec((1, d, tn), lambda l, j: (l, 0, j)),
            pl.BlockSpec((1, 1, tn), lambda l, j: (l, 0, j)),
        ],
        out_specs=pl.BlockSpec((1, SUBLANES, tn), lambda l, j: (l, 0, j)),
        out_shape=jax.ShapeDtypeStruct((depth, SUBLANES, n6), F32),
        compiler_params=_cparams(("arbitrary", "arbitrary")),
        name="adaln",
    )(c_pad, w_ada, b_ada.reshape(depth, 1, n6))


def _rope(x, c, s1, s2):
    w = x.shape[-1]
    half = ROPE_DIM // 2
    return x * c + pltpu.roll(x, w - half, 1) * s1 + pltpu.roll(x, half, 1) * s2


def _split_hi_lo(x):
    hi = x.astype(BF16)
    lo = (x - hi.astype(F32)).astype(BF16)
    return hi, lo


def _inproj_kernel(x_ref, sh_ref, sc_ref, wa_ref, wk_ref, ww_ref, wc_ref, gq_ref, gk_ref,
                   lng_ref, lnb_ref, rc_ref, rs1_ref, rs2_ref, gsum_ref,
                   q_ref, k_ref, vt_ref, qi_ref, kk_ref, wit_ref, hg_ref):
    x = x_ref[...]
    ms = jnp.mean(x * x, axis=-1, keepdims=True)
    h = x * lax.rsqrt(ms + NORM_EPS) * (1.0 + sc_ref[0]) + sh_ref[0]
    hb = h.astype(BF16)

    rc1, rs11, rs21 = rc_ref[...], rs1_ref[...], rs2_ref[...]
    reps = ATTN_WIDTH // LANES
    rc = jnp.concatenate([rc1] * reps, axis=1)
    rs1 = jnp.concatenate([rs11] * reps, axis=1)
    rs2 = jnp.concatenate([rs21] * reps, axis=1)
    gsum = gsum_ref[...]

    def head_norm_rope(t, gain):
        hi, lo = _split_hi_lo(t * t)
        ssq = _dot(hi, gsum) + _dot(lo, gsum)
        tn = t * lax.rsqrt(ssq * (1.0 / HEAD_DIM) + NORM_EPS) * gain
        return _rope(tn, rc, rs1, rs2)

    lo_half = lax.broadcasted_iota(I32, (x.shape[0], LANES), 1) < HEAD_DIM

    def one_head_per_group(t):
        parts = []
        for p in range(reps):
            g = t[:, p * LANES:(p + 1) * LANES]
            parts += [jnp.where(lo_half, g, 0.0), jnp.where(lo_half, 0.0, g)]
        return jnp.concatenate(parts, axis=1).astype(BF16)

    w = ATTN_WIDTH
    pa = _dot(hb, wa_ref[...])
    q_ref[...] = one_head_per_group(head_norm_rope(pa[:, 0:w], gq_ref[...]) * Q_SCALE)
    k_ref[...] = head_norm_rope(pa[:, w:2 * w], gk_ref[...]).astype(BF16)
    vt_ref[...] = jnp.transpose(pa[:, 2 * w:3 * w]).astype(BF16)
    qi_ref[...] = one_head_per_group(_rope(pa[:, 3 * w:4 * w], rc, rs1, rs2) * (IDX_DIM ** -0.5))

    pk = _dot(hb, wk_ref[...])
    mu = jnp.mean(pk, axis=-1, keepdims=True)
    var = jnp.mean(jnp.square(pk - mu), axis=-1, keepdims=True)
    kn = (pk - mu) * lax.rsqrt(var + NORM_EPS) * lng_ref[...] + lnb_ref[...]
    kk_ref[...] = _rope(kn, rc1, rs11, rs21).astype(BF16)

    wit = jnp.transpose(_dot(hb, ww_ref[...]) * (IDX_HEADS ** -0.5))
    wit_ref[...] = wit[:IDX_HEADS]

    pc = _dot(hb, wc_ref[...])
    cw = pc.shape[1] // 2
    hg_ref[...] = pc[:, :cw] * jax.nn.sigmoid(pc[:, cw:])


def _inproj(x2, sh, sc, wa, wk, ww, wc, gq, gk, lng, lnb, rc, rs1, rs2, gsum, seq, tm):
    n, d = x2.shape
    per_b = seq // tm
    row = lambda i: (i, 0)
    const = lambda i: (0, 0)
    bat = lambda i: (i // per_b, 0, 0)
    cw = wc.shape[1] // 2
    outs = [
        jax.ShapeDtypeStruct((n, 2 * ATTN_WIDTH), BF16),
        jax.ShapeDtypeStruct((n, ATTN_WIDTH), BF16),
        jax.ShapeDtypeStruct((ATTN_WIDTH, n), BF16),
        jax.ShapeDtypeStruct((n, 2 * ATTN_WIDTH), BF16),
        jax.ShapeDtypeStruct((n, LANES), BF16),
        jax.ShapeDtypeStruct((IDX_HEADS, n), F32),
        jax.ShapeDtypeStruct((n, cw), F32),
    ]
    col = lambda i: (0, i)
    token_minor = (False, False, True, False, False, True, False)
    out_specs = [pl.BlockSpec((o.shape[0], tm), col) if tmin else pl.BlockSpec((tm, o.shape[1]), row)
                 for o, tmin in zip(outs, token_minor)]
    return pl.pallas_call(
        _inproj_kernel,
        grid=(n // tm,),
        in_specs=[
            pl.BlockSpec((tm, d), row),
            pl.BlockSpec((1, 1, d), bat),
            pl.BlockSpec((1, 1, d), bat),
            pl.BlockSpec(wa.shape, const),
            pl.BlockSpec(wk.shape, const),
            pl.BlockSpec(ww.shape, const),
            pl.BlockSpec(wc.shape, const),
            pl.BlockSpec(gq.shape, const),
            pl.BlockSpec(gk.shape, const),
            pl.BlockSpec(lng.shape, const),
            pl.BlockSpec(lnb.shape, const),
            pl.BlockSpec((tm, LANES), row),
            pl.BlockSpec((tm, LANES), row),
            pl.BlockSpec((tm, LANES), row),
            pl.BlockSpec(gsum.shape, const),
        ],
        out_specs=out_specs,
        out_shape=outs,
        compiler_params=_cparams(("arbitrary",)),
        name="inproj",
    )(x2, sh, sc, wa, wk, ww, wc, gq, gk, lng, lnb, rc, rs1, rs2, gsum)


CONV_HALO = 32
CONV_ROWS = 64


def _conv_kernel(cur_ref, halo_ref, w_ref, b_ref, lg_ref, lb_ref, og_ref, o_ref, ext_ref, *, tt):
    i = pl.program_id(1)
    ext_ref[0:CONV_HALO, :] = jnp.where(i > 0, halo_ref[0], 0.0)
    ext_ref[CONV_HALO:CONV_HALO + tt, :] = cur_ref[0]
    lead = CONV_HALO - (CONV_KERNEL - 1)
    for r0 in range(0, tt, CONV_ROWS):
        acc = jnp.zeros((CONV_ROWS, ext_ref.shape[1]), F32)
        for j in range(CONV_KERNEL):
            acc = acc + ext_ref[r0 + lead + j:r0 + lead + j + CONV_ROWS, :] * w_ref[j:j + 1, :]
        hcv = acc + b_ref[...]
        mu = jnp.mean(hcv, axis=-1, keepdims=True)
        var = jnp.mean(jnp.square(hcv - mu), axis=-1, keepdims=True)
        y = (hcv - mu) * lax.rsqrt(var + NORM_EPS) * lg_ref[...] + lb_ref[...]
        y = y * jax.nn.sigmoid(y)
        ms = jnp.mean(y * y, axis=-1, keepdims=True)
        o_ref[0, r0:r0 + CONV_ROWS, :] = (y * lax.rsqrt(ms + NORM_EPS) * og_ref[...]).astype(BF16)


def _conv(hg3, w_pad, b, lg, lb, og, tt):
    bsz, seq, cw = hg3.shape
    hb = tt // CONV_HALO
    const = lambda bb, i: (0, 0)
    return pl.pallas_call(
        functools.partial(_conv_kernel, tt=tt),
        grid=(bsz, seq // tt),
        in_specs=[
            pl.BlockSpec((1, tt, cw), lambda bb, i: (bb, i, 0)),
            pl.BlockSpec((1, CONV_HALO, cw), lambda bb, i: (bb, jnp.maximum(i * hb - 1, 0), 0)),
            pl.BlockSpec(w_pad.shape, const),
            pl.BlockSpec(b.shape, const),
            pl.BlockSpec(lg.shape, const),
            pl.BlockSpec(lb.shape, const),
            pl.BlockSpec(og.shape, const),
        ],
        out_specs=pl.BlockSpec((1, tt, cw), lambda bb, i: (bb, i, 0)),
        out_shape=jax.ShapeDtypeStruct((bsz, seq, cw), BF16),
        scratch_shapes=[pltpu.VMEM((CONV_HALO + tt, cw), F32)],
        compiler_params=_cparams(("arbitrary", "arbitrary")),
        name="conv",
    )(hg3, hg3, w_pad, b, lg, lb, og)


def _attn_kernel(q_ref, qi_ref, wit_ref, k_ref, vt_ref, kk_ref, og_ref, o_ref,
                 key_ref, k16_ref, bias_ref, m_ref, a_ref, acc_ref, lm_ref, p_ref, *, tq, n_sel, seq):
    i = pl.program_id(1)
    nk = i + 1
    krow = lax.broadcasted_iota(I32, (tq, tq), 0)
    qcol = lax.broadcasted_iota(I32, (tq, tq), 1) + i * tq

    def chunk_rows(c):
        return pl.ds(pl.multiple_of(c * tq, tq), tq)

    wit = wit_ref[...]

    def score_chunk(c, carry):
        kic = kk_ref[0, chunk_rows(c), :]
        s = jnp.zeros((tq, tq), F32)
        for h in range(IDX_HEADS):
            qh = qi_ref[0, :, h * LANES:(h + 1) * LANES]
            s = s + wit[h:h + 1, :] * jnp.maximum(_nt_dot(kic, qh), 0.0)
        s = jnp.where(krow + c * tq <= qcol, s, -jnp.inf)
        bits = pltpu.bitcast(s, I32)
        key = jnp.where(bits < 0, bits ^ jnp.int32(0x7FFFFFFF), bits)
        key_ref[c] = jnp.where(s == 0.0, 0, key)
        return carry

    lax.fori_loop(0, nk, score_chunk, 0)

    def count(pred_fn):
        def body(c, acc):
            hit = jnp.where(pred_fn(c, key_ref[c]), 1, 0)
            return acc + jnp.sum(hit.reshape(tq // SUBLANES, SUBLANES, tq), axis=0)
        acc = lax.fori_loop(0, nk, body, jnp.zeros((SUBLANES, tq), I32))
        return jnp.sum(acc, axis=0, keepdims=True)

    def count_ge(thr):
        return count(lambda c, key: key >= thr)

    grp = tq // PACK16
    one_h, zero_h = jnp.ones((), BF16), jnp.zeros((), BF16)

    def count16(thr):
        thr16 = jnp.broadcast_to(thr, (PACK16, tq)).astype(jnp.int16)

        def body(cp, acc):
            for c in (2 * cp, 2 * cp + 1):
                hit = jnp.where(k16_ref[c].reshape(grp, PACK16, tq) >= thr16[None], one_h, zero_h)
                parts = [hit[r] for r in range(grp)]
                while len(parts) > 1:
                    parts = [parts[r] + parts[r + 1] for r in range(0, len(parts), 2)]
                acc = acc + parts[0].astype(F32)
            return acc

        acc = lax.fori_loop(0, (nk + 1) // 2, body, jnp.zeros((PACK16, tq), F32))
        return jnp.sum(acc, axis=0, keepdims=True).astype(I32)

    def search16():
        def step(bi, cand):
            trial = cand + jnp.left_shift(jnp.int32(1), 15 - bi)
            return jnp.where(count16(trial) >= n_sel, trial, cand)
        return lax.fori_loop(0, 16, step, jnp.full((1, tq), INT16_MIN, I32))

    def store_upper(c, carry):
        k16_ref[c] = lax.shift_right_arithmetic(key_ref[c], 16).astype(jnp.int16)
        return carry

    lax.fori_loop(0, nk, store_upper, 0)

    @pl.when(nk % 2 == 1)
    def _():
        k16_ref[nk] = jnp.full((tq, tq), INT16_MIN, jnp.int16)

    tau_hi = search16()

    def store_lower(c, carry):
        key = key_ref[c]
        hi = lax.shift_right_arithmetic(key, 16)
        lo = (key & 0xFFFF) + INT16_MIN
        k16_ref[c] = jnp.where(hi == tau_hi, lo, jnp.where(hi > tau_hi, -INT16_MIN - 1, INT16_MIN)
                               ).astype(jnp.int16)
        return carry

    lax.fori_loop(0, nk, store_lower, 0)
    tau = jnp.left_shift(tau_hi, 16) | (search16() - INT16_MIN)
    n_ge = count_ge(tau)
    need = n_sel - count_ge(tau + 1)

    def tie_cut():
        def step(bi, jm):
            trial = jm + jnp.left_shift(jnp.int32(1), (seq.bit_length() - 2) - bi)
            f = count(lambda c, key: jnp.where(key == tau, krow + c * tq, seq) < trial)
            return jnp.where(f < need, trial, jm)
        return lax.fori_loop(0, seq.bit_length() - 1, step, jnp.zeros((1, tq), I32))

    has_tie = jnp.max(n_ge) > n_sel
    jm = lax.cond(has_tie, tie_cut, lambda: jnp.full((1, tq), seq, I32))

    def bias_chunk(c, carry):
        key = key_ref[c]
        kidx = krow + c * tq
        tied = jnp.where(kidx <= jm, 0.0, NEG_BIG)
        above = jnp.where(key > tau, 0.0, NEG_BIG)
        bias_ref[c] = jnp.where(kidx <= qcol, jnp.where(key == tau, tied, above), NEG_BIG)
        return carry

    lax.fori_loop(0, nk, bias_chunk, 0)

    m_ref[...] = jnp.full(m_ref.shape, NEG_BIG, F32)
    acc_ref[...] = jnp.zeros(acc_ref.shape, F32)
    ones_rows = jnp.ones((PACK16, tq), BF16)

    def attn_chunk(c, carry):
        rows = chunk_rows(c)
        for hd in range(N_HEADS):
            p = hd // 2
            kp = k_ref[0, rows, p * LANES:(p + 1) * LANES]
            qh = q_ref[0, :, hd * LANES:(hd + 1) * LANES]
            lm_ref[hd] = _nt_dot(kp, qh) + bias_ref[c]
        for hd in range(N_HEADS):
            m_old = m_ref[hd]
            m_new = jnp.maximum(m_old, jnp.max(lm_ref[hd], axis=0, keepdims=True))
            m_ref[hd] = m_new
            a_ref[hd] = jnp.exp2(m_old - m_new)
            p_ref[hd] = jnp.exp2(lm_ref[hd] - m_new).astype(BF16)
        for hd in range(N_HEADS):
            vth = jnp.concatenate([vt_ref[hd * HEAD_DIM:(hd + 1) * HEAD_DIM, rows], ones_rows], axis=0)
            acc_ref[hd] = acc_ref[hd] * a_ref[hd] + _dot(vth, p_ref[hd])
        return carry

    lax.fori_loop(0, nk, attn_chunk, 0)

    ot = jnp.concatenate([acc_ref[hd, :HEAD_DIM] / acc_ref[hd, HEAD_DIM:HEAD_DIM + 1]
                          for hd in range(N_HEADS)], axis=0)
    ms = jnp.mean(ot * ot, axis=0, keepdims=True)
    ot = ot * lax.rsqrt(ms + NORM_EPS)
    o_ref[0] = (jnp.transpose(ot) * og_ref[...]).astype(BF16)


def _attention(q3, k3, vt, qi3, kk3, wit, og, tq):
    bsz, seq, w = k3.shape
    n_sel = min(TOPK_KEYS, seq // 4)
    nc = seq // tq
    qblk = lambda bb, i: (bb, i, 0)
    full = lambda bb, i: (bb, 0, 0)
    return pl.pallas_call(
        functools.partial(_attn_kernel, tq=tq, n_sel=n_sel, seq=seq),
        grid=(bsz, nc),
        in_specs=[
            pl.BlockSpec((1, tq, 2 * w), qblk),
            pl.BlockSpec((1, tq, 2 * w), qblk),
            pl.BlockSpec((SUBLANES, tq), lambda bb, i: (0, bb * nc + i)),
            pl.BlockSpec((1, seq, w), full),
            pl.BlockSpec((w, seq), lambda bb, i: (0, bb)),
            pl.BlockSpec((1, seq, LANES), full),
            pl.BlockSpec(og.shape, lambda bb, i: (0, 0)),
        ],
        out_specs=pl.BlockSpec((1, tq, w), qblk),
        out_shape=jax.ShapeDtypeStruct((bsz, seq, w), BF16),
        scratch_shapes=[
            pltpu.VMEM((nc, tq, tq), I32),
            pltpu.VMEM((nc + nc % 2, tq, tq), jnp.int16),
            pltpu.VMEM((nc, tq, tq), F32),
            pltpu.VMEM((N_HEADS, 1, tq), F32),
            pltpu.VMEM((N_HEADS, 1, tq), F32),
            pltpu.VMEM((N_HEADS, HEAD_DIM + PACK16, tq), F32),
            pltpu.VMEM((N_HEADS, tq, tq), F32),
            pltpu.VMEM((N_HEADS, tq, tq), BF16),
        ],
        compiler_params=_cparams(("arbitrary", "arbitrary")),
        name="dsa_attention",
    )(q3, qi3, wit, k3, vt, kk3, og)


def _outproj_kernel(an_ref, cn_ref, x_ref, g1_ref, sh_ref, sc_ref, woa_ref, woc_ref, wq_ref,
                    x1_ref, h2_ref, pq_ref):
    y = _dot(an_ref[...], woa_ref[...]) + _dot(cn_ref[...], woc_ref[...])
    x1 = x_ref[...] + g1_ref[0] * y
    x1_ref[...] = x1
    ms = jnp.mean(x1 * x1, axis=-1, keepdims=True)
    h2 = (x1 * lax.rsqrt(ms + NORM_EPS) * (1.0 + sc_ref[0]) + sh_ref[0]).astype(BF16)
    h2_ref[...] = h2
    pq_ref[...] = _dot(h2, wq_ref[...]).astype(BF16)


def _outproj(an, cn, x2, g1, sh2, sc2, woa, woc, wq, seq, tm):
    n, d = x2.shape
    per_b = seq // tm
    row = lambda i: (i, 0)
    const = lambda i: (0, 0)
    bat = lambda i: (i // per_b, 0, 0)
    return pl.pallas_call(
        _outproj_kernel,
        grid=(n // tm,),
        in_specs=[
            pl.BlockSpec((tm, an.shape[1]), row),
            pl.BlockSpec((tm, cn.shape[1]), row),
            pl.BlockSpec((tm, d), row),
            pl.BlockSpec((1, 1, d), bat),
            pl.BlockSpec((1, 1, d), bat),
            pl.BlockSpec((1, 1, d), bat),
            pl.BlockSpec(woa.shape, const),
            pl.BlockSpec(woc.shape, const),
            pl.BlockSpec(wq.shape, const),
        ],
        out_specs=[pl.BlockSpec((tm, d), row), pl.BlockSpec((tm, d), row),
                   pl.BlockSpec((tm, wq.shape[1]), row)],
        out_shape=[jax.ShapeDtypeStruct((n, d), F32), jax.ShapeDtypeStruct((n, d), BF16),
                   jax.ShapeDtypeStruct((n, wq.shape[1]), BF16)],
        compiler_params=_cparams(("arbitrary",)),
        name="outproj",
    )(an, cn, x2, g1, sh2, sc2, woa, woc, wq)


def _top_rows(s, rank=None, ids=None):
    if rank is None:
        rank = lax.broadcasted_iota(I32, s.shape, 0)
    vals, picks = [], []
    for _ in range(PEER_TOPK):
        m = jnp.max(s, axis=0, keepdims=True)
        pos = jnp.min(jnp.where(s == m, rank, 2 ** 30), axis=0, keepdims=True)
        hit = rank == pos
        vals.append(m)
        picks.append(pos if ids is None else jnp.max(jnp.where(hit, ids, -1), axis=0, keepdims=True))
        s = jnp.where(hit, -jnp.inf, s)
    return jnp.concatenate(vals, axis=0), jnp.concatenate(picks, axis=0)


def _pair_candidates(v1, i1, v2, i2):
    tr = v1.shape[1]
    k = PEER_TOPK
    r8 = lax.broadcasted_iota(I32, (SUBLANES, tr), 0)
    rk = lax.broadcasted_iota(I32, (k, tr), 0)
    sums = [v1 + v2[0:1]]
    eids = [i1 * PEER_KEYS + i2[0:1]]
    order = [rk * k]
    for b in range(1, SUBLANES):
        ok = r8 < k // (b + 1)
        sums.append(jnp.where(ok, v1[0:SUBLANES] + v2[b:b + 1], -jnp.inf))
        eids.append(i1[0:SUBLANES] * PEER_KEYS + i2[b:b + 1])
        order.append(r8 * k + b)
    sums.append(v1[0:1] + v2[SUBLANES:k])
    eids.append(i1[0:1] * PEER_KEYS + i2[SUBLANES:k])
    order.append(r8 + SUBLANES)
    return jnp.concatenate(sums, axis=0), jnp.concatenate(eids, axis=0), jnp.concatenate(order, axis=0)


def _route_kernel(pq_ref, sk_ref, a_ref, b_ref, g_ref, e_scr, g_scr):
    h = pl.program_id(1)
    half = PEER_KEYS
    s1 = _nt_dot(sk_ref[0, 0], pq_ref[:, :half])
    s2 = _nt_dot(sk_ref[0, 1], pq_ref[:, half:])
    v1, i1 = _top_rows(s1)
    v2, i2 = _top_rows(s2)
    cand, cid, order = _pair_candidates(v1, i1, v2, i2)
    ts, e = _top_rows(cand, order, cid)
    ex = jnp.exp(ts - ts[0:1])
    g = ex / jnp.sum(ex, axis=0, keepdims=True)
    rows = pl.ds(pl.multiple_of(h * PEER_TOPK, PEER_TOPK), PEER_TOPK)
    e_scr[rows, :] = e
    g_scr[rows, :] = g

    @pl.when(h == PEER_HEADS - 1)
    def _():
        e_all = e_scr[...]
        a_ref[...] = jnp.transpose(jnp.right_shift(e_all, 7).astype(F32))
        b_ref[...] = jnp.transpose(jnp.bitwise_and(e_all, PEER_KEYS - 1).astype(F32))
        g_ref[...] = jnp.transpose(g_scr[...])


def _route(pq, sk, tr):
    n = pq.shape[0]
    qd = 2 * PEER_KEYS
    out = jax.ShapeDtypeStruct((n, PEER_PAIRS), F32)
    oblk = pl.BlockSpec((tr, PEER_PAIRS), lambda i, h: (i, 0))
    return pl.pallas_call(
        _route_kernel,
        grid=(n // tr, PEER_HEADS),
        in_specs=[
            pl.BlockSpec((tr, qd), lambda i, h: (i, h)),
            pl.BlockSpec((1, 2, PEER_KEYS, PEER_KEYS), lambda i, h: (h, 0, 0, 0)),
        ],
        out_specs=[oblk, oblk, oblk],
        out_shape=[out, out, out],
        scratch_shapes=[pltpu.VMEM((PEER_PAIRS, tr), I32), pltpu.VMEM((PEER_PAIRS, tr), F32)],
        compiler_params=_cparams(("arbitrary", "arbitrary")),
        name="peer_route",
    )(pq, sk)


C_HALF = PEER_KEYS // 2
C_PITCH = C_HALF + SUBLANES
C_UNROLL = 16
HI16 = -65536


def _gelu(x):
    return 0.5 * x * (1.0 + lax.erf(x * (2.0 ** -0.5)))


def _bf16_bits(x):
    return pltpu.bitcast(x.astype(BF16).astype(F32), I32)


def _peer_kernel(h2_ref, a_ref, b_ref, g_ref, ulo_ref, uhi_ref, vlo_ref, vhi_ref, x1_ref, g2_ref, o_ref,
                 c_scr, acc_ref, *, tm, ec):
    j = pl.program_id(1)
    nj = pl.num_programs(1)
    k1 = ec // 2 // PEER_KEYS

    @pl.when(j == 0)
    def _():
        srow = lax.broadcasted_iota(I32, (PEER_KEYS, PEER_PAIRS), 0).astype(F32)

        def tokens(tb, carry):
            for u in range(C_UNROLL):
                t = tb * C_UNROLL + u
                a_row = a_ref[pl.ds(t, 1), :]
                b_row = b_ref[pl.ds(t, 1), :]
                g_row = g_ref[pl.ds(t, 1), :]
                at = jnp.where(srow == a_row, g_row, 0.0).astype(BF16)
                bt = jnp.where(srow == b_row, 1.0, 0.0).astype(BF16)
                ct = _nt_dot(at, bt)
                word = (lax.shift_right_logical(_bf16_bits(ct[:C_HALF]), 16)
                        | (_bf16_bits(ct[C_HALF:]) & HI16))
                c_scr[pl.ds(pl.multiple_of(t * C_PITCH, SUBLANES), C_HALF), :] = word
            return carry

        lax.fori_loop(0, tm // C_UNROLL, tokens, 0)
        acc_ref[...] = jnp.zeros(acc_ref.shape, F32)

    h2 = h2_ref[...]
    act_lo = _gelu(_dot(h2, ulo_ref[...]))
    act_hi = _gelu(_dot(h2, uhi_ref[...]))
    p_lo, p_hi = [], []
    for ii in range(k1):
        word = c_scr[pl.ds(j * k1 + ii, tm, stride=C_PITCH), :]
        c_lo = pltpu.bitcast(lax.shift_left(word, 16), F32)
        c_hi = pltpu.bitcast(word & HI16, F32)
        cols = slice(ii * PEER_KEYS, (ii + 1) * PEER_KEYS)
        p_lo.append((act_lo[:, cols] * c_lo).astype(BF16))
        p_hi.append((act_hi[:, cols] * c_hi).astype(BF16))
    acc_ref[...] += (_dot(jnp.concatenate(p_lo, axis=1), vlo_ref[...])
                     + _dot(jnp.concatenate(p_hi, axis=1), vhi_ref[...]))

    @pl.when(j == nj - 1)
    def _():
        o_ref[...] = x1_ref[...] + g2_ref[0] * acc_ref[...]


def _peer(h2, a, b, g, ut, v, x1, g2, seq, tm, ec):
    n, d = x1.shape
    ne = v.shape[0]
    per_b = seq // tm
    eb = ec // 2
    nhalf = ne // 2 // eb
    row = lambda i, j: (i, 0)
    lo = lambda i, j: (j, 0)
    hi = lambda i, j: (j + nhalf, 0)
    return pl.pallas_call(
        functools.partial(_peer_kernel, tm=tm, ec=ec),
        grid=(n // tm, ne // ec),
        in_specs=[
            pl.BlockSpec((tm, d), row),
            pl.BlockSpec((tm, PEER_PAIRS), row),
            pl.BlockSpec((tm, PEER_PAIRS), row),
            pl.BlockSpec((tm, PEER_PAIRS), row),
            pl.BlockSpec((d, eb), lambda i, j: (0, j)),
            pl.BlockSpec((d, eb), lambda i, j: (0, j + nhalf)),
            pl.BlockSpec((eb, d), lo),
            pl.BlockSpec((eb, d), hi),
            pl.BlockSpec((tm, d), row),
            pl.BlockSpec((1, 1, d), lambda i, j: (i // per_b, 0, 0)),
        ],
        out_specs=pl.BlockSpec((tm, d), row),
        out_shape=jax.ShapeDtypeStruct((n, d), F32),
        scratch_shapes=[pltpu.VMEM((tm * C_PITCH, PEER_KEYS), I32), pltpu.VMEM((tm, d), F32)],
        compiler_params=_cparams(("arbitrary", "arbitrary")),
        name="peer_experts",
    )(h2, a, b, g, ut, ut, v, v, x1, g2)


def _rope_lane_tables(positions):
    half = ROPE_DIM // 2
    inv_freq = jnp.power(jnp.float32(ROPE_THETA), -jnp.arange(0, ROPE_DIM, 2, dtype=F32) / ROPE_DIM)
    ang = positions.astype(F32).reshape(-1, 1) * inv_freq
    cos, sin = jnp.cos(ang), jnp.sin(ang)
    j = jnp.arange(LANES) % HEAD_DIM
    cos_l, sin_l = cos[:, j % half], sin[:, j % half]
    rc = jnp.where(j < ROPE_DIM, cos_l, 1.0)
    rs1 = jnp.where(j < half, -sin_l, 0.0)
    rs2 = jnp.where((j >= half) & (j < ROPE_DIM), sin_l, 0.0)
    return rc, rs1, rs2


def _tile_size(total, want):
    t = min(want, total)
    assert total % t == 0
    return t


def kernel(x, c, positions, w_ada, b_ada, w_in, q_norm_g, k_norm_g, idx_k_ln_g, idx_k_ln_b,
           conv_dw, conv_b, conv_ln_g, conv_ln_b, out_g_attn, out_g_conv, w_out, peer_wq,
           peer_sub_keys, peer_u, peer_v):
    bsz, seq, d = x.shape
    depth = w_ada.shape[0]
    n = bsz * seq
    w = ATTN_WIDTH
    cw = d - w
    assert bsz <= SUBLANES and cw % LANES == 0

    tm = _tile_size(seq, 512)
    tq = _tile_size(seq, 256)
    tt = _tile_size(seq, 512)
    tr = _tile_size(seq, 256)
    tp = _tile_size(seq, 512)
    ec = 1024

    c_pad = jnp.zeros((SUBLANES, d), F32).at[:bsz].set(c)
    mod = _ada(c_pad, w_ada, b_ada)[:, :bsz]
    rc, rs1, rs2 = _rope_lane_tables(positions)
    hd_id = jnp.arange(w) // HEAD_DIM
    gsum = (hd_id[:, None] == hd_id[None, :]).astype(BF16)

    o_q, o_k, o_v, o_qi, o_ki, o_wi = 0, w, 2 * w, 3 * w, 4 * w, 4 * w + IDX_DIM
    o_cu = o_wi + IDX_HEADS

    x2 = x.reshape(n, d)
    for l in range(depth):
        sh1, sc1, g1, sh2, sc2, g2 = [mod[l, :, i * d:(i + 1) * d].reshape(bsz, 1, d) for i in range(6)]
        wl = w_in[l]
        wa = wl[:, o_q:o_ki].astype(BF16)
        wki = wl[:, o_ki:o_wi]
        wk = jnp.concatenate([wki, wki], axis=1).astype(BF16)
        ww = jnp.pad(wl[:, o_wi:o_cu], ((0, 0), (0, LANES - IDX_HEADS))).astype(BF16)
        wc = wl[:, o_cu:].astype(BF16)
        tile_h = lambda g_: jnp.tile(g_, N_HEADS).reshape(1, w)
        dup = lambda g_: jnp.concatenate([g_, g_]).reshape(1, LANES)
        q, k, vt, qi, kk, wit, hg = _inproj(
            x2, sh1, sc1, wa, wk, ww, wc, tile_h(q_norm_g[l]), tile_h(k_norm_g[l]),
            dup(idx_k_ln_g[l]), dup(idx_k_ln_b[l]), rc, rs1, rs2, gsum, seq, tm)

        r3 = lambda t: t.reshape(bsz, seq, t.shape[-1])
        an = _attention(r3(q), r3(k), vt, r3(qi), r3(kk), wit, out_g_attn[l].reshape(1, w), tq)
        w_pad = jnp.pad(conv_dw[l], ((0, CONV_HALO - CONV_KERNEL), (0, 0)))
        cn = _conv(r3(hg), w_pad, conv_b[l].reshape(1, cw), conv_ln_g[l].reshape(1, cw),
                   conv_ln_b[l].reshape(1, cw), out_g_conv[l].reshape(1, cw), tt)

        wo = w_out[l].astype(BF16)
        x1, h2, pq = _outproj(an.reshape(n, w), cn.reshape(n, cw), x2, g1, sh2, sc2,
                              wo[:w], wo[w:], peer_wq[l].astype(BF16), seq, tm)
        a, b, g = _route(pq, peer_sub_keys[l].astype(BF16), tr)
        x2 = _peer(h2, a, b, g, jnp.transpose(peer_u[l]).astype(BF16), peer_v[l].astype(BF16), x1, g2,
                   seq, tp, ec)
    return x2.reshape(bsz, seq, d)
```

```python
import functools

import jax
import jax.numpy as jnp
from jax import lax
from jax.experimental import pallas as pl
from jax.experimental.pallas import tpu as pltpu

F32 = jnp.float32
BF16 = jnp.bfloat16
I32 = jnp.int32

N_HEADS = 8
HEAD_DIM = 64
ATTN_WIDTH = N_HEADS * HEAD_DIM
CONV_KERNEL = 31
IDX_HEADS = 8
IDX_DIM = 64
TOPK_KEYS = 256
ROPE_THETA = 500000.0
ROPE_DIM = HEAD_DIM // 4
PEER_HEADS = 8
PEER_KEYS = 128
PEER_TOPK = 16
PEER_PAIRS = PEER_HEADS * PEER_TOPK
NORM_EPS = 1e-6
LOG2_E = 1.4426950408889634
Q_SCALE = HEAD_DIM ** -0.5 * LOG2_E

LANES = 128
SUBLANES = 8
VMEM_LIMIT = 56 * 1024 * 1024

PACK16 = 2 * SUBLANES

NEG_BIG = -1e30
INT_MIN = -2 ** 31
INT16_MIN = -2 ** 15


def _cparams(sem):
    return pltpu.CompilerParams(dimension_semantics=sem, vmem_limit_bytes=VMEM_LIMIT)


def _nt_dot(a, b):
    return lax.dot_general(a, b, (((1,), (1,)), ((), ())), preferred_element_type=F32)


def _dot(a, b):
    return jnp.dot(a, b, preferred_element_type=F32)


def _ada_kernel(c_ref, w_ref, b_ref, o_ref):
    c = c_ref[...]
    ca = c * jax.nn.sigmoid(c)
    o_ref[0] = _dot(ca, w_ref[0]) + b_ref[0]


def _ada(c_pad, w_ada, b_ada):
    depth, d, n6 = w_ada.shape
    tn = n6 // 4
    return pl.pallas_call(
        _ada_kernel,
        grid=(depth, n6 // tn),
        in_specs=[
            pl.BlockSpec((SUBLANES, d), lambda l, j: (0, 0)),
            pl.BlockSpec((1, d, tn), lambda l, j: (l, 0, j)),
            pl.BlockSpec((1, 1, tn), lambda l, j: (l, 0, j)),
        ],
        out_specs=pl.BlockSpec((1, SUBLANES, tn), lambda l, j: (l, 0, j)),
        out_shape=jax.ShapeDtypeStruct((depth, SUBLANES, n6), F32),
        compiler_params=_cparams(("arbitrary", "arbitrary")),
        name="adaln",
    )(c_pad, w_ada, b_ada.reshape(depth, 1, n6))


def _rope(x, c, s1, s2):
    w = x.shape[-1]
    half = ROPE_DIM // 2
    return x * c + pltpu.roll(x, w - half, 1) * s1 + pltpu.roll(x, half, 1) * s2


def _split_hi_lo(x):
    hi = x.astype(BF16)
    lo = (x - hi.astype(F32)).astype(BF16)
    return hi, lo


def _inproj_kernel(x_ref, sh_ref, sc_ref, wa_ref, wk_ref, ww_ref, wc_ref, gq_ref, gk_ref,
                   lng_ref, lnb_ref, rc_ref, rs1_ref, rs2_ref, gsum_ref,
                   q_ref, k_ref, vt_ref, qi_ref, kk_ref, wit_ref, hg_ref):
    x = x_ref[...]
    ms = jnp.mean(x * x, axis=-1, keepdims=True)
    h = x * lax.rsqrt(ms + NORM_EPS) * (1.0 + sc_ref[0]) + sh_ref[0]
    hb = h.astype(BF16)

    rc1, rs11, rs21 = rc_ref[...], rs1_ref[...], rs2_ref[...]
    reps = ATTN_WIDTH // LANES
    rc = jnp.concatenate([rc1] * reps, axis=1)
    rs1 = jnp.concatenate([rs11] * reps, axis=1)
    rs2 = jnp.concatenate([rs21] * reps, axis=1)
    gsum = gsum_ref[...]

    def head_norm_rope(t, gain):
        hi, lo = _split_hi_lo(t * t)
        ssq = _dot(hi, gsum) + _dot(lo, gsum)
        tn = t * lax.rsqrt(ssq * (1.0 / HEAD_DIM) + NORM_EPS) * gain
        return _rope(tn, rc, rs1, rs2)

    lo_half = lax.broadcasted_iota(I32, (x.shape[0], LANES), 1) < HEAD_DIM

    def one_head_per_group(t):
        parts = []
        for p in range(reps):
            g = t[:, p * LANES:(p + 1) * LANES]
            parts += [jnp.where(lo_half, g, 0.0), jnp.where(lo_half, 0.0, g)]
        return jnp.concatenate(parts, axis=1).astype(BF16)

    w = ATTN_WIDTH
    pa = _dot(hb, wa_ref[...])
    q_ref[...] = one_head_per_group(head_norm_rope(pa[:, 0:w], gq_ref[...]) * Q_SCALE)
    k_ref[...] = head_norm_rope(pa[:, w:2 * w], gk_ref[...]).astype(BF16)
    vt_ref[...] = jnp.transpose(pa[:, 2 * w:3 * w]).astype(BF16)
    qi_ref[...] = one_head_per_group(_rope(pa[:, 3 * w:4 * w], rc, rs1, rs2) * (IDX_DIM ** -0.5))

    pk = _dot(hb, wk_ref[...])
    mu = jnp.mean(pk, axis=-1, keepdims=True)
    var = jnp.mean(jnp.square(pk - mu), axis=-1, keepdims=True)
    kn = (pk - mu) * lax.rsqrt(var + NORM_EPS) * lng_ref[...] + lnb_ref[...]
    kk_ref[...] = _rope(kn, rc1, rs11, rs21).astype(BF16)

    wit = jnp.transpose(_dot(hb, ww_ref[...]) * (IDX_HEADS ** -0.5))
    wit_ref[...] = wit[:IDX_HEADS]

    pc = _dot(hb, wc_ref[...])
    cw = pc.shape[1] // 2
    hg_ref[...] = pc[:, :cw] * jax.nn.sigmoid(pc[:, cw:])


def _inproj(x2, sh, sc, wa, wk, ww, wc, gq, gk, lng, lnb, rc, rs1, rs2, gsum, seq, tm):
    n, d = x2.shape
    per_b = seq // tm
    row = lambda i: (i, 0)
    const = lambda i: (0, 0)
    bat = lambda i: (i // per_b, 0, 0)
    cw = wc.shape[1] // 2
    outs = [
        jax.ShapeDtypeStruct((n, 2 * ATTN_WIDTH), BF16),
        jax.ShapeDtypeStruct((n, ATTN_WIDTH), BF16),
        jax.ShapeDtypeStruct((ATTN_WIDTH, n), BF16),
        jax.ShapeDtypeStruct((n, 2 * ATTN_WIDTH), BF16),
        jax.ShapeDtypeStruct((n, LANES), BF16),
        jax.ShapeDtypeStruct((IDX_HEADS, n), F32),
        jax.ShapeDtypeStruct((n, cw), F32),
    ]
    col = lambda i: (0, i)
    token_minor = (False, False, True, False, False, True, False)
    out_specs = [pl.BlockSpec((o.shape[0], tm), col) if tmin else pl.BlockSpec((tm, o.shape[1]), row)
                 for o, tmin in zip(outs, token_minor)]
    return pl.pallas_call(
        _inproj_kernel,
        grid=(n // tm,),
        in_specs=[
            pl.BlockSpec((tm, d), row),
            pl.BlockSpec((1, 1, d), bat),
            pl.BlockSpec((1, 1, d), bat),
            pl.BlockSpec(wa.shape, const),
            pl.BlockSpec(wk.shape, const),
            pl.BlockSpec(ww.shape, const),
            pl.BlockSpec(wc.shape, const),
            pl.BlockSpec(gq.shape, const),
            pl.BlockSpec(gk.shape, const),
            pl.BlockSpec(lng.shape, const),
            pl.BlockSpec(lnb.shape, const),
            pl.BlockSpec((tm, LANES), row),
            pl.BlockSpec((tm, LANES), row),
            pl.BlockSpec((tm, LANES), row),
            pl.BlockSpec(gsum.shape, const),
        ],
        out_specs=out_specs,
        out_shape=outs,
        compiler_params=_cparams(("arbitrary",)),
        name="inproj",
    )(x2, sh, sc, wa, wk, ww, wc, gq, gk, lng, lnb, rc, rs1, rs2, gsum)


CONV_HALO = 32
CONV_ROWS = 64


def _conv_kernel(cur_ref, halo_ref, w_ref, b_ref, lg_ref, lb_ref, og_ref, o_ref, ext_ref, *, tt):
    i = pl.program_id(1)
    ext_ref[0:CONV_HALO, :] = jnp.where(i > 0, halo_ref[0], 0.0)
    ext_ref[CONV_HALO:CONV_HALO + tt, :] = cur_ref[0]
    lead = CONV_HALO - (CONV_KERNEL - 1)
    for r0 in range(0, tt, CONV_ROWS):
        acc = jnp.zeros((CONV_ROWS, ext_ref.shape[1]), F32)
        for j in range(CONV_KERNEL):
            acc = acc + ext_ref[r0 + lead + j:r0 + lead + j + CONV_ROWS, :] * w_ref[j:j + 1, :]
        hcv = acc + b_ref[...]
        mu = jnp.mean(hcv, axis=-1, keepdims=True)
        var = jnp.mean(jnp.square(hcv - mu), axis=-1, keepdims=True)
        y = (hcv - mu) * lax.rsqrt(var + NORM_EPS) * lg_ref[...] + lb_ref[...]
        y = y * jax.nn.sigmoid(y)
        ms = jnp.mean(y * y, axis=-1, keepdims=True)
        o_ref[0, r0:r0 + CONV_ROWS, :] = (y * lax.rsqrt(ms + NORM_EPS) * og_ref[...]).astype(BF16)


def _conv(hg3, w_pad, b, lg, lb, og, tt):
    bsz, seq, cw = hg3.shape
    hb = tt // CONV_HALO
    const = lambda bb, i: (0, 0)
    return pl.pallas_call(
        functools.partial(_conv_kernel, tt=tt),
        grid=(bsz, seq // tt),
        in_specs=[
            pl.BlockSpec((1, tt, cw), lambda bb, i: (bb, i, 0)),
            pl.BlockSpec((1, CONV_HALO, cw), lambda bb, i: (bb, jnp.maximum(i * hb - 1, 0), 0)),
            pl.BlockSpec(w_pad.shape, const),
            pl.BlockSpec(b.shape, const),
            pl.BlockSpec(lg.shape, const),
            pl.BlockSpec(lb.shape, const),
            pl.BlockSpec(og.shape, const),
        ],
        out_specs=pl.BlockSpec((1, tt, cw), lambda bb, i: (bb, i, 0)),
        out_shape=jax.ShapeDtypeStruct((bsz, seq, cw), BF16),
        scratch_shapes=[pltpu.VMEM((CONV_HALO + tt, cw), F32)],
        compiler_params=_cparams(("arbitrary", "arbitrary")),
        name="conv",
    )(hg3, hg3, w_pad, b, lg, lb, og)


def _attn_kernel(q_ref, qi_ref, wit_ref, k_ref, vt_ref, kk_ref, og_ref, o_ref,
                 key_ref, k16_ref, bias_ref, m_ref, a_ref, acc_ref, lm_ref, p_ref, *, tq, n_sel, seq):
    i = pl.program_id(1)
    nk = i + 1
    krow = lax.broadcasted_iota(I32, (tq, tq), 0)
    qcol = lax.broadcasted_iota(I32, (tq, tq), 1) + i * tq

    def chunk_rows(c):
        return pl.ds(pl.multiple_of(c * tq, tq), tq)

    wit = wit_ref[...]

    def score_pair(cp, carry):
        for c in (2 * cp, 2 * cp + 1):
            kic = kk_ref[0, chunk_rows(c), :]
            s = jnp.zeros((tq, tq), F32)
            for h in range(IDX_HEADS):
                qh = qi_ref[0, :, h * LANES:(h + 1) * LANES]
                s = s + wit[h:h + 1, :] * jnp.maximum(_nt_dot(kic, qh), 0.0)
            s = jnp.where(krow + c * tq <= qcol, s, -jnp.inf)
            bits = pltpu.bitcast(s, I32)
            key = jnp.where(bits < 0, bits ^ jnp.int32(0x7FFFFFFF), bits)
            key_ref[c] = jnp.where(s == 0.0, 0, key)
        return carry

    lax.fori_loop(0, (nk + 1) // 2, score_pair, 0)

    def count(pred_fn):
        def body(c, acc):
            hit = jnp.where(pred_fn(c, key_ref[c]), 1, 0)
            return acc + jnp.sum(hit.reshape(tq // SUBLANES, SUBLANES, tq), axis=0)
        acc = lax.fori_loop(0, nk, body, jnp.zeros((SUBLANES, tq), I32))
        return jnp.sum(acc, axis=0, keepdims=True)

    def count_ge(thr):
        return count(lambda c, key: key >= thr)

    grp = tq // PACK16
    one_h, zero_h = jnp.ones((), BF16), jnp.zeros((), BF16)

    def count16(thr):
        thr16 = jnp.broadcast_to(thr, (PACK16, tq)).astype(jnp.int16)

        def body(cp, acc):
            for c in (2 * cp, 2 * cp + 1):
                hit = jnp.where(k16_ref[c].reshape(grp, PACK16, tq) >= thr16[None], one_h, zero_h)
                parts = [hit[r] for r in range(grp)]
                while len(parts) > 1:
                    parts = [parts[r] + parts[r + 1] for r in range(0, len(parts), 2)]
                acc = acc + parts[0].astype(F32)
            return acc

        acc = lax.fori_loop(0, (nk + 1) // 2, body, jnp.zeros((PACK16, tq), F32))
        return jnp.sum(acc, axis=0, keepdims=True).astype(I32)

    def search16():
        def step(bi, cand):
            trial = cand + jnp.left_shift(jnp.int32(1), 15 - bi)
            return jnp.where(count16(trial) >= n_sel, trial, cand)
        return lax.fori_loop(0, 16, step, jnp.full((1, tq), INT16_MIN, I32))

    def store_upper(c, carry):
        k16_ref[c] = lax.shift_right_arithmetic(key_ref[c], 16).astype(jnp.int16)
        return carry

    lax.fori_loop(0, nk, store_upper, 0)

    @pl.when(nk % 2 == 1)
    def _():
        k16_ref[nk] = jnp.full((tq, tq), INT16_MIN, jnp.int16)

    tau_hi = search16()

    def store_lower(c, carry):
        key = key_ref[c]
        hi = lax.shift_right_arithmetic(key, 16)
        lo = (key & 0xFFFF) + INT16_MIN
        k16_ref[c] = jnp.where(hi == tau_hi, lo, jnp.where(hi > tau_hi, -INT16_MIN - 1, INT16_MIN)
                               ).astype(jnp.int16)
        return carry

    lax.fori_loop(0, nk, store_lower, 0)
    tau = jnp.left_shift(tau_hi, 16) | (search16() - INT16_MIN)
    n_ge = count_ge(tau)
    need = n_sel - count_ge(tau + 1)

    def tie_cut():
        def step(bi, jm):
            trial = jm + jnp.left_shift(jnp.int32(1), (seq.bit_length() - 2) - bi)
            f = count(lambda c, key: jnp.where(key == tau, krow + c * tq, seq) < trial)
            return jnp.where(f < need, trial, jm)
        return lax.fori_loop(0, seq.bit_length() - 1, step, jnp.zeros((1, tq), I32))

    has_tie = jnp.max(n_ge) > n_sel
    jm = lax.cond(has_tie, tie_cut, lambda: jnp.full((1, tq), seq, I32))

    def bias_chunk(c, carry):
        key = key_ref[c]
        kidx = krow + c * tq
        tied = jnp.where(kidx <= jm, 0.0, NEG_BIG)
        above = jnp.where(key > tau, 0.0, NEG_BIG)
        bias_ref[c] = jnp.where(kidx <= qcol, jnp.where(key == tau, tied, above), NEG_BIG)
        return carry

    lax.fori_loop(0, nk, bias_chunk, 0)

    m_ref[...] = jnp.full(m_ref.shape, NEG_BIG, F32)
    acc_ref[...] = jnp.zeros(acc_ref.shape, F32)
    ones_rows = jnp.ones((PACK16, tq), BF16)

    def logits(c, slot):
        rows = chunk_rows(c)
        for hd in range(N_HEADS):
            p = hd // 2
            kp = k_ref[0, rows, p * LANES:(p + 1) * LANES]
            qh = q_ref[0, :, hd * LANES:(hd + 1) * LANES]
            lm_ref[slot, hd] = _nt_dot(kp, qh) + bias_ref[c]

    def softmax_pv(c, slot):
        for hd in range(N_HEADS):
            m_old = m_ref[hd]
            m_new = jnp.maximum(m_old, jnp.max(lm_ref[slot, hd], axis=0, keepdims=True))
            m_ref[hd] = m_new
            a_ref[hd] = jnp.exp2(m_old - m_new)
            p_ref[hd] = jnp.exp2(lm_ref[slot, hd] - m_new).astype(BF16)
        rows = chunk_rows(c)
        for hd in range(N_HEADS):
            vth = jnp.concatenate([vt_ref[hd * HEAD_DIM:(hd + 1) * HEAD_DIM, rows], ones_rows], axis=0)
            acc_ref[hd] = acc_ref[hd] * a_ref[hd] + _dot(vth, p_ref[hd])

    def attn_pair(cp, carry):
        c0 = 2 * cp
        logits(c0 + 1, 1)
        softmax_pv(c0, 0)
        logits(jnp.minimum(c0 + 2, last_c), 0)
        softmax_pv(c0 + 1, 1)
        return carry

    @pl.when(nk % 2 == 1)
    def _():
        bias_ref[nk] = jnp.full((tq, tq), NEG_BIG, F32)

    last_c = nk - 1 + nk % 2
    logits(0, 0)
    lax.fori_loop(0, (nk + 1) // 2, attn_pair, 0)

    ot = jnp.concatenate([acc_ref[hd, :HEAD_DIM] / acc_ref[hd, HEAD_DIM:HEAD_DIM + 1]
                          for hd in range(N_HEADS)], axis=0)
    ms = jnp.mean(ot * ot, axis=0, keepdims=True)
    ot = ot * lax.rsqrt(ms + NORM_EPS)
    o_ref[0] = (jnp.transpose(ot) * og_ref[...]).astype(BF16)


def _attention(q3, k3, vt, qi3, kk3, wit, og, tq):
    bsz, seq, w = k3.shape
    n_sel = min(TOPK_KEYS, seq // 4)
    nc = seq // tq
    assert nc % 2 == 0, "the kernel walks key chunks in pairs"
    qblk = lambda bb, i: (bb, i, 0)
    full = lambda bb, i: (bb, 0, 0)
    return pl.pallas_call(
        functools.partial(_attn_kernel, tq=tq, n_sel=n_sel, seq=seq),
        grid=(bsz, nc),
        in_specs=[
            pl.BlockSpec((1, tq, 2 * w), qblk),
            pl.BlockSpec((1, tq, 2 * w), qblk),
            pl.BlockSpec((SUBLANES, tq), lambda bb, i: (0, bb * nc + i)),
            pl.BlockSpec((1, seq, w), full),
            pl.BlockSpec((w, seq), lambda bb, i: (0, bb)),
            pl.BlockSpec((1, seq, LANES), full),
            pl.BlockSpec(og.shape, lambda bb, i: (0, 0)),
        ],
        out_specs=pl.BlockSpec((1, tq, w), qblk),
        out_shape=jax.ShapeDtypeStruct((bsz, seq, w), BF16),
        scratch_shapes=[
            pltpu.VMEM((nc, tq, tq), I32),
            pltpu.VMEM((nc, tq, tq), jnp.int16),
            pltpu.VMEM((nc, tq, tq), F32),
            pltpu.VMEM((N_HEADS, 1, tq), F32),
            pltpu.VMEM((N_HEADS, 1, tq), F32),
            pltpu.VMEM((N_HEADS, HEAD_DIM + PACK16, tq), F32),
            pltpu.VMEM((2, N_HEADS, tq, tq), F32),
            pltpu.VMEM((N_HEADS, tq, tq), BF16),
        ],
        compiler_params=_cparams(("arbitrary", "arbitrary")),
        name="dsa_attention",
    )(q3, qi3, wit, k3, vt, kk3, og)


def _outproj_kernel(an_ref, cn_ref, x_ref, g1_ref, sh_ref, sc_ref, woa_ref, woc_ref, wq_ref,
                    x1_ref, h2_ref, pq_ref):
    y = _dot(an_ref[...], woa_ref[...]) + _dot(cn_ref[...], woc_ref[...])
    x1 = x_ref[...] + g1_ref[0] * y
    x1_ref[...] = x1
    ms = jnp.mean(x1 * x1, axis=-1, keepdims=True)
    h2 = (x1 * lax.rsqrt(ms + NORM_EPS) * (1.0 + sc_ref[0]) + sh_ref[0]).astype(BF16)
    h2_ref[...] = h2
    pq_ref[...] = _dot(h2, wq_ref[...]).astype(BF16)


def _outproj(an, cn, x2, g1, sh2, sc2, woa, woc, wq, seq, tm):
    n, d = x2.shape
    per_b = seq // tm
    row = lambda i: (i, 0)
    const = lambda i: (0, 0)
    bat = lambda i: (i // per_b, 0, 0)
    return pl.pallas_call(
        _outproj_kernel,
        grid=(n // tm,),
        in_specs=[
            pl.BlockSpec((tm, an.shape[1]), row),
            pl.BlockSpec((tm, cn.shape[1]), row),
            pl.BlockSpec((tm, d), row),
            pl.BlockSpec((1, 1, d), bat),
            pl.BlockSpec((1, 1, d), bat),
            pl.BlockSpec((1, 1, d), bat),
            pl.BlockSpec(woa.shape, const),
            pl.BlockSpec(woc.shape, const),
            pl.BlockSpec(wq.shape, const),
        ],
        out_specs=[pl.BlockSpec((tm, d), row), pl.BlockSpec((tm, d), row),
                   pl.BlockSpec((tm, wq.shape[1]), row)],
        out_shape=[jax.ShapeDtypeStruct((n, d), F32), jax.ShapeDtypeStruct((n, d), BF16),
                   jax.ShapeDtypeStruct((n, wq.shape[1]), BF16)],
        compiler_params=_cparams(("arbitrary",)),
        name="outproj",
    )(an, cn, x2, g1, sh2, sc2, woa, woc, wq)


def _top_rows(s, rank=None, ids=None):
    return _top_rows_multi([(s, rank, ids)])[0]


def _top_rows_multi(problems):
    state = []
    for s, rank, ids in problems:
        if rank is None:
            rank = lax.broadcasted_iota(I32, s.shape, 0)
        state.append([s, rank, ids, [], []])
    for _ in range(PEER_TOPK):
        for st in state:
            s, rank, ids, vals, picks = st
            m = jnp.max(s, axis=0, keepdims=True)
            pos = jnp.min(jnp.where(s == m, rank, 2 ** 30), axis=0, keepdims=True)
            hit = rank == pos
            vals.append(m)
            picks.append(pos if ids is None else jnp.max(jnp.where(hit, ids, -1), axis=0, keepdims=True))
            st[0] = jnp.where(hit, -jnp.inf, s)
    return [(jnp.concatenate(st[3], axis=0), jnp.concatenate(st[4], axis=0)) for st in state]


def _pair_candidates(v1, i1, v2, i2):
    tr = v1.shape[1]
    k = PEER_TOPK
    r8 = lax.broadcasted_iota(I32, (SUBLANES, tr), 0)
    rk = lax.broadcasted_iota(I32, (k, tr), 0)
    sums = [v1 + v2[0:1]]
    eids = [i1 * PEER_KEYS + i2[0:1]]
    order = [rk * k]
    for b in range(1, SUBLANES):
        ok = r8 < k // (b + 1)
        sums.append(jnp.where(ok, v1[0:SUBLANES] + v2[b:b + 1], -jnp.inf))
        eids.append(i1[0:SUBLANES] * PEER_KEYS + i2[b:b + 1])
        order.append(r8 * k + b)
    sums.append(v1[0:1] + v2[SUBLANES:k])
    eids.append(i1[0:1] * PEER_KEYS + i2[SUBLANES:k])
    order.append(r8 + SUBLANES)
    return jnp.concatenate(sums, axis=0), jnp.concatenate(eids, axis=0), jnp.concatenate(order, axis=0)


def _route_kernel(pq_ref, sk_ref, a_ref, b_ref, g_ref, e_scr, g_scr):
    h = pl.program_id(1)
    half = PEER_KEYS
    s1 = _nt_dot(sk_ref[0, 0], pq_ref[:, :half])
    s2 = _nt_dot(sk_ref[0, 1], pq_ref[:, half:])
    (v1, i1), (v2, i2) = _top_rows_multi([(s1, None, None), (s2, None, None)])
    cand, cid, order = _pair_candidates(v1, i1, v2, i2)
    ts, e = _top_rows(cand, order, cid)
    ex = jnp.exp(ts - ts[0:1])
    g = ex / jnp.sum(ex, axis=0, keepdims=True)
    rows = pl.ds(pl.multiple_of(h * PEER_TOPK, PEER_TOPK), PEER_TOPK)
    e_scr[rows, :] = e
    g_scr[rows, :] = g

    @pl.when(h == PEER_HEADS - 1)
    def _():
        e_all = e_scr[...]
        a_ref[...] = jnp.transpose(jnp.right_shift(e_all, 7).astype(F32))
        b_ref[...] = jnp.transpose(jnp.bitwise_and(e_all, PEER_KEYS - 1).astype(F32))
        g_ref[...] = jnp.transpose(g_scr[...])


def _route(pq, sk, tr):
    n = pq.shape[0]
    qd = 2 * PEER_KEYS
    out = jax.ShapeDtypeStruct((n, PEER_PAIRS), F32)
    oblk = pl.BlockSpec((tr, PEER_PAIRS), lambda i, h: (i, 0))
    return pl.pallas_call(
        _route_kernel,
        grid=(n // tr, PEER_HEADS),
        in_specs=[
            pl.BlockSpec((tr, qd), lambda i, h: (i, h)),
            pl.BlockSpec((1, 2, PEER_KEYS, PEER_KEYS), lambda i, h: (h, 0, 0, 0)),
        ],
        out_specs=[oblk, oblk, oblk],
        out_shape=[out, out, out],
        scratch_shapes=[pltpu.VMEM((PEER_PAIRS, tr), I32), pltpu.VMEM((PEER_PAIRS, tr), F32)],
        compiler_params=_cparams(("arbitrary", "arbitrary")),
        name="peer_route",
    )(pq, sk)


C_HALF = PEER_KEYS // 2
C_PITCH = C_HALF + SUBLANES
C_UNROLL = 16
HI16 = -65536


def _gelu(x):
    return 0.5 * x * (1.0 + lax.erf(x * (2.0 ** -0.5)))


def _bf16_bits(x):
    return pltpu.bitcast(x.astype(BF16).astype(F32), I32)


def _peer_kernel(h2_ref, a_ref, b_ref, g_ref, ulo_ref, uhi_ref, vlo_ref, vhi_ref, x1_ref, g2_ref, o_ref,
                 c_scr, acc_ref, *, tm, ec):
    j = pl.program_id(1)
    nj = pl.num_programs(1)
    k1 = ec // 2 // PEER_KEYS

    @pl.when(j == 0)
    def _():
        srow = lax.broadcasted_iota(I32, (PEER_KEYS, PEER_PAIRS), 0).astype(F32)

        def tokens(tb, carry):
            for u in range(C_UNROLL):
                t = tb * C_UNROLL + u
                a_row = a_ref[pl.ds(t, 1), :]
                b_row = b_ref[pl.ds(t, 1), :]
                g_row = g_ref[pl.ds(t, 1), :]
                at = jnp.where(srow == a_row, g_row, 0.0).astype(BF16)
                bt = jnp.where(srow == b_row, 1.0, 0.0).astype(BF16)
                ct = _nt_dot(at, bt)
                word = lax.shift_right_logical(_bf16_bits(ct[:C_HALF]), 16) | _bf16_bits(ct[C_HALF:])
                c_scr[pl.ds(pl.multiple_of(t * C_PITCH, SUBLANES), C_HALF), :] = word
            return carry

        lax.fori_loop(0, tm // C_UNROLL, tokens, 0)
        acc_ref[...] = jnp.zeros(acc_ref.shape, F32)

    h2 = h2_ref[...]
    act_lo = _gelu(_dot(h2, ulo_ref[...]))
    act_hi = _gelu(_dot(h2, uhi_ref[...]))
    p_lo, p_hi = [], []
    for ii in range(k1):
        word = c_scr[pl.ds(j * k1 + ii, tm, stride=C_PITCH), :]
        c_lo = pltpu.bitcast(lax.shift_left(word, 16), F32)
        c_hi = pltpu.bitcast(word & HI16, F32)
        cols = slice(ii * PEER_KEYS, (ii + 1) * PEER_KEYS)
        p_lo.append((act_lo[:, cols] * c_lo).astype(BF16))
        p_hi.append((act_hi[:, cols] * c_hi).astype(BF16))
    acc_ref[...] += (_dot(jnp.concatenate(p_lo, axis=1), vlo_ref[...])
                     + _dot(jnp.concatenate(p_hi, axis=1), vhi_ref[...]))

    @pl.when(j == nj - 1)
    def _():
        o_ref[...] = x1_ref[...] + g2_ref[0] * acc_ref[...]


def _peer(h2, a, b, g, ut, v, x1, g2, seq, tm, ec):
    n, d = x1.shape
    ne = v.shape[0]
    per_b = seq // tm
    eb = ec // 2
    nhalf = ne // 2 // eb
    row = lambda i, j: (i, 0)
    lo = lambda i, j: (j, 0)
    hi = lambda i, j: (j + nhalf, 0)
    return pl.pallas_call(
        functools.partial(_peer_kernel, tm=tm, ec=ec),
        grid=(n // tm, ne // ec),
        in_specs=[
            pl.BlockSpec((tm, d), row),
            pl.BlockSpec((tm, PEER_PAIRS), row),
            pl.BlockSpec((tm, PEER_PAIRS), row),
            pl.BlockSpec((tm, PEER_PAIRS), row),
            pl.BlockSpec((d, eb), lambda i, j: (0, j)),
            pl.BlockSpec((d, eb), lambda i, j: (0, j + nhalf)),
            pl.BlockSpec((eb, d), lo),
            pl.BlockSpec((eb, d), hi),
            pl.BlockSpec((tm, d), row),
            pl.BlockSpec((1, 1, d), lambda i, j: (i // per_b, 0, 0)),
        ],
        out_specs=pl.BlockSpec((tm, d), row),
        out_shape=jax.ShapeDtypeStruct((n, d), F32),
        scratch_shapes=[pltpu.VMEM((tm * C_PITCH, PEER_KEYS), I32), pltpu.VMEM((tm, d), F32)],
        compiler_params=_cparams(("arbitrary", "arbitrary")),
        name="peer_experts",
    )(h2, a, b, g, ut, ut, v, v, x1, g2)


def _rope_lane_tables(positions):
    half = ROPE_DIM // 2
    inv_freq = jnp.power(jnp.float32(ROPE_THETA), -jnp.arange(0, ROPE_DIM, 2, dtype=F32) / ROPE_DIM)
    ang = positions.astype(F32).reshape(-1, 1) * inv_freq
    cos, sin = jnp.cos(ang), jnp.sin(ang)
    j = jnp.arange(LANES) % HEAD_DIM
    cos_l, sin_l = cos[:, j % half], sin[:, j % half]
    rc = jnp.where(j < ROPE_DIM, cos_l, 1.0)
    rs1 = jnp.where(j < half, -sin_l, 0.0)
    rs2 = jnp.where((j >= half) & (j < ROPE_DIM), sin_l, 0.0)
    return rc, rs1, rs2


def _tile_size(total, want):
    t = min(want, total)
    assert total % t == 0
    return t


def kernel(x, c, positions, w_ada, b_ada, w_in, q_norm_g, k_norm_g, idx_k_ln_g, idx_k_ln_b,
           conv_dw, conv_b, conv_ln_g, conv_ln_b, out_g_attn, out_g_conv, w_out, peer_wq,
           peer_sub_keys, peer_u, peer_v):
    bsz, seq, d = x.shape
    depth = w_ada.shape[0]
    n = bsz * seq
    w = ATTN_WIDTH
    cw = d - w
    assert bsz <= SUBLANES and cw % LANES == 0

    tm = _tile_size(seq, 512)
    tq = _tile_size(seq, 256)
    tt = _tile_size(seq, 512)
    tr = _tile_size(seq, 256)
    tp = _tile_size(seq, 512)
    ec = 1024

    c_pad = jnp.zeros((SUBLANES, d), F32).at[:bsz].set(c)
    mod = _ada(c_pad, w_ada, b_ada)[:, :bsz]
    rc, rs1, rs2 = _rope_lane_tables(positions)
    hd_id = jnp.arange(w) // HEAD_DIM
    gsum = (hd_id[:, None] == hd_id[None, :]).astype(BF16)

    o_q, o_k, o_v, o_qi, o_ki, o_wi = 0, w, 2 * w, 3 * w, 4 * w, 4 * w + IDX_DIM
    o_cu = o_wi + IDX_HEADS

    x2 = x.reshape(n, d)
    for l in range(depth):
        sh1, sc1, g1, sh2, sc2, g2 = [mod[l, :, i * d:(i + 1) * d].reshape(bsz, 1, d) for i in range(6)]
        wl = w_in[l]
        wa = wl[:, o_q:o_ki].astype(BF16)
        wki = wl[:, o_ki:o_wi]
        wk = jnp.concatenate([wki, wki], axis=1).astype(BF16)
        ww = jnp.pad(wl[:, o_wi:o_cu], ((0, 0), (0, LANES - IDX_HEADS))).astype(BF16)
        wc = wl[:, o_cu:].astype(BF16)
        tile_h = lambda g_: jnp.tile(g_, N_HEADS).reshape(1, w)
        dup = lambda g_: jnp.concatenate([g_, g_]).reshape(1, LANES)
        q, k, vt, qi, kk, wit, hg = _inproj(
            x2, sh1, sc1, wa, wk, ww, wc, tile_h(q_norm_g[l]), tile_h(k_norm_g[l]),
            dup(idx_k_ln_g[l]), dup(idx_k_ln_b[l]), rc, rs1, rs2, gsum, seq, tm)

        r3 = lambda t: t.reshape(bsz, seq, t.shape[-1])
        an = _attention(r3(q), r3(k), vt, r3(qi), r3(kk), wit, out_g_attn[l].reshape(1, w), tq)
        w_pad = jnp.pad(conv_dw[l], ((0, CONV_HALO - CONV_KERNEL), (0, 0)))
        cn = _conv(r3(hg), w_pad, conv_b[l].reshape(1, cw), conv_ln_g[l].reshape(1, cw),
                   conv_ln_b[l].reshape(1, cw), out_g_conv[l].reshape(1, cw), tt)

        wo = w_out[l].astype(BF16)
        x1, h2, pq = _outproj(an.reshape(n, w), cn.reshape(n, cw), x2, g1, sh2, sc2,
                              wo[:w], wo[w:], peer_wq[l].astype(BF16), seq, tm)
        a, b, g = _route(pq, peer_sub_keys[l].astype(BF16), tr)
        x2 = _peer(h2, a, b, g, jnp.transpose(peer_u[l]).astype(BF16), peer_v[l].astype(BF16), x1, g2,
                   seq, tp, ec)
    return x2.reshape(bsz, seq, d)
```

```python
import functools

import jax
import jax.numpy as jnp
from jax import lax
from jax.experimental import pallas as pl
from jax.experimental.pallas import tpu as pltpu

F32 = jnp.float32
BF16 = jnp.bfloat16
I32 = jnp.int32

N_HEADS = 8
HEAD_DIM = 64
ATTN_WIDTH = N_HEADS * HEAD_DIM
CONV_KERNEL = 31
IDX_HEADS = 8
IDX_DIM = 64
TOPK_KEYS = 256
ROPE_THETA = 500000.0
ROPE_DIM = HEAD_DIM // 4
PEER_HEADS = 8
PEER_KEYS = 128
PEER_TOPK = 16
PEER_PAIRS = PEER_HEADS * PEER_TOPK
NORM_EPS = 1e-6
LOG2_E = 1.4426950408889634
Q_SCALE = HEAD_DIM ** -0.5 * LOG2_E

LANES = 128
SUBLANES = 8
VMEM_LIMIT = 56 * 1024 * 1024

PACK16 = 2 * SUBLANES

NEG_BIG = -1e30
INT_MIN = -2 ** 31
INT16_MIN = -2 ** 15


def _cparams(sem):
    return pltpu.CompilerParams(dimension_semantics=sem, vmem_limit_bytes=VMEM_LIMIT)


def _nt_dot(a, b):
    return lax.dot_general(a, b, (((1,), (1,)), ((), ())), preferred_element_type=F32)


def _dot(a, b):
    return jnp.dot(a, b, preferred_element_type=F32)


def _ada_kernel(c_ref, w_ref, b_ref, o_ref):
    c = c_ref[...]
    ca = c * jax.nn.sigmoid(c)
    o_ref[0] = _dot(ca, w_ref[0]) + b_ref[0]


def _ada(c_pad, w_ada, b_ada):
    depth, d, n6 = w_ada.shape
    tn = n6 // 4
    return pl.pallas_call(
        _ada_kernel,
        grid=(depth, n6 // tn),
        in_specs=[
            pl.BlockSpec((SUBLANES, d), lambda l, j: (0, 0)),
            pl.BlockSpec((1, d, tn), lambda l, j: (l, 0, j)),
            pl.BlockSpec((1, 1, tn), lambda l, j: (l, 0, j)),
        ],
        out_specs=pl.BlockSpec((1, SUBLANES, tn), lambda l, j: (l, 0, j)),
        out_shape=jax.ShapeDtypeStruct((depth, SUBLANES, n6), F32),
        compiler_params=_cparams(("arbitrary", "arbitrary")),
        name="adaln",
    )(c_pad, w_ada, b_ada.reshape(depth, 1, n6))


def _rope(x, c, s1, s2):
    w = x.shape[-1]
    half = ROPE_DIM // 2
    return x * c + pltpu.roll(x, w - half, 1) * s1 + pltpu.roll(x, half, 1) * s2


def _split_hi_lo(x):
    hi = x.astype(BF16)
    lo = (x - hi.astype(F32)).astype(BF16)
    return hi, lo


def _inproj_kernel(x_ref, sh_ref, sc_ref, wa_ref, wk_ref, ww_ref, wc_ref, gq_ref, gk_ref,
                   lng_ref, lnb_ref, rc_ref, rs1_ref, rs2_ref, gsum_ref,
                   q_ref, k_ref, vt_ref, qi_ref, kk_ref, wit_ref, hg_ref):
    x = x_ref[...]
    ms = jnp.mean(x * x, axis=-1, keepdims=True)
    h = x * lax.rsqrt(ms + NORM_EPS) * (1.0 + sc_ref[0]) + sh_ref[0]
    hb = h.astype(BF16)

    rc1, rs11, rs21 = rc_ref[...], rs1_ref[...], rs2_ref[...]
    reps = ATTN_WIDTH // LANES
    rc = jnp.concatenate([rc1] * reps, axis=1)
    rs1 = jnp.concatenate([rs11] * reps, axis=1)
    rs2 = jnp.concatenate([rs21] * reps, axis=1)
    gsum = gsum_ref[...]

    def head_norm_rope(t, gain):
        hi, lo = _split_hi_lo(t * t)
        ssq = _dot(hi, gsum) + _dot(lo, gsum)
        tn = t * lax.rsqrt(ssq * (1.0 / HEAD_DIM) + NORM_EPS) * gain
        return _rope(tn, rc, rs1, rs2)

    lo_half = lax.broadcasted_iota(I32, (x.shape[0], LANES), 1) < HEAD_DIM

    def one_head_per_group(t):
        parts = []
        for p in range(reps):
            g = t[:, p * LANES:(p + 1) * LANES]
            parts += [jnp.where(lo_half, g, 0.0), jnp.where(lo_half, 0.0, g)]
        return jnp.concatenate(parts, axis=1).astype(BF16)

    w = ATTN_WIDTH
    pa = _dot(hb, wa_ref[...])
    q_ref[...] = one_head_per_group(head_norm_rope(pa[:, 0:w], gq_ref[...]) * Q_SCALE)
    k_ref[...] = head_norm_rope(pa[:, w:2 * w], gk_ref[...]).astype(BF16)
    vt_ref[...] = jnp.transpose(pa[:, 2 * w:3 * w]).astype(BF16)
    qi_ref[...] = one_head_per_group(_rope(pa[:, 3 * w:4 * w], rc, rs1, rs2) * (IDX_DIM ** -0.5))

    pk = _dot(hb, wk_ref[...])
    mu = jnp.mean(pk, axis=-1, keepdims=True)
    var = jnp.mean(jnp.square(pk - mu), axis=-1, keepdims=True)
    kn = (pk - mu) * lax.rsqrt(var + NORM_EPS) * lng_ref[...] + lnb_ref[...]
    kk_ref[...] = _rope(kn, rc1, rs11, rs21).astype(BF16)

    wit = jnp.transpose(_dot(hb, ww_ref[...]) * (IDX_HEADS ** -0.5))
    wit_ref[...] = wit[:IDX_HEADS]

    pc = _dot(hb, wc_ref[...])
    cw = pc.shape[1] // 2
    hg_ref[...] = pc[:, :cw] * jax.nn.sigmoid(pc[:, cw:])


def _inproj(x2, sh, sc, wa, wk, ww, wc, gq, gk, lng, lnb, rc, rs1, rs2, gsum, seq, tm):
    n, d = x2.shape
    per_b = seq // tm
    row = lambda i: (i, 0)
    const = lambda i: (0, 0)
    bat = lambda i: (i // per_b, 0, 0)
    cw = wc.shape[1] // 2
    outs = [
        jax.ShapeDtypeStruct((n, 2 * ATTN_WIDTH), BF16),
        jax.ShapeDtypeStruct((n, ATTN_WIDTH), BF16),
        jax.ShapeDtypeStruct((ATTN_WIDTH, n), BF16),
        jax.ShapeDtypeStruct((n, 2 * ATTN_WIDTH), BF16),
        jax.ShapeDtypeStruct((n, LANES), BF16),
        jax.ShapeDtypeStruct((IDX_HEADS, n), F32),
        jax.ShapeDtypeStruct((n, cw), F32),
    ]
    col = lambda i: (0, i)
    token_minor = (False, False, True, False, False, True, False)
    out_specs = [pl.BlockSpec((o.shape[0], tm), col) if tmin else pl.BlockSpec((tm, o.shape[1]), row)
                 for o, tmin in zip(outs, token_minor)]
    return pl.pallas_call(
        _inproj_kernel,
        grid=(n // tm,),
        in_specs=[
            pl.BlockSpec((tm, d), row),
            pl.BlockSpec((1, 1, d), bat),
            pl.BlockSpec((1, 1, d), bat),
            pl.BlockSpec(wa.shape, const),
            pl.BlockSpec(wk.shape, const),
            pl.BlockSpec(ww.shape, const),
            pl.BlockSpec(wc.shape, const),
            pl.BlockSpec(gq.shape, const),
            pl.BlockSpec(gk.shape, const),
            pl.BlockSpec(lng.shape, const),
            pl.BlockSpec(lnb.shape, const),
            pl.BlockSpec((tm, LANES), row),
            pl.BlockSpec((tm, LANES), row),
            pl.BlockSpec((tm, LANES), row),
            pl.BlockSpec(gsum.shape, const),
        ],
        out_specs=out_specs,
        out_shape=outs,
        compiler_params=_cparams(("arbitrary",)),
        name="inproj",
    )(x2, sh, sc, wa, wk, ww, wc, gq, gk, lng, lnb, rc, rs1, rs2, gsum)


CONV_HALO = 32
CONV_ROWS = 64


def _conv_kernel(cur_ref, halo_ref, w_ref, b_ref, lg_ref, lb_ref, og_ref, o_ref, ext_ref, *, tt):
    i = pl.program_id(1)
    ext_ref[0, 0:CONV_HALO, :] = jnp.where(i > 0, halo_ref[0], 0.0)
    ext_ref[0, CONV_HALO:CONV_HALO + tt, :] = cur_ref[0]
    rows = CONV_HALO + tt - SUBLANES
    for s in range(1, SUBLANES):
        ext_ref[s, 0:rows, :] = ext_ref[0, s:s + rows, :]
    lead = CONV_HALO - (CONV_KERNEL - 1)
    for r0 in range(0, tt, CONV_ROWS):
        acc = jnp.zeros((CONV_ROWS, ext_ref.shape[2]), F32)
        for j in range(CONV_KERNEL):
            s = (lead + j) % SUBLANES
            base = r0 + lead + j - s
            acc = acc + ext_ref[s, base:base + CONV_ROWS, :] * w_ref[j:j + 1, :]
        hcv = acc + b_ref[...]
        mu = jnp.mean(hcv, axis=-1, keepdims=True)
        var = jnp.mean(jnp.square(hcv - mu), axis=-1, keepdims=True)
        y = (hcv - mu) * lax.rsqrt(var + NORM_EPS) * lg_ref[...] + lb_ref[...]
        y = y * jax.nn.sigmoid(y)
        ms = jnp.mean(y * y, axis=-1, keepdims=True)
        o_ref[0, r0:r0 + CONV_ROWS, :] = (y * lax.rsqrt(ms + NORM_EPS) * og_ref[...]).astype(BF16)


def _conv(hg3, w_pad, b, lg, lb, og, tt):
    bsz, seq, cw = hg3.shape
    hb = tt // CONV_HALO
    const = lambda bb, i: (0, 0)
    return pl.pallas_call(
        functools.partial(_conv_kernel, tt=tt),
        grid=(bsz, seq // tt),
        in_specs=[
            pl.BlockSpec((1, tt, cw), lambda bb, i: (bb, i, 0)),
            pl.BlockSpec((1, CONV_HALO, cw), lambda bb, i: (bb, jnp.maximum(i * hb - 1, 0), 0)),
            pl.BlockSpec(w_pad.shape, const),
            pl.BlockSpec(b.shape, const),
            pl.BlockSpec(lg.shape, const),
            pl.BlockSpec(lb.shape, const),
            pl.BlockSpec(og.shape, const),
        ],
        out_specs=pl.BlockSpec((1, tt, cw), lambda bb, i: (bb, i, 0)),
        out_shape=jax.ShapeDtypeStruct((bsz, seq, cw), BF16),
        scratch_shapes=[pltpu.VMEM((SUBLANES, CONV_HALO + tt, cw), F32)],
        compiler_params=_cparams(("arbitrary", "arbitrary")),
        name="conv",
    )(hg3, hg3, w_pad, b, lg, lb, og)


def _attn_kernel(q_ref, qi_ref, wit_ref, k_ref, vt_ref, kk_ref, og_ref, o_ref,
                 key_ref, k16_ref, bias_ref, m_ref, a_ref, acc_ref, lm_ref, p_ref, *, tq, n_sel, seq):
    i = pl.program_id(1)
    nk = i + 1
    krow = lax.broadcasted_iota(I32, (tq, tq), 0)
    qcol = lax.broadcasted_iota(I32, (tq, tq), 1) + i * tq

    def chunk_rows(c):
        return pl.ds(pl.multiple_of(c * tq, tq), tq)

    wit = wit_ref[...]

    def score_pair(cp, carry):
        for c in (2 * cp, 2 * cp + 1):
            kic = kk_ref[0, chunk_rows(c), :]
            s = jnp.zeros((tq, tq), F32)
            for h in range(IDX_HEADS):
                qh = qi_ref[0, :, h * LANES:(h + 1) * LANES]
                s = s + wit[h:h + 1, :] * jnp.maximum(_nt_dot(kic, qh), 0.0)
            s = jnp.where(krow + c * tq <= qcol, s, -jnp.inf)
            bits = pltpu.bitcast(s, I32)
            key = jnp.where(bits < 0, bits ^ jnp.int32(0x7FFFFFFF), bits)
            key_ref[c] = jnp.where(s == 0.0, 0, key)
        return carry

    lax.fori_loop(0, (nk + 1) // 2, score_pair, 0)

    def count(pred_fn):
        def body(c, acc):
            hit = jnp.where(pred_fn(c, key_ref[c]), 1, 0)
            return acc + jnp.sum(hit.reshape(tq // SUBLANES, SUBLANES, tq), axis=0)
        acc = lax.fori_loop(0, nk, body, jnp.zeros((SUBLANES, tq), I32))
        return jnp.sum(acc, axis=0, keepdims=True)

    def count_ge(thr):
        return count(lambda c, key: key >= thr)

    grp = tq // PACK16
    one_h, zero_h = jnp.ones((), BF16), jnp.zeros((), BF16)

    def count16(thr):
        thr16 = jnp.broadcast_to(thr, (PACK16, tq)).astype(jnp.int16)

        def body(cp, acc):
            for c in (2 * cp, 2 * cp + 1):
                hit = jnp.where(k16_ref[c].reshape(grp, PACK16, tq) >= thr16[None], one_h, zero_h)
                parts = [hit[r] for r in range(grp)]
                while len(parts) > 1:
                    parts = [parts[r] + parts[r + 1] for r in range(0, len(parts), 2)]
                acc = acc + parts[0].astype(F32)
            return acc

        acc = lax.fori_loop(0, (nk + 1) // 2, body, jnp.zeros((PACK16, tq), F32))
        return jnp.sum(acc, axis=0, keepdims=True).astype(I32)

    def search16():
        def step(bi, cand):
            trial = cand + jnp.left_shift(jnp.int32(1), 15 - bi)
            return jnp.where(count16(trial) >= n_sel, trial, cand)
        return lax.fori_loop(0, 16, step, jnp.full((1, tq), INT16_MIN, I32))

    def store_upper(c, carry):
        k16_ref[c] = lax.shift_right_arithmetic(key_ref[c], 16).astype(jnp.int16)
        return carry

    lax.fori_loop(0, nk, store_upper, 0)

    @pl.when(nk % 2 == 1)
    def _():
        k16_ref[nk] = jnp.full((tq, tq), INT16_MIN, jnp.int16)

    tau_hi = search16()

    def store_lower(c, carry):
        key = key_ref[c]
        hi = lax.shift_right_arithmetic(key, 16)
        lo = (key & 0xFFFF) + INT16_MIN
        k16_ref[c] = jnp.where(hi == tau_hi, lo, jnp.where(hi > tau_hi, -INT16_MIN - 1, INT16_MIN)
                               ).astype(jnp.int16)
        return carry

    lax.fori_loop(0, nk, store_lower, 0)
    tau = jnp.left_shift(tau_hi, 16) | (search16() - INT16_MIN)
    n_ge = count_ge(tau)
    need = n_sel - count_ge(tau + 1)

    def tie_cut():
        def step(bi, jm):
            trial = jm + jnp.left_shift(jnp.int32(1), (seq.bit_length() - 2) - bi)
            f = count(lambda c, key: jnp.where(key == tau, krow + c * tq, seq) < trial)
            return jnp.where(f < need, trial, jm)
        return lax.fori_loop(0, seq.bit_length() - 1, step, jnp.zeros((1, tq), I32))

    has_tie = jnp.max(n_ge) > n_sel
    jm = lax.cond(has_tie, tie_cut, lambda: jnp.full((1, tq), seq, I32))

    def bias_chunk(c, carry):
        key = key_ref[c]
        kidx = krow + c * tq
        tied = jnp.where(kidx <= jm, 0.0, NEG_BIG)
        above = jnp.where(key > tau, 0.0, NEG_BIG)
        bias_ref[c] = jnp.where(kidx <= qcol, jnp.where(key == tau, tied, above), NEG_BIG)
        return carry

    lax.fori_loop(0, nk, bias_chunk, 0)

    m_ref[...] = jnp.full(m_ref.shape, NEG_BIG, F32)
    acc_ref[...] = jnp.zeros(acc_ref.shape, F32)
    ones_rows = jnp.ones((PACK16, tq), BF16)

    def logits(c, slot):
        rows = chunk_rows(c)
        for hd in range(N_HEADS):
            p = hd // 2
            kp = k_ref[0, rows, p * LANES:(p + 1) * LANES]
            qh = q_ref[0, :, hd * LANES:(hd + 1) * LANES]
            lm_ref[slot, hd] = _nt_dot(kp, qh) + bias_ref[c]

    def softmax_pv(c, slot):
        for hd in range(N_HEADS):
            m_old = m_ref[hd]
            m_new = jnp.maximum(m_old, jnp.max(lm_ref[slot, hd], axis=0, keepdims=True))
            m_ref[hd] = m_new
            a_ref[hd] = jnp.exp2(m_old - m_new)
            p_ref[hd] = jnp.exp2(lm_ref[slot, hd] - m_new).astype(BF16)
        rows = chunk_rows(c)
        for hd in range(N_HEADS):
            vth = jnp.concatenate([vt_ref[hd * HEAD_DIM:(hd + 1) * HEAD_DIM, rows], ones_rows], axis=0)
            acc_ref[hd] = acc_ref[hd] * a_ref[hd] + _dot(vth, p_ref[hd])

    def attn_pair(cp, carry):
        c0 = 2 * cp
        logits(c0 + 1, 1)
        softmax_pv(c0, 0)
        logits(jnp.minimum(c0 + 2, last_c), 0)
        softmax_pv(c0 + 1, 1)
        return carry

    @pl.when(nk % 2 == 1)
    def _():
        bias_ref[nk] = jnp.full((tq, tq), NEG_BIG, F32)

    last_c = nk - 1 + nk % 2
    logits(0, 0)
    lax.fori_loop(0, (nk + 1) // 2, attn_pair, 0)

    ot = jnp.concatenate([acc_ref[hd, :HEAD_DIM] / acc_ref[hd, HEAD_DIM:HEAD_DIM + 1]
                          for hd in range(N_HEADS)], axis=0)
    ms = jnp.mean(ot * ot, axis=0, keepdims=True)
    ot = ot * lax.rsqrt(ms + NORM_EPS)
    o_ref[0] = (jnp.transpose(ot) * og_ref[...]).astype(BF16)


def _attention(q3, k3, vt, qi3, kk3, wit, og, tq):
    bsz, seq, w = k3.shape
    n_sel = min(TOPK_KEYS, seq // 4)
    nc = seq // tq
    assert nc % 2 == 0, "the kernel walks key chunks in pairs"
    qblk = lambda bb, i: (bb, i, 0)
    full = lambda bb, i: (bb, 0, 0)
    return pl.pallas_call(
        functools.partial(_attn_kernel, tq=tq, n_sel=n_sel, seq=seq),
        grid=(bsz, nc),
        in_specs=[
            pl.BlockSpec((1, tq, 2 * w), qblk),
            pl.BlockSpec((1, tq, 2 * w), qblk),
            pl.BlockSpec((SUBLANES, tq), lambda bb, i: (0, bb * nc + i)),
            pl.BlockSpec((1, seq, w), full),
            pl.BlockSpec((w, seq), lambda bb, i: (0, bb)),
            pl.BlockSpec((1, seq, LANES), full),
            pl.BlockSpec(og.shape, lambda bb, i: (0, 0)),
        ],
        out_specs=pl.BlockSpec((1, tq, w), qblk),
        out_shape=jax.ShapeDtypeStruct((bsz, seq, w), BF16),
        scratch_shapes=[
            pltpu.VMEM((nc, tq, tq), I32),
            pltpu.VMEM((nc, tq, tq), jnp.int16),
            pltpu.VMEM((nc, tq, tq), F32),
            pltpu.VMEM((N_HEADS, 1, tq), F32),
            pltpu.VMEM((N_HEADS, 1, tq), F32),
            pltpu.VMEM((N_HEADS, HEAD_DIM + PACK16, tq), F32),
            pltpu.VMEM((2, N_HEADS, tq, tq), F32),
            pltpu.VMEM((N_HEADS, tq, tq), BF16),
        ],
        compiler_params=_cparams(("arbitrary", "arbitrary")),
        name="dsa_attention",
    )(q3, qi3, wit, k3, vt, kk3, og)


def _outproj_kernel(an_ref, cn_ref, x_ref, g1_ref, sh_ref, sc_ref, woa_ref, woc_ref, wq_ref,
                    x1_ref, h2_ref, pq_ref):
    y = _dot(an_ref[...], woa_ref[...]) + _dot(cn_ref[...], woc_ref[...])
    x1 = x_ref[...] + g1_ref[0] * y
    x1_ref[...] = x1
    ms = jnp.mean(x1 * x1, axis=-1, keepdims=True)
    h2 = (x1 * lax.rsqrt(ms + NORM_EPS) * (1.0 + sc_ref[0]) + sh_ref[0]).astype(BF16)
    h2_ref[...] = h2
    pq_ref[...] = _dot(h2, wq_ref[...]).astype(BF16)


def _outproj(an, cn, x2, g1, sh2, sc2, woa, woc, wq, seq, tm):
    n, d = x2.shape
    per_b = seq // tm
    row = lambda i: (i, 0)
    const = lambda i: (0, 0)
    bat = lambda i: (i // per_b, 0, 0)
    return pl.pallas_call(
        _outproj_kernel,
        grid=(n // tm,),
        in_specs=[
            pl.BlockSpec((tm, an.shape[1]), row),
            pl.BlockSpec((tm, cn.shape[1]), row),
            pl.BlockSpec((tm, d), row),
            pl.BlockSpec((1, 1, d), bat),
            pl.BlockSpec((1, 1, d), bat),
            pl.BlockSpec((1, 1, d), bat),
            pl.BlockSpec(woa.shape, const),
            pl.BlockSpec(woc.shape, const),
            pl.BlockSpec(wq.shape, const),
        ],
        out_specs=[pl.BlockSpec((tm, d), row), pl.BlockSpec((tm, d), row),
                   pl.BlockSpec((tm, wq.shape[1]), row)],
        out_shape=[jax.ShapeDtypeStruct((n, d), F32), jax.ShapeDtypeStruct((n, d), BF16),
                   jax.ShapeDtypeStruct((n, wq.shape[1]), BF16)],
        compiler_params=_cparams(("arbitrary",)),
        name="outproj",
    )(an, cn, x2, g1, sh2, sc2, woa, woc, wq)


def _top_rows(s, rank=None, ids=None):
    return _top_rows_multi([(s, rank, ids)])[0]


def _top_rows_multi(problems):
    state = []
    for s, rank, ids in problems:
        if rank is None:
            rank = lax.broadcasted_iota(I32, s.shape, 0)
        state.append([s, rank, ids, [], []])
    for _ in range(PEER_TOPK):
        for st in state:
            s, rank, ids, vals, picks = st
            m = jnp.max(s, axis=0, keepdims=True)
            pos = jnp.min(jnp.where(s == m, rank, 2 ** 30), axis=0, keepdims=True)
            hit = rank == pos
            vals.append(m)
            picks.append(pos if ids is None else jnp.max(jnp.where(hit, ids, -1), axis=0, keepdims=True))
            st[0] = jnp.where(hit, -jnp.inf, s)
    return [(jnp.concatenate(st[3], axis=0), jnp.concatenate(st[4], axis=0)) for st in state]


def _pair_candidates(v1, i1, v2, i2):
    tr = v1.shape[1]
    k = PEER_TOPK
    r8 = lax.broadcasted_iota(I32, (SUBLANES, tr), 0)
    rk = lax.broadcasted_iota(I32, (k, tr), 0)
    sums = [v1 + v2[0:1]]
    eids = [i1 * PEER_KEYS + i2[0:1]]
    order = [rk * k]
    for b in range(1, SUBLANES):
        ok = r8 < k // (b + 1)
        sums.append(jnp.where(ok, v1[0:SUBLANES] + v2[b:b + 1], -jnp.inf))
        eids.append(i1[0:SUBLANES] * PEER_KEYS + i2[b:b + 1])
        order.append(r8 * k + b)
    sums.append(v1[0:1] + v2[SUBLANES:k])
    eids.append(i1[0:1] * PEER_KEYS + i2[SUBLANES:k])
    order.append(r8 + SUBLANES)
    return jnp.concatenate(sums, axis=0), jnp.concatenate(eids, axis=0), jnp.concatenate(order, axis=0)


def _route_kernel(pq_ref, sk_ref, a_ref, b_ref, g_ref, e_scr, g_scr):
    h = pl.program_id(1)
    half = PEER_KEYS
    s1 = _nt_dot(sk_ref[0, 0], pq_ref[:, :half])
    s2 = _nt_dot(sk_ref[0, 1], pq_ref[:, half:])
    (v1, i1), (v2, i2) = _top_rows_multi([(s1, None, None), (s2, None, None)])
    cand, cid, order = _pair_candidates(v1, i1, v2, i2)
    ts, e = _top_rows(cand, order, cid)
    ex = jnp.exp(ts - ts[0:1])
    g = ex / jnp.sum(ex, axis=0, keepdims=True)
    rows = pl.ds(pl.multiple_of(h * PEER_TOPK, PEER_TOPK), PEER_TOPK)
    e_scr[rows, :] = e
    g_scr[rows, :] = g

    @pl.when(h == PEER_HEADS - 1)
    def _():
        e_all = e_scr[...]
        a_ref[...] = jnp.transpose(jnp.right_shift(e_all, 7).astype(F32))
        b_ref[...] = jnp.transpose(jnp.bitwise_and(e_all, PEER_KEYS - 1).astype(F32))
        g_ref[...] = jnp.transpose(g_scr[...])


def _route(pq, sk, tr):
    n = pq.shape[0]
    qd = 2 * PEER_KEYS
    out = jax.ShapeDtypeStruct((n, PEER_PAIRS), F32)
    oblk = pl.BlockSpec((tr, PEER_PAIRS), lambda i, h: (i, 0))
    return pl.pallas_call(
        _route_kernel,
        grid=(n // tr, PEER_HEADS),
        in_specs=[
            pl.BlockSpec((tr, qd), lambda i, h: (i, h)),
            pl.BlockSpec((1, 2, PEER_KEYS, PEER_KEYS), lambda i, h: (h, 0, 0, 0)),
        ],
        out_specs=[oblk, oblk, oblk],
        out_shape=[out, out, out],
        scratch_shapes=[pltpu.VMEM((PEER_PAIRS, tr), I32), pltpu.VMEM((PEER_PAIRS, tr), F32)],
        compiler_params=_cparams(("arbitrary", "arbitrary")),
        name="peer_route",
    )(pq, sk)


C_HALF = PEER_KEYS // 2
C_PITCH = C_HALF + SUBLANES
C_UNROLL = 16
HI16 = -65536


def _gelu(x):
    return 0.5 * x * (1.0 + lax.erf(x * (2.0 ** -0.5)))


def _bf16_bits(x):
    return pltpu.bitcast(x.astype(BF16).astype(F32), I32)


def _peer_kernel(h2_ref, a_ref, b_ref, g_ref, ulo_ref, uhi_ref, vlo_ref, vhi_ref, x1_ref, g2_ref, o_ref,
                 c_scr, acc_ref, *, tm, ec):
    j = pl.program_id(1)
    nj = pl.num_programs(1)
    k1 = ec // 2 // PEER_KEYS

    @pl.when(j == 0)
    def _():
        srow = lax.broadcasted_iota(I32, (PEER_KEYS, PEER_PAIRS), 0).astype(F32)

        def tokens(tb, carry):
            for u in range(C_UNROLL):
                t = tb * C_UNROLL + u
                a_row = a_ref[pl.ds(t, 1), :]
                b_row = b_ref[pl.ds(t, 1), :]
                g_row = g_ref[pl.ds(t, 1), :]
                at = jnp.where(srow == a_row, g_row, 0.0).astype(BF16)
                bt = jnp.where(srow == b_row, 1.0, 0.0).astype(BF16)
                ct = _nt_dot(at, bt)
                word = (lax.shift_right_logical(_bf16_bits(ct[:C_HALF]), 16)
                        | (_bf16_bits(ct[C_HALF:]) & HI16))
                c_scr[pl.ds(pl.multiple_of(t * C_PITCH, SUBLANES), C_HALF), :] = word
            return carry

        lax.fori_loop(0, tm // C_UNROLL, tokens, 0)
        acc_ref[...] = jnp.zeros(acc_ref.shape, F32)

    h2 = h2_ref[...]
    act_lo = _gelu(_dot(h2, ulo_ref[...]))
    act_hi = _gelu(_dot(h2, uhi_ref[...]))
    p_lo, p_hi = [], []
    for ii in range(k1):
        word = c_scr[pl.ds(j * k1 + ii, tm, stride=C_PITCH), :]
        c_lo = pltpu.bitcast(lax.shift_left(word, 16), F32)
        c_hi = pltpu.bitcast(word & HI16, F32)
        cols = slice(ii * PEER_KEYS, (ii + 1) * PEER_KEYS)
        p_lo.append((act_lo[:, cols] * c_lo).astype(BF16))
        p_hi.append((act_hi[:, cols] * c_hi).astype(BF16))
    acc_ref[...] += (_dot(jnp.concatenate(p_lo, axis=1), vlo_ref[...])
                     + _dot(jnp.concatenate(p_hi, axis=1), vhi_ref[...]))

    @pl.when(j == nj - 1)
    def _():
        o_ref[...] = x1_ref[...] + g2_ref[0] * acc_ref[...]


def _peer(h2, a, b, g, ut, v, x1, g2, seq, tm, ec):
    n, d = x1.shape
    ne = v.shape[0]
    per_b = seq // tm
    eb = ec // 2
    nhalf = ne // 2 // eb
    row = lambda i, j: (i, 0)
    lo = lambda i, j: (j, 0)
    hi = lambda i, j: (j + nhalf, 0)
    return pl.pallas_call(
        functools.partial(_peer_kernel, tm=tm, ec=ec),
        grid=(n // tm, ne // ec),
        in_specs=[
            pl.BlockSpec((tm, d), row),
            pl.BlockSpec((tm, PEER_PAIRS), row),
            pl.BlockSpec((tm, PEER_PAIRS), row),
            pl.BlockSpec((tm, PEER_PAIRS), row),
            pl.BlockSpec((d, eb), lambda i, j: (0, j)),
            pl.BlockSpec((d, eb), lambda i, j: (0, j + nhalf)),
            pl.BlockSpec((eb, d), lo),
            pl.BlockSpec((eb, d), hi),
            pl.BlockSpec((tm, d), row),
            pl.BlockSpec((1, 1, d), lambda i, j: (i // per_b, 0, 0)),
        ],
        out_specs=pl.BlockSpec((tm, d), row),
        out_shape=jax.ShapeDtypeStruct((n, d), F32),
        scratch_shapes=[pltpu.VMEM((tm * C_PITCH, PEER_KEYS), I32), pltpu.VMEM((tm, d), F32)],
        compiler_params=_cparams(("arbitrary", "arbitrary")),
        name="peer_experts",
    )(h2, a, b, g, ut, ut, v, v, x1, g2)


def _rope_lane_tables(positions):
    half = ROPE_DIM // 2
    inv_freq = jnp.power(jnp.float32(ROPE_THETA), -jnp.arange(0, ROPE_DIM, 2, dtype=F32) / ROPE_DIM)
    ang = positions.astype(F32).reshape(-1, 1) * inv_freq
    cos, sin = jnp.cos(ang), jnp.sin(ang)
    j = jnp.arange(LANES) % HEAD_DIM
    cos_l, sin_l = cos[:, j % half], sin[:, j % half]
    rc = jnp.where(j < ROPE_DIM, cos_l, 1.0)
    rs1 = jnp.where(j < half, -sin_l, 0.0)
    rs2 = jnp.where((j >= half) & (j < ROPE_DIM), sin_l, 0.0)
    return rc, rs1, rs2


def _tile_size(total, want):
    t = min(want, total)
    assert total % t == 0
    return t


def kernel(x, c, positions, w_ada, b_ada, w_in, q_norm_g, k_norm_g, idx_k_ln_g, idx_k_ln_b,
           conv_dw, conv_b, conv_ln_g, conv_ln_b, out_g_attn, out_g_conv, w_out, peer_wq,
           peer_sub_keys, peer_u, peer_v):
    bsz, seq, d = x.shape
    depth = w_ada.shape[0]
    n = bsz * seq
    w = ATTN_WIDTH
    cw = d - w
    assert bsz <= SUBLANES and cw % LANES == 0

    tm = _tile_size(seq, 512)
    tq = _tile_size(seq, 256)
    tt = _tile_size(seq, 512)
    tr = _tile_size(seq, 256)
    tp = _tile_size(seq, 512)
    ec = 2048

    c_pad = jnp.zeros((SUBLANES, d), F32).at[:bsz].set(c)
    mod = _ada(c_pad, w_ada, b_ada)[:, :bsz]
    rc, rs1, rs2 = _rope_lane_tables(positions)
    hd_id = jnp.arange(w) // HEAD_DIM
    gsum = (hd_id[:, None] == hd_id[None, :]).astype(BF16)

    o_q, o_k, o_v, o_qi, o_ki, o_wi = 0, w, 2 * w, 3 * w, 4 * w, 4 * w + IDX_DIM
    o_cu = o_wi + IDX_HEADS

    x2 = x.reshape(n, d)
    for l in range(depth):
        sh1, sc1, g1, sh2, sc2, g2 = [mod[l, :, i * d:(i + 1) * d].reshape(bsz, 1, d) for i in range(6)]
        wl = w_in[l]
        wa = wl[:, o_q:o_ki].astype(BF16)
        wki = wl[:, o_ki:o_wi]
        wk = jnp.concatenate([wki, wki], axis=1).astype(BF16)
        ww = jnp.pad(wl[:, o_wi:o_cu], ((0, 0), (0, LANES - IDX_HEADS))).astype(BF16)
        wc = wl[:, o_cu:].astype(BF16)
        tile_h = lambda g_: jnp.tile(g_, N_HEADS).reshape(1, w)
        dup = lambda g_: jnp.concatenate([g_, g_]).reshape(1, LANES)
        q, k, vt, qi, kk, wit, hg = _inproj(
            x2, sh1, sc1, wa, wk, ww, wc, tile_h(q_norm_g[l]), tile_h(k_norm_g[l]),
            dup(idx_k_ln_g[l]), dup(idx_k_ln_b[l]), rc, rs1, rs2, gsum, seq, tm)

        r3 = lambda t: t.reshape(bsz, seq, t.shape[-1])
        an = _attention(r3(q), r3(k), vt, r3(qi), r3(kk), wit, out_g_attn[l].reshape(1, w), tq)
        w_pad = jnp.pad(conv_dw[l], ((0, CONV_HALO - CONV_KERNEL), (0, 0)))
        cn = _conv(r3(hg), w_pad, conv_b[l].reshape(1, cw), conv_ln_g[l].reshape(1, cw),
                   conv_ln_b[l].reshape(1, cw), out_g_conv[l].reshape(1, cw), tt)

        wo = w_out[l].astype(BF16)
        x1, h2, pq = _outproj(an.reshape(n, w), cn.reshape(n, cw), x2, g1, sh2, sc2,
                              wo[:w], wo[w:], peer_wq[l].astype(BF16), seq, tm)
        a, b, g = _route(pq, peer_sub_keys[l].astype(BF16), tr)
        x2 = _peer(h2, a, b, g, jnp.transpose(peer_u[l]).astype(BF16), peer_v[l].astype(BF16), x1, g2,
                   seq, tp, ec)
    return x2.reshape(bsz, seq, d)
```

```python
import functools

import jax
import jax.numpy as jnp
from jax import lax
from jax.experimental import pallas as pl
from jax.experimental.pallas import tpu as pltpu

F32 = jnp.float32
BF16 = jnp.bfloat16
I32 = jnp.int32

N_HEADS = 8
HEAD_DIM = 64
ATTN_WIDTH = N_HEADS * HEAD_DIM
CONV_KERNEL = 31
IDX_HEADS = 8
IDX_DIM = 64
TOPK_KEYS = 256
ROPE_THETA = 500000.0
ROPE_DIM = HEAD_DIM // 4
PEER_HEADS = 8
PEER_KEYS = 128
PEER_TOPK = 16
PEER_PAIRS = PEER_HEADS * PEER_TOPK
NORM_EPS = 1e-6
LOG2_E = 1.4426950408889634
Q_SCALE = HEAD_DIM ** -0.5 * LOG2_E

LANES = 128
SUBLANES = 8
VMEM_LIMIT = 56 * 1024 * 1024

PACK16 = 2 * SUBLANES

NEG_BIG = -1e30
INT_MIN = -2 ** 31
INT16_MIN = -2 ** 15


def _cparams(sem):
    return pltpu.CompilerParams(dimension_semantics=sem, vmem_limit_bytes=VMEM_LIMIT)


def _nt_dot(a, b):
    return lax.dot_general(a, b, (((1,), (1,)), ((), ())), preferred_element_type=F32)


def _dot(a, b):
    return jnp.dot(a, b, preferred_element_type=F32)


def _ada_kernel(c_ref, w_ref, b_ref, o_ref):
    c = c_ref[...]
    ca = c * jax.nn.sigmoid(c)
    o_ref[0] = _dot(ca, w_ref[0]) + b_ref[0]


def _ada(c_pad, w_ada, b_ada):
    depth, d, n6 = w_ada.shape
    tn = n6 // 4
    return pl.pallas_call(
        _ada_kernel,
        grid=(depth, n6 // tn),
        in_specs=[
            pl.BlockSpec((SUBLANES, d), lambda l, j: (0, 0)),
            pl.BlockSpec((1, d, tn), lambda l, j: (l, 0, j)),
            pl.BlockSpec((1, 1, tn), lambda l, j: (l, 0, j)),
        ],
        out_specs=pl.BlockSpec((1, SUBLANES, tn), lambda l, j: (l, 0, j)),
        out_shape=jax.ShapeDtypeStruct((depth, SUBLANES, n6), F32),
        compiler_params=_cparams(("arbitrary", "arbitrary")),
        name="adaln",
    )(c_pad, w_ada, b_ada.reshape(depth, 1, n6))


def _rope(x, c, s1, s2):
    w = x.shape[-1]
    half = ROPE_DIM // 2
    return x * c + pltpu.roll(x, w - half, 1) * s1 + pltpu.roll(x, half, 1) * s2


def _split_hi_lo(x):
    hi = x.astype(BF16)
    lo = (x - hi.astype(F32)).astype(BF16)
    return hi, lo


def _inproj_kernel(x_ref, sh_ref, sc_ref, wa_ref, wk_ref, ww_ref, wc_ref, gq_ref, gk_ref,
                   lng_ref, lnb_ref, rc_ref, rs1_ref, rs2_ref, gsum_ref,
                   q_ref, k_ref, vt_ref, qi_ref, kk_ref, wit_ref, hg_ref):
    x = x_ref[...]
    ms = jnp.mean(x * x, axis=-1, keepdims=True)
    h = x * lax.rsqrt(ms + NORM_EPS) * (1.0 + sc_ref[0]) + sh_ref[0]
    hb = h.astype(BF16)

    rc1, rs11, rs21 = rc_ref[...], rs1_ref[...], rs2_ref[...]
    reps = ATTN_WIDTH // LANES
    rc = jnp.concatenate([rc1] * reps, axis=1)
    rs1 = jnp.concatenate([rs11] * reps, axis=1)
    rs2 = jnp.concatenate([rs21] * reps, axis=1)
    gsum = gsum_ref[...]

    def head_norm_rope(t, gain):
        hi, lo = _split_hi_lo(t * t)
        ssq = _dot(hi, gsum) + _dot(lo, gsum)
        tn = t * lax.rsqrt(ssq * (1.0 / HEAD_DIM) + NORM_EPS) * gain
        return _rope(tn, rc, rs1, rs2)

    lo_half = lax.broadcasted_iota(I32, (x.shape[0], LANES), 1) < HEAD_DIM

    def one_head_per_group(t):
        parts = []
        for p in range(reps):
            g = t[:, p * LANES:(p + 1) * LANES]
            parts += [jnp.where(lo_half, g, 0.0), jnp.where(lo_half, 0.0, g)]
        return jnp.concatenate(parts, axis=1).astype(BF16)

    w = ATTN_WIDTH
    pa = _dot(hb, wa_ref[...])
    q_ref[...] = one_head_per_group(head_norm_rope(pa[:, 0:w], gq_ref[...]) * Q_SCALE)
    k_ref[...] = head_norm_rope(pa[:, w:2 * w], gk_ref[...]).astype(BF16)
    vt_ref[...] = jnp.transpose(pa[:, 2 * w:3 * w]).astype(BF16)
    qi_ref[...] = one_head_per_group(_rope(pa[:, 3 * w:4 * w], rc, rs1, rs2) * (IDX_DIM ** -0.5))

    pk = _dot(hb, wk_ref[...])
    mu = jnp.mean(pk, axis=-1, keepdims=True)
    var = jnp.mean(jnp.square(pk - mu), axis=-1, keepdims=True)
    kn = (pk - mu) * lax.rsqrt(var + NORM_EPS) * lng_ref[...] + lnb_ref[...]
    kk_ref[...] = _rope(kn, rc1, rs11, rs21).astype(BF16)

    wit = jnp.transpose(_dot(hb, ww_ref[...]) * (IDX_HEADS ** -0.5))
    wit_ref[...] = wit[:IDX_HEADS]

    pc = _dot(hb, wc_ref[...])
    cw = pc.shape[1] // 2
    hg_ref[...] = pc[:, :cw] * jax.nn.sigmoid(pc[:, cw:])


def _inproj(x2, sh, sc, wa, wk, ww, wc, gq, gk, lng, lnb, rc, rs1, rs2, gsum, seq, tm):
    n, d = x2.shape
    per_b = seq // tm
    row = lambda i: (i, 0)
    const = lambda i: (0, 0)
    bat = lambda i: (i // per_b, 0, 0)
    cw = wc.shape[1] // 2
    outs = [
        jax.ShapeDtypeStruct((n, 2 * ATTN_WIDTH), BF16),
        jax.ShapeDtypeStruct((n, ATTN_WIDTH), BF16),
        jax.ShapeDtypeStruct((ATTN_WIDTH, n), BF16),
        jax.ShapeDtypeStruct((n, 2 * ATTN_WIDTH), BF16),
        jax.ShapeDtypeStruct((n, LANES), BF16),
        jax.ShapeDtypeStruct((IDX_HEADS, n), F32),
        jax.ShapeDtypeStruct((n, cw), F32),
    ]
    col = lambda i: (0, i)
    token_minor = (False, False, True, False, False, True, False)
    out_specs = [pl.BlockSpec((o.shape[0], tm), col) if tmin else pl.BlockSpec((tm, o.shape[1]), row)
                 for o, tmin in zip(outs, token_minor)]
    return pl.pallas_call(
        _inproj_kernel,
        grid=(n // tm,),
        in_specs=[
            pl.BlockSpec((tm, d), row),
            pl.BlockSpec((1, 1, d), bat),
            pl.BlockSpec((1, 1, d), bat),
            pl.BlockSpec(wa.shape, const),
            pl.BlockSpec(wk.shape, const),
            pl.BlockSpec(ww.shape, const),
            pl.BlockSpec(wc.shape, const),
            pl.BlockSpec(gq.shape, const),
            pl.BlockSpec(gk.shape, const),
            pl.BlockSpec(lng.shape, const),
            pl.BlockSpec(lnb.shape, const),
            pl.BlockSpec((tm, LANES), row),
            pl.BlockSpec((tm, LANES), row),
            pl.BlockSpec((tm, LANES), row),
            pl.BlockSpec(gsum.shape, const),
        ],
        out_specs=out_specs,
        out_shape=outs,
        compiler_params=_cparams(("arbitrary",)),
        name="inproj",
    )(x2, sh, sc, wa, wk, ww, wc, gq, gk, lng, lnb, rc, rs1, rs2, gsum)


CONV_HALO = 32
CONV_ROWS = 64


def _conv_kernel(cur_ref, halo_ref, w_ref, b_ref, lg_ref, lb_ref, og_ref, o_ref, ext_ref, *, tt):
    i = pl.program_id(1)
    ext_ref[0, 0:CONV_HALO, :] = jnp.where(i > 0, halo_ref[0], 0.0)
    ext_ref[0, CONV_HALO:CONV_HALO + tt, :] = cur_ref[0]
    rows = CONV_HALO + tt - SUBLANES
    for s in range(1, SUBLANES):
        ext_ref[s, 0:rows, :] = ext_ref[0, s:s + rows, :]
    lead = CONV_HALO - (CONV_KERNEL - 1)
    for r0 in range(0, tt, CONV_ROWS):
        acc = jnp.zeros((CONV_ROWS, ext_ref.shape[2]), F32)
        for j in range(CONV_KERNEL):
            s = (lead + j) % SUBLANES
            base = r0 + lead + j - s
            acc = acc + ext_ref[s, base:base + CONV_ROWS, :] * w_ref[j:j + 1, :]
        hcv = acc + b_ref[...]
        mu = jnp.mean(hcv, axis=-1, keepdims=True)
        var = jnp.mean(jnp.square(hcv - mu), axis=-1, keepdims=True)
        y = (hcv - mu) * lax.rsqrt(var + NORM_EPS) * lg_ref[...] + lb_ref[...]
        y = y * jax.nn.sigmoid(y)
        ms = jnp.mean(y * y, axis=-1, keepdims=True)
        o_ref[0, r0:r0 + CONV_ROWS, :] = (y * lax.rsqrt(ms + NORM_EPS) * og_ref[...]).astype(BF16)


def _conv(hg3, w_pad, b, lg, lb, og, tt):
    bsz, seq, cw = hg3.shape
    hb = tt // CONV_HALO
    const = lambda bb, i: (0, 0)
    return pl.pallas_call(
        functools.partial(_conv_kernel, tt=tt),
        grid=(bsz, seq // tt),
        in_specs=[
            pl.BlockSpec((1, tt, cw), lambda bb, i: (bb, i, 0)),
            pl.BlockSpec((1, CONV_HALO, cw), lambda bb, i: (bb, jnp.maximum(i * hb - 1, 0), 0)),
            pl.BlockSpec(w_pad.shape, const),
            pl.BlockSpec(b.shape, const),
            pl.BlockSpec(lg.shape, const),
            pl.BlockSpec(lb.shape, const),
            pl.BlockSpec(og.shape, const),
        ],
        out_specs=pl.BlockSpec((1, tt, cw), lambda bb, i: (bb, i, 0)),
        out_shape=jax.ShapeDtypeStruct((bsz, seq, cw), BF16),
        scratch_shapes=[pltpu.VMEM((SUBLANES, CONV_HALO + tt, cw), F32)],
        compiler_params=_cparams(("arbitrary", "arbitrary")),
        name="conv",
    )(hg3, hg3, w_pad, b, lg, lb, og)


def _attn_kernel(q_ref, qi_ref, wit_ref, k_ref, vt_ref, kk_ref, og_ref, o_ref,
                 key_ref, k16_ref, bias_ref, m_ref, a_ref, acc_ref, lm_ref, p_ref, *, tq, n_sel, seq):
    i = pl.program_id(1)
    nk = i + 1
    krow = lax.broadcasted_iota(I32, (tq, tq), 0)
    qcol = lax.broadcasted_iota(I32, (tq, tq), 1) + i * tq

    def chunk_rows(c):
        return pl.ds(pl.multiple_of(c * tq, tq), tq)

    wit = wit_ref[...]

    def score_chunk(c):
        kic = kk_ref[0, chunk_rows(c), :]
        s = jnp.zeros((tq, tq), F32)
        for h in range(IDX_HEADS):
            qh = qi_ref[0, :, h * LANES:(h + 1) * LANES]
            s = s + wit[h:h + 1, :] * jnp.maximum(_nt_dot(kic, qh), 0.0)
        s = jnp.where(krow + c * tq <= qcol, s, -jnp.inf)
        bits = pltpu.bitcast(s, I32)
        key = jnp.where(bits < 0, bits ^ jnp.int32(0x7FFFFFFF), bits)
        key_ref[c] = jnp.where(s == 0.0, 0, key)

    def score_pair(cp, carry):
        score_chunk(2 * cp)
        score_chunk(2 * cp + 1)
        return carry

    lax.fori_loop(0, nk // 2, score_pair, 0)

    @pl.when(nk % 2 == 1)
    def _():
        score_chunk(nk - 1)

    def count(pred_fn):
        def body(c, acc):
            hit = jnp.where(pred_fn(c, key_ref[c]), 1, 0)
            return acc + jnp.sum(hit.reshape(tq // SUBLANES, SUBLANES, tq), axis=0)
        acc = lax.fori_loop(0, nk, body, jnp.zeros((SUBLANES, tq), I32))
        return jnp.sum(acc, axis=0, keepdims=True)

    def count_ge(thr):
        return count(lambda c, key: key >= thr)

    grp = tq // PACK16
    one_h, zero_h = jnp.ones((), BF16), jnp.zeros((), BF16)

    def count16(thr):
        thr16 = jnp.broadcast_to(thr, (PACK16, tq)).astype(jnp.int16)

        def body(cp, acc):
            for c in (2 * cp, 2 * cp + 1):
                hit = jnp.where(k16_ref[c].reshape(grp, PACK16, tq) >= thr16[None], one_h, zero_h)
                parts = [hit[r] for r in range(grp)]
                while len(parts) > 1:
                    parts = [parts[r] + parts[r + 1] for r in range(0, len(parts), 2)]
                acc = acc + parts[0].astype(F32)
            return acc

        acc = lax.fori_loop(0, (nk + 1) // 2, body, jnp.zeros((PACK16, tq), F32))
        return jnp.sum(acc, axis=0, keepdims=True).astype(I32)

    def search16():
        def step(bi, cand):
            trial = cand + jnp.left_shift(jnp.int32(1), 15 - bi)
            return jnp.where(count16(trial) >= n_sel, trial, cand)
        return lax.fori_loop(0, 16, step, jnp.full((1, tq), INT16_MIN, I32))

    def store_upper(c, carry):
        k16_ref[c] = lax.shift_right_arithmetic(key_ref[c], 16).astype(jnp.int16)
        return carry

    lax.fori_loop(0, nk, store_upper, 0)

    @pl.when(nk % 2 == 1)
    def _():
        k16_ref[nk] = jnp.full((tq, tq), INT16_MIN, jnp.int16)

    tau_hi = search16()

    def store_lower(c, carry):
        key = key_ref[c]
        hi = lax.shift_right_arithmetic(key, 16)
        lo = (key & 0xFFFF) + INT16_MIN
        k16_ref[c] = jnp.where(hi == tau_hi, lo, jnp.where(hi > tau_hi, -INT16_MIN - 1, INT16_MIN)
                               ).astype(jnp.int16)
        return carry

    lax.fori_loop(0, nk, store_lower, 0)
    tau = jnp.left_shift(tau_hi, 16) | (search16() - INT16_MIN)
    n_ge = count_ge(tau)
    need = n_sel - count_ge(tau + 1)

    def tie_cut():
        def step(bi, jm):
            trial = jm + jnp.left_shift(jnp.int32(1), (seq.bit_length() - 2) - bi)
            f = count(lambda c, key: jnp.where(key == tau, krow + c * tq, seq) < trial)
            return jnp.where(f < need, trial, jm)
        return lax.fori_loop(0, seq.bit_length() - 1, step, jnp.zeros((1, tq), I32))

    has_tie = jnp.max(n_ge) > n_sel
    jm = lax.cond(has_tie, tie_cut, lambda: jnp.full((1, tq), seq, I32))

    def bias_chunk(c, carry):
        key = key_ref[c]
        kidx = krow + c * tq
        tied = jnp.where(kidx <= jm, 0.0, NEG_BIG)
        above = jnp.where(key > tau, 0.0, NEG_BIG)
        bias_ref[c] = jnp.where(kidx <= qcol, jnp.where(key == tau, tied, above), NEG_BIG)
        return carry

    lax.fori_loop(0, nk, bias_chunk, 0)

    m_ref[...] = jnp.full(m_ref.shape, NEG_BIG, F32)
    acc_ref[...] = jnp.zeros(acc_ref.shape, F32)
    ones_rows = jnp.ones((PACK16, tq), BF16)

    def logits(c, slot):
        rows = chunk_rows(c)
        for hd in range(N_HEADS):
            p = hd // 2
            kp = k_ref[0, rows, p * LANES:(p + 1) * LANES]
            qh = q_ref[0, :, hd * LANES:(hd + 1) * LANES]
            lm_ref[slot, hd] = _nt_dot(kp, qh) + bias_ref[c]

    def softmax_pv(c, slot):
        for hd in range(N_HEADS):
            m_old = m_ref[hd]
            m_new = jnp.maximum(m_old, jnp.max(lm_ref[slot, hd], axis=0, keepdims=True))
            m_ref[hd] = m_new
            a_ref[hd] = jnp.exp2(m_old - m_new)
            p_ref[hd] = jnp.exp2(lm_ref[slot, hd] - m_new).astype(BF16)
        rows = chunk_rows(c)
        for hd in range(N_HEADS):
            vth = jnp.concatenate([vt_ref[hd * HEAD_DIM:(hd + 1) * HEAD_DIM, rows], ones_rows], axis=0)
            acc_ref[hd] = acc_ref[hd] * a_ref[hd] + _dot(vth, p_ref[hd])

    def attn_pair(cp, carry):
        c0 = 2 * cp
        logits(c0 + 1, 1)
        softmax_pv(c0, 0)
        logits(jnp.minimum(c0 + 2, last_c), 0)
        softmax_pv(c0 + 1, 1)
        return carry

    @pl.when(nk % 2 == 1)
    def _():
        bias_ref[nk] = jnp.full((tq, tq), NEG_BIG, F32)

    last_c = nk - 1 + nk % 2
    logits(0, 0)
    lax.fori_loop(0, (nk + 1) // 2, attn_pair, 0)

    ot = jnp.concatenate([acc_ref[hd, :HEAD_DIM] / acc_ref[hd, HEAD_DIM:HEAD_DIM + 1]
                          for hd in range(N_HEADS)], axis=0)
    ms = jnp.mean(ot * ot, axis=0, keepdims=True)
    ot = ot * lax.rsqrt(ms + NORM_EPS)
    o_ref[0] = (jnp.transpose(ot) * og_ref[...]).astype(BF16)


def _attention(q3, k3, vt, qi3, kk3, wit, og, tq):
    bsz, seq, w = k3.shape
    n_sel = min(TOPK_KEYS, seq // 4)
    nc = seq // tq
    assert nc % 2 == 0, "the count and softmax loops walk key chunks in pairs"
    qblk = lambda bb, i: (bb, i, 0)
    full = lambda bb, i: (bb, 0, 0)
    return pl.pallas_call(
        functools.partial(_attn_kernel, tq=tq, n_sel=n_sel, seq=seq),
        grid=(bsz, nc),
        in_specs=[
            pl.BlockSpec((1, tq, 2 * w), qblk),
            pl.BlockSpec((1, tq, 2 * w), qblk),
            pl.BlockSpec((SUBLANES, tq), lambda bb, i: (0, bb * nc + i)),
            pl.BlockSpec((1, seq, w), full),
            pl.BlockSpec((w, seq), lambda bb, i: (0, bb)),
            pl.BlockSpec((1, seq, LANES), full),
            pl.BlockSpec(og.shape, lambda bb, i: (0, 0)),
        ],
        out_specs=pl.BlockSpec((1, tq, w), qblk),
        out_shape=jax.ShapeDtypeStruct((bsz, seq, w), BF16),
        scratch_shapes=[
            pltpu.VMEM((nc, tq, tq), I32),
            pltpu.VMEM((nc, tq, tq), jnp.int16),
            pltpu.VMEM((nc, tq, tq), F32),
            pltpu.VMEM((N_HEADS, 1, tq), F32),
            pltpu.VMEM((N_HEADS, 1, tq), F32),
            pltpu.VMEM((N_HEADS, HEAD_DIM + PACK16, tq), F32),
            pltpu.VMEM((2, N_HEADS, tq, tq), F32),
            pltpu.VMEM((N_HEADS, tq, tq), BF16),
        ],
        compiler_params=_cparams(("arbitrary", "arbitrary")),
        name="dsa_attention",
    )(q3, qi3, wit, k3, vt, kk3, og)


def _outproj_kernel(an_ref, cn_ref, x_ref, g1_ref, sh_ref, sc_ref, woa_ref, woc_ref, wq_ref,
                    x1_ref, h2_ref, pq_ref):
    y = _dot(an_ref[...], woa_ref[...]) + _dot(cn_ref[...], woc_ref[...])
    x1 = x_ref[...] + g1_ref[0] * y
    x1_ref[...] = x1
    ms = jnp.mean(x1 * x1, axis=-1, keepdims=True)
    h2 = (x1 * lax.rsqrt(ms + NORM_EPS) * (1.0 + sc_ref[0]) + sh_ref[0]).astype(BF16)
    h2_ref[...] = h2
    pq_ref[...] = _dot(h2, wq_ref[...]).astype(BF16)


def _outproj(an, cn, x2, g1, sh2, sc2, woa, woc, wq, seq, tm):
    n, d = x2.shape
    per_b = seq // tm
    row = lambda i: (i, 0)
    const = lambda i: (0, 0)
    bat = lambda i: (i // per_b, 0, 0)
    return pl.pallas_call(
        _outproj_kernel,
        grid=(n // tm,),
        in_specs=[
            pl.BlockSpec((tm, an.shape[1]), row),
            pl.BlockSpec((tm, cn.shape[1]), row),
            pl.BlockSpec((tm, d), row),
            pl.BlockSpec((1, 1, d), bat),
            pl.BlockSpec((1, 1, d), bat),
            pl.BlockSpec((1, 1, d), bat),
            pl.BlockSpec(woa.shape, const),
            pl.BlockSpec(woc.shape, const),
            pl.BlockSpec(wq.shape, const),
        ],
        out_specs=[pl.BlockSpec((tm, d), row), pl.BlockSpec((tm, d), row),
                   pl.BlockSpec((tm, wq.shape[1]), row)],
        out_shape=[jax.ShapeDtypeStruct((n, d), F32), jax.ShapeDtypeStruct((n, d), BF16),
                   jax.ShapeDtypeStruct((n, wq.shape[1]), BF16)],
        compiler_params=_cparams(("arbitrary",)),
        name="outproj",
    )(an, cn, x2, g1, sh2, sc2, woa, woc, wq)


def _top_rows(s, rank=None, ids=None):
    return _top_rows_multi([(s, rank, ids)])[0]


def _top_rows_multi(problems):
    state = []
    for s, rank, ids in problems:
        if rank is None:
            rank = lax.broadcasted_iota(I32, s.shape, 0).astype(F32)
        state.append([s, rank, ids, [], []])
    for _ in range(PEER_TOPK):
        for st in state:
            s, rank, ids, vals, picks = st
            m = jnp.max(s, axis=0, keepdims=True)
            pos = jnp.min(jnp.where(s == m, rank, jnp.inf), axis=0, keepdims=True)
            hit = rank == pos
            vals.append(m)
            picks.append(pos if ids is None else jnp.max(jnp.where(hit, ids, -1.0), axis=0, keepdims=True))
            st[0] = jnp.where(hit, -jnp.inf, s)
    return [(jnp.concatenate(st[3], axis=0), jnp.concatenate(st[4], axis=0)) for st in state]


def _pair_candidates(v1, i1, v2, i2):
    tr = v1.shape[1]
    k = PEER_TOPK
    r8 = lax.broadcasted_iota(I32, (SUBLANES, tr), 0).astype(F32)
    rk = lax.broadcasted_iota(I32, (k, tr), 0).astype(F32)
    sums = [v1 + v2[0:1]]
    eids = [i1 * PEER_KEYS + i2[0:1]]
    order = [rk * k]
    for b in range(1, SUBLANES):
        ok = r8 < k // (b + 1)
        sums.append(jnp.where(ok, v1[0:SUBLANES] + v2[b:b + 1], -jnp.inf))
        eids.append(i1[0:SUBLANES] * PEER_KEYS + i2[b:b + 1])
        order.append(r8 * k + b)
    sums.append(v1[0:1] + v2[SUBLANES:k])
    eids.append(i1[0:1] * PEER_KEYS + i2[SUBLANES:k])
    order.append(r8 + SUBLANES)
    return jnp.concatenate(sums, axis=0), jnp.concatenate(eids, axis=0), jnp.concatenate(order, axis=0)


def _route_kernel(pq_ref, sk_ref, a_ref, b_ref, g_ref, e_scr, g_scr):
    h = pl.program_id(1)
    half = PEER_KEYS
    s1 = _nt_dot(sk_ref[0, 0], pq_ref[:, :half])
    s2 = _nt_dot(sk_ref[0, 1], pq_ref[:, half:])
    (v1, i1), (v2, i2) = _top_rows_multi([(s1, None, None), (s2, None, None)])
    cand, cid, order = _pair_candidates(v1, i1, v2, i2)
    ts, e = _top_rows(cand, order, cid)
    ex = jnp.exp(ts - ts[0:1])
    g = ex / jnp.sum(ex, axis=0, keepdims=True)
    rows = pl.ds(pl.multiple_of(h * PEER_TOPK, PEER_TOPK), PEER_TOPK)
    e_scr[rows, :] = e
    g_scr[rows, :] = g

    @pl.when(h == PEER_HEADS - 1)
    def _():
        e_all = e_scr[...]
        first = jnp.floor(e_all * (1.0 / PEER_KEYS))
        a_ref[...] = jnp.transpose(first)
        b_ref[...] = jnp.transpose(e_all - first * PEER_KEYS)
        g_ref[...] = jnp.transpose(g_scr[...])


def _route(pq, sk, tr):
    n = pq.shape[0]
    qd = 2 * PEER_KEYS
    out = jax.ShapeDtypeStruct((n, PEER_PAIRS), F32)
    oblk = pl.BlockSpec((tr, PEER_PAIRS), lambda i, h: (i, 0))
    return pl.pallas_call(
        _route_kernel,
        grid=(n // tr, PEER_HEADS),
        in_specs=[
            pl.BlockSpec((tr, qd), lambda i, h: (i, h)),
            pl.BlockSpec((1, 2, PEER_KEYS, PEER_KEYS), lambda i, h: (h, 0, 0, 0)),
        ],
        out_specs=[oblk, oblk, oblk],
        out_shape=[out, out, out],
        scratch_shapes=[pltpu.VMEM((PEER_PAIRS, tr), F32), pltpu.VMEM((PEER_PAIRS, tr), F32)],
        compiler_params=_cparams(("arbitrary", "arbitrary")),
        name="peer_route",
    )(pq, sk)


C_HALF = PEER_KEYS // 2
C_PITCH = C_HALF + SUBLANES
C_UNROLL = 16
HI16 = -65536


def _gelu(x):
    return 0.5 * x * (1.0 + lax.erf(x * (2.0 ** -0.5)))


def _bf16_bits(x):
    return pltpu.bitcast(x.astype(BF16).astype(F32), I32)


def _peer_kernel(h2_ref, a_ref, b_ref, g_ref, ulo_ref, uhi_ref, vlo_ref, vhi_ref, x1_ref, g2_ref, o_ref,
                 c_scr, acc_ref, *, tm, ec):
    j = pl.program_id(1)
    nj = pl.num_programs(1)
    k1 = ec // 2 // PEER_KEYS

    @pl.when(j == 0)
    def _():
        srow = lax.broadcasted_iota(I32, (PEER_KEYS, PEER_PAIRS), 0).astype(F32)

        def tokens(tb, carry):
            for u in range(C_UNROLL):
                t = tb * C_UNROLL + u
                a_row = a_ref[pl.ds(t, 1), :]
                b_row = b_ref[pl.ds(t, 1), :]
                g_row = g_ref[pl.ds(t, 1), :]
                at = jnp.where(srow == a_row, g_row, 0.0).astype(BF16)
                bt = jnp.where(srow == b_row, 1.0, 0.0).astype(BF16)
                ct = _nt_dot(at, bt)
                word = (lax.shift_right_logical(_bf16_bits(ct[:C_HALF]), 16)
                        | (_bf16_bits(ct[C_HALF:]) & HI16))
                c_scr[pl.ds(pl.multiple_of(t * C_PITCH, SUBLANES), C_HALF), :] = word
            return carry

        lax.fori_loop(0, tm // C_UNROLL, tokens, 0)
        acc_ref[...] = jnp.zeros(acc_ref.shape, F32)

    h2 = h2_ref[...]
    act_lo = _gelu(_dot(h2, ulo_ref[...]))
    act_hi = _gelu(_dot(h2, uhi_ref[...]))
    p_lo, p_hi = [], []
    for ii in range(k1):
        word = c_scr[pl.ds(j * k1 + ii, tm, stride=C_PITCH), :]
        c_lo = pltpu.bitcast(lax.shift_left(word, 16), F32)
        c_hi = pltpu.bitcast(word & HI16, F32)
        cols = slice(ii * PEER_KEYS, (ii + 1) * PEER_KEYS)
        p_lo.append((act_lo[:, cols] * c_lo).astype(BF16))
        p_hi.append((act_hi[:, cols] * c_hi).astype(BF16))
    acc_ref[...] += (_dot(jnp.concatenate(p_lo, axis=1), vlo_ref[...])
                     + _dot(jnp.concatenate(p_hi, axis=1), vhi_ref[...]))

    @pl.when(j == nj - 1)
    def _():
        o_ref[...] = x1_ref[...] + g2_ref[0] * acc_ref[...]


def _peer(h2, a, b, g, ut, v, x1, g2, seq, tm, ec):
    n, d = x1.shape
    ne = v.shape[0]
    per_b = seq // tm
    eb = ec // 2
    nhalf = ne // 2 // eb
    row = lambda i, j: (i, 0)
    lo = lambda i, j: (j, 0)
    hi = lambda i, j: (j + nhalf, 0)
    return pl.pallas_call(
        functools.partial(_peer_kernel, tm=tm, ec=ec),
        grid=(n // tm, ne // ec),
        in_specs=[
            pl.BlockSpec((tm, d), row),
            pl.BlockSpec((tm, PEER_PAIRS), row),
            pl.BlockSpec((tm, PEER_PAIRS), row),
            pl.BlockSpec((tm, PEER_PAIRS), row),
            pl.BlockSpec((d, eb), lambda i, j: (0, j)),
            pl.BlockSpec((d, eb), lambda i, j: (0, j + nhalf)),
            pl.BlockSpec((eb, d), lo),
            pl.BlockSpec((eb, d), hi),
            pl.BlockSpec((tm, d), row),
            pl.BlockSpec((1, 1, d), lambda i, j: (i // per_b, 0, 0)),
        ],
        out_specs=pl.BlockSpec((tm, d), row),
        out_shape=jax.ShapeDtypeStruct((n, d), F32),
        scratch_shapes=[pltpu.VMEM((tm * C_PITCH, PEER_KEYS), I32), pltpu.VMEM((tm, d), F32)],
        compiler_params=_cparams(("arbitrary", "arbitrary")),
        name="peer_experts",
    )(h2, a, b, g, ut, ut, v, v, x1, g2)


def _rope_lane_tables(positions):
    half = ROPE_DIM // 2
    inv_freq = jnp.power(jnp.float32(ROPE_THETA), -jnp.arange(0, ROPE_DIM, 2, dtype=F32) / ROPE_DIM)
    ang = positions.astype(F32).reshape(-1, 1) * inv_freq
    cos, sin = jnp.cos(ang), jnp.sin(ang)
    j = jnp.arange(LANES) % HEAD_DIM
    cos_l, sin_l = cos[:, j % half], sin[:, j % half]
    rc = jnp.where(j < ROPE_DIM, cos_l, 1.0)
    rs1 = jnp.where(j < half, -sin_l, 0.0)
    rs2 = jnp.where((j >= half) & (j < ROPE_DIM), sin_l, 0.0)
    return rc, rs1, rs2


def _tile_size(total, want):
    t = min(want, total)
    assert total % t == 0
    return t


def kernel(x, c, positions, w_ada, b_ada, w_in, q_norm_g, k_norm_g, idx_k_ln_g, idx_k_ln_b,
           conv_dw, conv_b, conv_ln_g, conv_ln_b, out_g_attn, out_g_conv, w_out, peer_wq,
           peer_sub_keys, peer_u, peer_v):
    bsz, seq, d = x.shape
    depth = w_ada.shape[0]
    n = bsz * seq
    w = ATTN_WIDTH
    cw = d - w
    assert bsz <= SUBLANES and cw % LANES == 0

    tm = _tile_size(seq, 512)
    tq = _tile_size(seq, 256)
    tt = _tile_size(seq, 512)
    tr = _tile_size(seq, 256)
    tp = _tile_size(seq, 512)
    ec = 2048

    c_pad = jnp.zeros((SUBLANES, d), F32).at[:bsz].set(c)
    mod = _ada(c_pad, w_ada, b_ada)[:, :bsz]
    rc, rs1, rs2 = _rope_lane_tables(positions)
    hd_id = jnp.arange(w) // HEAD_DIM
    gsum = (hd_id[:, None] == hd_id[None, :]).astype(BF16)

    o_q, o_k, o_v, o_qi, o_ki, o_wi = 0, w, 2 * w, 3 * w, 4 * w, 4 * w + IDX_DIM
    o_cu = o_wi + IDX_HEADS

    x2 = x.reshape(n, d)
    for l in range(depth):
        sh1, sc1, g1, sh2, sc2, g2 = [mod[l, :, i * d:(i + 1) * d].reshape(bsz, 1, d) for i in range(6)]
        wl = w_in[l]
        wa = wl[:, o_q:o_ki].astype(BF16)
        wki = wl[:, o_ki:o_wi]
        wk = jnp.concatenate([wki, wki], axis=1).astype(BF16)
        ww = jnp.pad(wl[:, o_wi:o_cu], ((0, 0), (0, LANES - IDX_HEADS))).astype(BF16)
        wc = wl[:, o_cu:].astype(BF16)
        tile_h = lambda g_: jnp.tile(g_, N_HEADS).reshape(1, w)
        dup = lambda g_: jnp.concatenate([g_, g_]).reshape(1, LANES)
        q, k, vt, qi, kk, wit, hg = _inproj(
            x2, sh1, sc1, wa, wk, ww, wc, tile_h(q_norm_g[l]), tile_h(k_norm_g[l]),
            dup(idx_k_ln_g[l]), dup(idx_k_ln_b[l]), rc, rs1, rs2, gsum, seq, tm)

        r3 = lambda t: t.reshape(bsz, seq, t.shape[-1])
        an = _attention(r3(q), r3(k), vt, r3(qi), r3(kk), wit, out_g_attn[l].reshape(1, w), tq)
        w_pad = jnp.pad(conv_dw[l], ((0, CONV_HALO - CONV_KERNEL), (0, 0)))
        cn = _conv(r3(hg), w_pad, conv_b[l].reshape(1, cw), conv_ln_g[l].reshape(1, cw),
                   conv_ln_b[l].reshape(1, cw), out_g_conv[l].reshape(1, cw), tt)

        wo = w_out[l].astype(BF16)
        x1, h2, pq = _outproj(an.reshape(n, w), cn.reshape(n, cw), x2, g1, sh2, sc2,
                              wo[:w], wo[w:], peer_wq[l].astype(BF16), seq, tm)
        a, b, g = _route(pq, peer_sub_keys[l].astype(BF16), tr)
        x2 = _peer(h2, a, b, g, jnp.transpose(peer_u[l]).astype(BF16), peer_v[l].astype(BF16), x1, g2,
                   seq, tp, ec)
    return x2.reshape(bsz, seq, d)
```

```python
import functools

import jax
import jax.numpy as jnp
from jax import lax
from jax.experimental import pallas as pl
from jax.experimental.pallas import tpu as pltpu

F32 = jnp.float32
BF16 = jnp.bfloat16
I32 = jnp.int32

N_HEADS = 8
HEAD_DIM = 64
ATTN_WIDTH = N_HEADS * HEAD_DIM
CONV_KERNEL = 31
IDX_HEADS = 8
IDX_DIM = 64
TOPK_KEYS = 256
ROPE_THETA = 500000.0
ROPE_DIM = HEAD_DIM // 4
PEER_HEADS = 8
PEER_KEYS = 128
PEER_TOPK = 16
PEER_PAIRS = PEER_HEADS * PEER_TOPK
NORM_EPS = 1e-6
LOG2_E = 1.4426950408889634
Q_SCALE = HEAD_DIM ** -0.5 * LOG2_E

LANES = 128
SUBLANES = 8
VMEM_LIMIT = 56 * 1024 * 1024

PACK16 = 2 * SUBLANES

NEG_BIG = -1e30
INT_MIN = -2 ** 31
INT16_MIN = -2 ** 15


def _cparams(sem):
    return pltpu.CompilerParams(dimension_semantics=sem, vmem_limit_bytes=VMEM_LIMIT)


def _nt_dot(a, b):
    return lax.dot_general(a, b, (((1,), (1,)), ((), ())), preferred_element_type=F32)


def _dot(a, b):
    return jnp.dot(a, b, preferred_element_type=F32)


def _ada_kernel(c_ref, w_ref, b_ref, o_ref):
    c = c_ref[...]
    ca = c * jax.nn.sigmoid(c)
    o_ref[0] = _dot(ca, w_ref[0]) + b_ref[0]


def _ada(c_pad, w_ada, b_ada):
    depth, d, n6 = w_ada.shape
    tn = n6 // 4
    return pl.pallas_call(
        _ada_kernel,
        grid=(depth, n6 // tn),
        in_specs=[
            pl.BlockSpec((SUBLANES, d), lambda l, j: (0, 0)),
            pl.BlockSpec((1, d, tn), lambda l, j: (l, 0, j)),
            pl.BlockSpec((1, 1, tn), lambda l, j: (l, 0, j)),
        ],
        out_specs=pl.BlockSpec((1, SUBLANES, tn), lambda l, j: (l, 0, j)),
        out_shape=jax.ShapeDtypeStruct((depth, SUBLANES, n6), F32),
        compiler_params=_cparams(("arbitrary", "arbitrary")),
        name="adaln",
    )(c_pad, w_ada, b_ada.reshape(depth, 1, n6))


def _rope(x, c, s1, s2):
    w = x.shape[-1]
    half = ROPE_DIM // 2
    return x * c + pltpu.roll(x, w - half, 1) * s1 + pltpu.roll(x, half, 1) * s2


def _split_hi_lo(x):
    hi = x.astype(BF16)
    lo = (x - hi.astype(F32)).astype(BF16)
    return hi, lo


def _inproj_kernel(x_ref, sh_ref, sc_ref, wa_ref, wk_ref, ww_ref, wc_ref, gq_ref, gk_ref,
                   lng_ref, lnb_ref, rc_ref, rs1_ref, rs2_ref, gsum_ref,
                   q_ref, k_ref, vt_ref, qi_ref, kk_ref, wit_ref, hg_ref):
    x = x_ref[...]
    ms = jnp.mean(x * x, axis=-1, keepdims=True)
    h = x * lax.rsqrt(ms + NORM_EPS) * (1.0 + sc_ref[0]) + sh_ref[0]
    hb = h.astype(BF16)

    rc1, rs11, rs21 = rc_ref[...], rs1_ref[...], rs2_ref[...]
    reps = ATTN_WIDTH // LANES
    rc = jnp.concatenate([rc1] * reps, axis=1)
    rs1 = jnp.concatenate([rs11] * reps, axis=1)
    rs2 = jnp.concatenate([rs21] * reps, axis=1)
    gsum = gsum_ref[...]

    def head_norm_rope(t, gain):
        hi, lo = _split_hi_lo(t * t)
        ssq = _dot(hi, gsum) + _dot(lo, gsum)
        tn = t * lax.rsqrt(ssq * (1.0 / HEAD_DIM) + NORM_EPS) * gain
        return _rope(tn, rc, rs1, rs2)

    lo_half = lax.broadcasted_iota(I32, (x.shape[0], LANES), 1) < HEAD_DIM

    def one_head_per_group(t):
        parts = []
        for p in range(reps):
            g = t[:, p * LANES:(p + 1) * LANES]
            parts += [jnp.where(lo_half, g, 0.0), jnp.where(lo_half, 0.0, g)]
        return jnp.concatenate(parts, axis=1).astype(BF16)

    w = ATTN_WIDTH
    pa = _dot(hb, wa_ref[...])
    q_ref[...] = one_head_per_group(head_norm_rope(pa[:, 0:w], gq_ref[...]) * Q_SCALE)
    k_ref[...] = head_norm_rope(pa[:, w:2 * w], gk_ref[...]).astype(BF16)
    vt_ref[...] = jnp.transpose(pa[:, 2 * w:3 * w]).astype(BF16)
    qi_ref[...] = one_head_per_group(_rope(pa[:, 3 * w:4 * w], rc, rs1, rs2) * (IDX_DIM ** -0.5))

    pk = _dot(hb, wk_ref[...])
    mu = jnp.mean(pk, axis=-1, keepdims=True)
    var = jnp.mean(jnp.square(pk - mu), axis=-1, keepdims=True)
    kn = (pk - mu) * lax.rsqrt(var + NORM_EPS) * lng_ref[...] + lnb_ref[...]
    kk_ref[...] = _rope(kn, rc1, rs11, rs21).astype(BF16)

    wit = jnp.transpose(_dot(hb, ww_ref[...]) * (IDX_HEADS ** -0.5))
    wit_ref[...] = wit[:IDX_HEADS]

    pc = _dot(hb, wc_ref[...])
    cw = pc.shape[1] // 2
    hg_ref[...] = pc[:, :cw] * jax.nn.sigmoid(pc[:, cw:])


def _inproj(x2, sh, sc, wa, wk, ww, wc, gq, gk, lng, lnb, rc, rs1, rs2, gsum, seq, tm):
    n, d = x2.shape
    per_b = seq // tm
    row = lambda i: (i, 0)
    const = lambda i: (0, 0)
    bat = lambda i: (i // per_b, 0, 0)
    cw = wc.shape[1] // 2
    outs = [
        jax.ShapeDtypeStruct((n, 2 * ATTN_WIDTH), BF16),
        jax.ShapeDtypeStruct((n, ATTN_WIDTH), BF16),
        jax.ShapeDtypeStruct((ATTN_WIDTH, n), BF16),
        jax.ShapeDtypeStruct((n, 2 * ATTN_WIDTH), BF16),
        jax.ShapeDtypeStruct((n, LANES), BF16),
        jax.ShapeDtypeStruct((IDX_HEADS, n), F32),
        jax.ShapeDtypeStruct((n, cw), F32),
    ]
    col = lambda i: (0, i)
    token_minor = (False, False, True, False, False, True, False)
    out_specs = [pl.BlockSpec((o.shape[0], tm), col) if tmin else pl.BlockSpec((tm, o.shape[1]), row)
                 for o, tmin in zip(outs, token_minor)]
    return pl.pallas_call(
        _inproj_kernel,
        grid=(n // tm,),
        in_specs=[
            pl.BlockSpec((tm, d), row),
            pl.BlockSpec((1, 1, d), bat),
            pl.BlockSpec((1, 1, d), bat),
            pl.BlockSpec(wa.shape, const),
            pl.BlockSpec(wk.shape, const),
            pl.BlockSpec(ww.shape, const),
            pl.BlockSpec(wc.shape, const),
            pl.BlockSpec(gq.shape, const),
            pl.BlockSpec(gk.shape, const),
            pl.BlockSpec(lng.shape, const),
            pl.BlockSpec(lnb.shape, const),
            pl.BlockSpec((tm, LANES), row),
            pl.BlockSpec((tm, LANES), row),
            pl.BlockSpec((tm, LANES), row),
            pl.BlockSpec(gsum.shape, const),
        ],
        out_specs=out_specs,
        out_shape=outs,
        compiler_params=_cparams(("arbitrary",)),
        name="inproj",
    )(x2, sh, sc, wa, wk, ww, wc, gq, gk, lng, lnb, rc, rs1, rs2, gsum)


CONV_HALO = 32
CONV_ROWS = 64


def _conv_kernel(cur_ref, halo_ref, w_ref, b_ref, lg_ref, lb_ref, og_ref, o_ref, ext_ref, *, tt):
    i = pl.program_id(1)
    ext_ref[0, 0:CONV_HALO, :] = jnp.where(i > 0, halo_ref[0], 0.0)
    ext_ref[0, CONV_HALO:CONV_HALO + tt, :] = cur_ref[0]
    rows = CONV_HALO + tt - SUBLANES
    for s in range(1, SUBLANES):
        ext_ref[s, 0:rows, :] = ext_ref[0, s:s + rows, :]
    lead = CONV_HALO - (CONV_KERNEL - 1)
    for r0 in range(0, tt, CONV_ROWS):
        acc = jnp.zeros((CONV_ROWS, ext_ref.shape[2]), F32)
        for j in range(CONV_KERNEL):
            s = (lead + j) % SUBLANES
            base = r0 + lead + j - s
            acc = acc + ext_ref[s, base:base + CONV_ROWS, :] * w_ref[j:j + 1, :]
        hcv = acc + b_ref[...]
        mu = jnp.mean(hcv, axis=-1, keepdims=True)
        var = jnp.mean(jnp.square(hcv - mu), axis=-1, keepdims=True)
        y = (hcv - mu) * lax.rsqrt(var + NORM_EPS) * lg_ref[...] + lb_ref[...]
        y = y * jax.nn.sigmoid(y)
        ms = jnp.mean(y * y, axis=-1, keepdims=True)
        o_ref[0, r0:r0 + CONV_ROWS, :] = (y * lax.rsqrt(ms + NORM_EPS) * og_ref[...]).astype(BF16)


def _conv(hg3, w_pad, b, lg, lb, og, tt):
    bsz, seq, cw = hg3.shape
    hb = tt // CONV_HALO
    const = lambda bb, i: (0, 0)
    return pl.pallas_call(
        functools.partial(_conv_kernel, tt=tt),
        grid=(bsz, seq // tt),
        in_specs=[
            pl.BlockSpec((1, tt, cw), lambda bb, i: (bb, i, 0)),
            pl.BlockSpec((1, CONV_HALO, cw), lambda bb, i: (bb, jnp.maximum(i * hb - 1, 0), 0)),
            pl.BlockSpec(w_pad.shape, const),
            pl.BlockSpec(b.shape, const),
            pl.BlockSpec(lg.shape, const),
            pl.BlockSpec(lb.shape, const),
            pl.BlockSpec(og.shape, const),
        ],
        out_specs=pl.BlockSpec((1, tt, cw), lambda bb, i: (bb, i, 0)),
        out_shape=jax.ShapeDtypeStruct((bsz, seq, cw), BF16),
        scratch_shapes=[pltpu.VMEM((SUBLANES, CONV_HALO + tt, cw), F32)],
        compiler_params=_cparams(("arbitrary", "arbitrary")),
        name="conv",
    )(hg3, hg3, w_pad, b, lg, lb, og)


def _attn_kernel(q_ref, qi_ref, wit_ref, k_ref, vt_ref, kk_ref, og_ref, o_ref,
                 key_ref, k16_ref, bias_ref, m_ref, a_ref, acc_ref, lm_ref, p_ref, qt_ref, *, tq, n_sel, seq):
    i = pl.program_id(1)
    nk = i + 1
    krow = lax.broadcasted_iota(I32, (tq, tq), 0)
    qcol = lax.broadcasted_iota(I32, (tq, tq), 1) + i * tq

    def chunk_rows(c):
        return pl.ds(pl.multiple_of(c * tq, tq), tq)

    wit = wit_ref[...]

    def score_chunk(c):
        kic = kk_ref[0, chunk_rows(c), :]
        s = jnp.zeros((tq, tq), F32)
        for h in range(IDX_HEADS):
            qh = qi_ref[0, :, h * LANES:(h + 1) * LANES]
            s = s + wit[h:h + 1, :] * jnp.maximum(_nt_dot(kic, qh), 0.0)
        s = jnp.where(krow + c * tq <= qcol, s, -jnp.inf)
        bits = pltpu.bitcast(s, I32)
        key = jnp.where(bits < 0, bits ^ jnp.int32(0x7FFFFFFF), bits)
        key_ref[c] = jnp.where(s == 0.0, 0, key)

    def score_pair(cp, carry):
        score_chunk(2 * cp)
        score_chunk(2 * cp + 1)
        return carry

    lax.fori_loop(0, nk // 2, score_pair, 0)

    @pl.when(nk % 2 == 1)
    def _():
        score_chunk(nk - 1)

    def count(pred_fn):
        def body(c, acc):
            hit = jnp.where(pred_fn(c, key_ref[c]), 1, 0)
            return acc + jnp.sum(hit.reshape(tq // SUBLANES, SUBLANES, tq), axis=0)
        acc = lax.fori_loop(0, nk, body, jnp.zeros((SUBLANES, tq), I32))
        return jnp.sum(acc, axis=0, keepdims=True)

    def count_ge(thr):
        return count(lambda c, key: key >= thr)

    grp = tq // PACK16
    one_h, zero_h = jnp.ones((), BF16), jnp.zeros((), BF16)

    def count16(thr):
        thr16 = jnp.broadcast_to(thr, (PACK16, tq)).astype(jnp.int16)

        def body(cp, acc):
            for c in (2 * cp, 2 * cp + 1):
                hit = jnp.where(k16_ref[c].reshape(grp, PACK16, tq) >= thr16[None], one_h, zero_h)
                parts = [hit[r] for r in range(grp)]
                while len(parts) > 1:
                    parts = [parts[r] + parts[r + 1] for r in range(0, len(parts), 2)]
                acc = acc + parts[0].astype(F32)
            return acc

        acc = lax.fori_loop(0, (nk + 1) // 2, body, jnp.zeros((PACK16, tq), F32))
        return jnp.sum(acc, axis=0, keepdims=True).astype(I32)

    def search16():
        def step(bi, cand):
            trial = cand + jnp.left_shift(jnp.int32(1), 15 - bi)
            return jnp.where(count16(trial) >= n_sel, trial, cand)
        return lax.fori_loop(0, 16, step, jnp.full((1, tq), INT16_MIN, I32))

    def store_upper(c, carry):
        k16_ref[c] = lax.shift_right_arithmetic(key_ref[c], 16).astype(jnp.int16)
        return carry

    lax.fori_loop(0, nk, store_upper, 0)

    @pl.when(nk % 2 == 1)
    def _():
        k16_ref[nk] = jnp.full((tq, tq), INT16_MIN, jnp.int16)

    tau_hi = search16()

    def store_lower(c, carry):
        key = key_ref[c]
        hi = lax.shift_right_arithmetic(key, 16)
        lo = (key & 0xFFFF) + INT16_MIN
        k16_ref[c] = jnp.where(hi == tau_hi, lo, jnp.where(hi > tau_hi, -INT16_MIN - 1, INT16_MIN)
                               ).astype(jnp.int16)
        return carry

    lax.fori_loop(0, nk, store_lower, 0)
    tau = jnp.left_shift(tau_hi, 16) | (search16() - INT16_MIN)
    n_ge = count_ge(tau)
    need = n_sel - count_ge(tau + 1)

    def tie_cut():
        def step(bi, jm):
            trial = jm + jnp.left_shift(jnp.int32(1), (seq.bit_length() - 2) - bi)
            f = count(lambda c, key: jnp.where(key == tau, krow + c * tq, seq) < trial)
            return jnp.where(f < need, trial, jm)
        return lax.fori_loop(0, seq.bit_length() - 1, step, jnp.zeros((1, tq), I32))

    has_tie = jnp.max(n_ge) > n_sel
    jm = lax.cond(has_tie, tie_cut, lambda: jnp.full((1, tq), seq, I32))

    def bias_chunk(c, carry):
        key = key_ref[c]
        kidx = krow + c * tq
        tied = jnp.where(kidx <= jm, 0.0, NEG_BIG)
        above = jnp.where(key > tau, 0.0, NEG_BIG)
        bias_ref[c] = jnp.where(kidx <= qcol, jnp.where(key == tau, tied, above), NEG_BIG)
        return carry

    lax.fori_loop(0, nk, bias_chunk, 0)

    m_ref[...] = jnp.full(m_ref.shape, NEG_BIG, F32)
    acc_ref[...] = jnp.zeros(acc_ref.shape, F32)
    ones_rows = jnp.ones((PACK16, tq), BF16)

    for hd in range(N_HEADS):
        qh = q_ref[0, :, hd * LANES:(hd + 1) * LANES]
        qt_ref[hd] = jnp.transpose(qh.astype(F32)).astype(BF16)

    def logits(c, slot):
        rows = chunk_rows(c)
        for hd in range(N_HEADS):
            p = hd // 2
            kp = k_ref[0, rows, p * LANES:(p + 1) * LANES]
            lm_ref[slot, hd] = _dot(kp, qt_ref[hd]) + bias_ref[c]

    def softmax_pv(c, slot):
        for hd in range(N_HEADS):
            m_old = m_ref[hd]
            m_new = jnp.maximum(m_old, jnp.max(lm_ref[slot, hd], axis=0, keepdims=True))
            m_ref[hd] = m_new
            a_ref[hd] = jnp.exp2(m_old - m_new)
            p_ref[hd] = jnp.exp2(lm_ref[slot, hd] - m_new).astype(BF16)
        rows = chunk_rows(c)
        for hd in range(N_HEADS):
            vth = jnp.concatenate([vt_ref[hd * HEAD_DIM:(hd + 1) * HEAD_DIM, rows], ones_rows], axis=0)
            acc_ref[hd] = acc_ref[hd] * a_ref[hd] + _dot(vth, p_ref[hd])

    def attn_pair(cp, carry):
        c0 = 2 * cp
        logits(c0 + 1, 1)
        softmax_pv(c0, 0)
        logits(jnp.minimum(c0 + 2, last_c), 0)
        softmax_pv(c0 + 1, 1)
        return carry

    @pl.when(nk % 2 == 1)
    def _():
        bias_ref[nk] = jnp.full((tq, tq), NEG_BIG, F32)

    last_c = nk - 1 + nk % 2
    logits(0, 0)
    lax.fori_loop(0, (nk + 1) // 2, attn_pair, 0)

    ot = jnp.concatenate([acc_ref[hd, :HEAD_DIM] / acc_ref[hd, HEAD_DIM:HEAD_DIM + 1]
                          for hd in range(N_HEADS)], axis=0)
    ms = jnp.mean(ot * ot, axis=0, keepdims=True)
    ot = ot * lax.rsqrt(ms + NORM_EPS)
    o_ref[0] = (jnp.transpose(ot) * og_ref[...]).astype(BF16)


def _attention(q3, k3, vt, qi3, kk3, wit, og, tq):
    bsz, seq, w = k3.shape
    n_sel = min(TOPK_KEYS, seq // 4)
    nc = seq // tq
    assert nc % 2 == 0, "the count and softmax loops walk key chunks in pairs"
    qblk = lambda bb, i: (bb, i, 0)
    full = lambda bb, i: (bb, 0, 0)
    return pl.pallas_call(
        functools.partial(_attn_kernel, tq=tq, n_sel=n_sel, seq=seq),
        grid=(bsz, nc),
        in_specs=[
            pl.BlockSpec((1, tq, 2 * w), qblk),
            pl.BlockSpec((1, tq, 2 * w), qblk),
            pl.BlockSpec((SUBLANES, tq), lambda bb, i: (0, bb * nc + i)),
            pl.BlockSpec((1, seq, w), full),
            pl.BlockSpec((w, seq), lambda bb, i: (0, bb)),
            pl.BlockSpec((1, seq, LANES), full),
            pl.BlockSpec(og.shape, lambda bb, i: (0, 0)),
        ],
        out_specs=pl.BlockSpec((1, tq, w), qblk),
        out_shape=jax.ShapeDtypeStruct((bsz, seq, w), BF16),
        scratch_shapes=[
            pltpu.VMEM((nc, tq, tq), I32),
            pltpu.VMEM((nc, tq, tq), jnp.int16),
            pltpu.VMEM((nc, tq, tq), F32),
            pltpu.VMEM((N_HEADS, 1, tq), F32),
            pltpu.VMEM((N_HEADS, 1, tq), F32),
            pltpu.VMEM((N_HEADS, HEAD_DIM + PACK16, tq), F32),
            pltpu.VMEM((2, N_HEADS, tq, tq), F32),
            pltpu.VMEM((N_HEADS, tq, tq), BF16),
            pltpu.VMEM((N_HEADS, LANES, tq), BF16),
        ],
        compiler_params=_cparams(("arbitrary", "arbitrary")),
        name="dsa_attention",
    )(q3, qi3, wit, k3, vt, kk3, og)


def _outproj_kernel(an_ref, cn_ref, x_ref, g1_ref, sh_ref, sc_ref, woa_ref, woc_ref, wq_ref,
                    x1_ref, h2_ref, pq_ref):
    y = _dot(an_ref[...], woa_ref[...]) + _dot(cn_ref[...], woc_ref[...])
    x1 = x_ref[...] + g1_ref[0] * y
    x1_ref[...] = x1
    ms = jnp.mean(x1 * x1, axis=-1, keepdims=True)
    h2 = (x1 * lax.rsqrt(ms + NORM_EPS) * (1.0 + sc_ref[0]) + sh_ref[0]).astype(BF16)
    h2_ref[...] = h2
    pq_ref[...] = _dot(h2, wq_ref[...]).astype(BF16)


def _outproj(an, cn, x2, g1, sh2, sc2, woa, woc, wq, seq, tm):
    n, d = x2.shape
    per_b = seq // tm
    row = lambda i: (i, 0)
    const = lambda i: (0, 0)
    bat = lambda i: (i // per_b, 0, 0)
    return pl.pallas_call(
        _outproj_kernel,
        grid=(n // tm,),
        in_specs=[
            pl.BlockSpec((tm, an.shape[1]), row),
            pl.BlockSpec((tm, cn.shape[1]), row),
            pl.BlockSpec((tm, d), row),
            pl.BlockSpec((1, 1, d), bat),
            pl.BlockSpec((1, 1, d), bat),
            pl.BlockSpec((1, 1, d), bat),
            pl.BlockSpec(woa.shape, const),
            pl.BlockSpec(woc.shape, const),
            pl.BlockSpec(wq.shape, const),
        ],
        out_specs=[pl.BlockSpec((tm, d), row), pl.BlockSpec((tm, d), row),
                   pl.BlockSpec((tm, wq.shape[1]), row)],
        out_shape=[jax.ShapeDtypeStruct((n, d), F32), jax.ShapeDtypeStruct((n, d), BF16),
                   jax.ShapeDtypeStruct((n, wq.shape[1]), BF16)],
        compiler_params=_cparams(("arbitrary",)),
        name="outproj",
    )(an, cn, x2, g1, sh2, sc2, woa, woc, wq)


def _top_rows(s, rank=None, ids=None):
    return _top_rows_multi([(s, rank, ids)])[0]


def _top_rows_multi(problems):
    state = []
    for s, rank, ids in problems:
        if rank is None:
            rank = lax.broadcasted_iota(I32, s.shape, 0).astype(F32)
        state.append([s, rank, ids, [], []])
    for _ in range(PEER_TOPK):
        for st in state:
            s, rank, ids, vals, picks = st
            m = jnp.max(s, axis=0, keepdims=True)
            pos = jnp.min(jnp.where(s == m, rank, jnp.inf), axis=0, keepdims=True)
            hit = rank == pos
            vals.append(m)
            picks.append(pos if ids is None else jnp.max(jnp.where(hit, ids, -1.0), axis=0, keepdims=True))
            st[0] = jnp.where(hit, -jnp.inf, s)
    return [(jnp.concatenate(st[3], axis=0), jnp.concatenate(st[4], axis=0)) for st in state]


def _pair_candidates(v1, i1, v2, i2):
    tr = v1.shape[1]
    k = PEER_TOPK
    r8 = lax.broadcasted_iota(I32, (SUBLANES, tr), 0).astype(F32)
    rk = lax.broadcasted_iota(I32, (k, tr), 0).astype(F32)
    sums = [v1 + v2[0:1]]
    eids = [i1 * PEER_KEYS + i2[0:1]]
    order = [rk * k]
    for b in range(1, SUBLANES):
        ok = r8 < k // (b + 1)
        sums.append(jnp.where(ok, v1[0:SUBLANES] + v2[b:b + 1], -jnp.inf))
        eids.append(i1[0:SUBLANES] * PEER_KEYS + i2[b:b + 1])
        order.append(r8 * k + b)
    sums.append(v1[0:1] + v2[SUBLANES:k])
    eids.append(i1[0:1] * PEER_KEYS + i2[SUBLANES:k])
    order.append(r8 + SUBLANES)
    return jnp.concatenate(sums, axis=0), jnp.concatenate(eids, axis=0), jnp.concatenate(order, axis=0)


def _route_kernel(pq_ref, sk_ref, a_ref, b_ref, g_ref, e_scr, g_scr):
    h = pl.program_id(1)
    half = PEER_KEYS
    s1 = _nt_dot(sk_ref[0, 0], pq_ref[:, :half])
    s2 = _nt_dot(sk_ref[0, 1], pq_ref[:, half:])
    (v1, i1), (v2, i2) = _top_rows_multi([(s1, None, None), (s2, None, None)])
    cand, cid, order = _pair_candidates(v1, i1, v2, i2)
    ts, e = _top_rows(cand, order, cid)
    ex = jnp.exp(ts - ts[0:1])
    g = ex / jnp.sum(ex, axis=0, keepdims=True)
    rows = pl.ds(pl.multiple_of(h * PEER_TOPK, PEER_TOPK), PEER_TOPK)
    e_scr[rows, :] = e
    g_scr[rows, :] = g

    @pl.when(h == PEER_HEADS - 1)
    def _():
        e_all = e_scr[...]
        first = jnp.floor(e_all * (1.0 / PEER_KEYS))
        a_ref[...] = jnp.transpose(first)
        b_ref[...] = jnp.transpose(e_all - first * PEER_KEYS)
        g_ref[...] = jnp.transpose(g_scr[...])


def _route(pq, sk, tr):
    n = pq.shape[0]
    qd = 2 * PEER_KEYS
    out = jax.ShapeDtypeStruct((n, PEER_PAIRS), F32)
    oblk = pl.BlockSpec((tr, PEER_PAIRS), lambda i, h: (i, 0))
    return pl.pallas_call(
        _route_kernel,
        grid=(n // tr, PEER_HEADS),
        in_specs=[
            pl.BlockSpec((tr, qd), lambda i, h: (i, h)),
            pl.BlockSpec((1, 2, PEER_KEYS, PEER_KEYS), lambda i, h: (h, 0, 0, 0)),
        ],
        out_specs=[oblk, oblk, oblk],
        out_shape=[out, out, out],
        scratch_shapes=[pltpu.VMEM((PEER_PAIRS, tr), F32), pltpu.VMEM((PEER_PAIRS, tr), F32)],
        compiler_params=_cparams(("arbitrary", "arbitrary")),
        name="peer_route",
    )(pq, sk)


C_HALF = PEER_KEYS // 2
C_PITCH = C_HALF + SUBLANES
C_UNROLL = 16
HI16 = -65536


def _gelu(x):
    return 0.5 * x * (1.0 + lax.erf(x * (2.0 ** -0.5)))


def _bf16_bits(x):
    return pltpu.bitcast(x.astype(BF16).astype(F32), I32)


def _peer_kernel(h2_ref, a_ref, b_ref, g_ref, ulo_ref, uhi_ref, vlo_ref, vhi_ref, x1_ref, g2_ref, o_ref,
                 c_scr, acc_ref, *, tm, ec):
    j = pl.program_id(1)
    nj = pl.num_programs(1)
    k1 = ec // 2 // PEER_KEYS

    @pl.when(j == 0)
    def _():
        srow = lax.broadcasted_iota(I32, (PEER_KEYS, PEER_PAIRS), 0).astype(F32)

        def tokens(tb, carry):
            for u in range(C_UNROLL):
                t = tb * C_UNROLL + u
                a_row = a_ref[pl.ds(t, 1), :]
                b_row = b_ref[pl.ds(t, 1), :]
                g_row = g_ref[pl.ds(t, 1), :]
                at = jnp.where(srow == a_row, g_row, 0.0).astype(BF16)
                bt = jnp.where(srow == b_row, 1.0, 0.0).astype(BF16)
                ct = _nt_dot(at, bt)
                word = (lax.shift_right_logical(_bf16_bits(ct[:C_HALF]), 16)
                        | (_bf16_bits(ct[C_HALF:]) & HI16))
                c_scr[pl.ds(pl.multiple_of(t * C_PITCH, SUBLANES), C_HALF), :] = word
            return carry

        lax.fori_loop(0, tm // C_UNROLL, tokens, 0)
        acc_ref[...] = jnp.zeros(acc_ref.shape, F32)

    h2 = h2_ref[...]
    act_lo = _gelu(_dot(h2, ulo_ref[...]))
    act_hi = _gelu(_dot(h2, uhi_ref[...]))
    p_lo, p_hi = [], []
    for ii in range(k1):
        word = c_scr[pl.ds(j * k1 + ii, tm, stride=C_PITCH), :]
        c_lo = pltpu.bitcast(lax.shift_left(word, 16), F32)
        c_hi = pltpu.bitcast(word & HI16, F32)
        cols = slice(ii * PEER_KEYS, (ii + 1) * PEER_KEYS)
        p_lo.append((act_lo[:, cols] * c_lo).astype(BF16))
        p_hi.append((act_hi[:, cols] * c_hi).astype(BF16))
    acc_ref[...] += (_dot(jnp.concatenate(p_lo, axis=1), vlo_ref[...])
                     + _dot(jnp.concatenate(p_hi, axis=1), vhi_ref[...]))

    @pl.when(j == nj - 1)
    def _():
        o_ref[...] = x1_ref[...] + g2_ref[0] * acc_ref[...]


def _peer(h2, a, b, g, ut, v, x1, g2, seq, tm, ec):
    n, d = x1.shape
    ne = v.shape[0]
    per_b = seq // tm
    eb = ec // 2
    nhalf = ne // 2 // eb
    row = lambda i, j: (i, 0)
    lo = lambda i, j: (j, 0)
    hi = lambda i, j: (j + nhalf, 0)
    return pl.pallas_call(
        functools.partial(_peer_kernel, tm=tm, ec=ec),
        grid=(n // tm, ne // ec),
        in_specs=[
            pl.BlockSpec((tm, d), row),
            pl.BlockSpec((tm, PEER_PAIRS), row),
            pl.BlockSpec((tm, PEER_PAIRS), row),
            pl.BlockSpec((tm, PEER_PAIRS), row),
            pl.BlockSpec((d, eb), lambda i, j: (0, j)),
            pl.BlockSpec((d, eb), lambda i, j: (0, j + nhalf)),
            pl.BlockSpec((eb, d), lo),
            pl.BlockSpec((eb, d), hi),
            pl.BlockSpec((tm, d), row),
            pl.BlockSpec((1, 1, d), lambda i, j: (i // per_b, 0, 0)),
        ],
        out_specs=pl.BlockSpec((tm, d), row),
        out_shape=jax.ShapeDtypeStruct((n, d), F32),
        scratch_shapes=[pltpu.VMEM((tm * C_PITCH, PEER_KEYS), I32), pltpu.VMEM((tm, d), F32)],
        compiler_params=_cparams(("arbitrary", "arbitrary")),
        name="peer_experts",
    )(h2, a, b, g, ut, ut, v, v, x1, g2)


def _rope_lane_tables(positions):
    half = ROPE_DIM // 2
    inv_freq = jnp.power(jnp.float32(ROPE_THETA), -jnp.arange(0, ROPE_DIM, 2, dtype=F32) / ROPE_DIM)
    ang = positions.astype(F32).reshape(-1, 1) * inv_freq
    cos, sin = jnp.cos(ang), jnp.sin(ang)
    j = jnp.arange(LANES) % HEAD_DIM
    cos_l, sin_l = cos[:, j % half], sin[:, j % half]
    rc = jnp.where(j < ROPE_DIM, cos_l, 1.0)
    rs1 = jnp.where(j < half, -sin_l, 0.0)
    rs2 = jnp.where((j >= half) & (j < ROPE_DIM), sin_l, 0.0)
    return rc, rs1, rs2


def _tile_size(total, want):
    t = min(want, total)
    assert total % t == 0
    return t


def kernel(x, c, positions, w_ada, b_ada, w_in, q_norm_g, k_norm_g, idx_k_ln_g, idx_k_ln_b,
           conv_dw, conv_b, conv_ln_g, conv_ln_b, out_g_attn, out_g_conv, w_out, peer_wq,
           peer_sub_keys, peer_u, peer_v):
    bsz, seq, d = x.shape
    depth = w_ada.shape[0]
    n = bsz * seq
    w = ATTN_WIDTH
    cw = d - w
    assert bsz <= SUBLANES and cw % LANES == 0

    tm = _tile_size(seq, 512)
    tq = _tile_size(seq, 256)
    tt = _tile_size(seq, 512)
    tr = _tile_size(seq, 256)
    tp = _tile_size(seq, 512)
    ec = 2048

    c_pad = jnp.zeros((SUBLANES, d), F32).at[:bsz].set(c)
    mod = _ada(c_pad, w_ada, b_ada)[:, :bsz]
    rc, rs1, rs2 = _rope_lane_tables(positions)
    hd_id = jnp.arange(w) // HEAD_DIM
    gsum = (hd_id[:, None] == hd_id[None, :]).astype(BF16)

    o_q, o_k, o_v, o_qi, o_ki, o_wi = 0, w, 2 * w, 3 * w, 4 * w, 4 * w + IDX_DIM
    o_cu = o_wi + IDX_HEADS

    x2 = x.reshape(n, d)
    for l in range(depth):
        sh1, sc1, g1, sh2, sc2, g2 = [mod[l, :, i * d:(i + 1) * d].reshape(bsz, 1, d) for i in range(6)]
        wl = w_in[l]
        wa = wl[:, o_q:o_ki].astype(BF16)
        wki = wl[:, o_ki:o_wi]
        wk = jnp.concatenate([wki, wki], axis=1).astype(BF16)
        ww = jnp.pad(wl[:, o_wi:o_cu], ((0, 0), (0, LANES - IDX_HEADS))).astype(BF16)
        wc = wl[:, o_cu:].astype(BF16)
        tile_h = lambda g_: jnp.tile(g_, N_HEADS).reshape(1, w)
        dup = lambda g_: jnp.concatenate([g_, g_]).reshape(1, LANES)
        q, k, vt, qi, kk, wit, hg = _inproj(
            x2, sh1, sc1, wa, wk, ww, wc, tile_h(q_norm_g[l]), tile_h(k_norm_g[l]),
            dup(idx_k_ln_g[l]), dup(idx_k_ln_b[l]), rc, rs1, rs2, gsum, seq, tm)

        r3 = lambda t: t.reshape(bsz, seq, t.shape[-1])
        an = _attention(r3(q), r3(k), vt, r3(qi), r3(kk), wit, out_g_attn[l].reshape(1, w), tq)
        w_pad = jnp.pad(conv_dw[l], ((0, CONV_HALO - CONV_KERNEL), (0, 0)))
        cn = _conv(r3(hg), w_pad, conv_b[l].reshape(1, cw), conv_ln_g[l].reshape(1, cw),
                   conv_ln_b[l].reshape(1, cw), out_g_conv[l].reshape(1, cw), tt)

        wo = w_out[l].astype(BF16)
        x1, h2, pq = _outproj(an.reshape(n, w), cn.reshape(n, cw), x2, g1, sh2, sc2,
                              wo[:w], wo[w:], peer_wq[l].astype(BF16), seq, tm)
        a, b, g = _route(pq, peer_sub_keys[l].astype(BF16), tr)
        x2 = _peer(h2, a, b, g, jnp.transpose(peer_u[l]).astype(BF16), peer_v[l].astype(BF16), x1, g2,
                   seq, tp, ec)
    return x2.reshape(bsz, seq, d)
```

```python
import functools

import jax
import jax.numpy as jnp
from jax import lax
from jax.experimental import pallas as pl
from jax.experimental.pallas import tpu as pltpu

F32 = jnp.float32
BF16 = jnp.bfloat16
I32 = jnp.int32

N_HEADS = 8
HEAD_DIM = 64
ATTN_WIDTH = N_HEADS * HEAD_DIM
CONV_KERNEL = 31
IDX_HEADS = 8
IDX_DIM = 64
TOPK_KEYS = 256
ROPE_THETA = 500000.0
ROPE_DIM = HEAD_DIM // 4
PEER_HEADS = 8
PEER_KEYS = 128
PEER_TOPK = 16
PEER_PAIRS = PEER_HEADS * PEER_TOPK
NORM_EPS = 1e-6
LOG2_E = 1.4426950408889634
Q_SCALE = HEAD_DIM ** -0.5 * LOG2_E

LANES = 128
SUBLANES = 8
VMEM_LIMIT = 56 * 1024 * 1024

PACK16 = 2 * SUBLANES
COUNT_CHUNKS = 4

NEG_BIG = -1e30
INT_MIN = -2 ** 31
INT16_MIN = -2 ** 15


def _cparams(sem):
    return pltpu.CompilerParams(dimension_semantics=sem, vmem_limit_bytes=VMEM_LIMIT)


def _nt_dot(a, b):
    return lax.dot_general(a, b, (((1,), (1,)), ((), ())), preferred_element_type=F32)


def _dot(a, b):
    return jnp.dot(a, b, preferred_element_type=F32)


def _ada_kernel(c_ref, w_ref, b_ref, o_ref):
    c = c_ref[...]
    ca = c * jax.nn.sigmoid(c)
    o_ref[0] = _dot(ca, w_ref[0]) + b_ref[0]


def _ada(c_pad, w_ada, b_ada):
    depth, d, n6 = w_ada.shape
    tn = n6 // 4
    return pl.pallas_call(
        _ada_kernel,
        grid=(depth, n6 // tn),
        in_specs=[
            pl.BlockSpec((SUBLANES, d), lambda l, j: (0, 0)),
            pl.BlockSpec((1, d, tn), lambda l, j: (l, 0, j)),
            pl.BlockSpec((1, 1, tn), lambda l, j: (l, 0, j)),
        ],
        out_specs=pl.BlockSpec((1, SUBLANES, tn), lambda l, j: (l, 0, j)),
        out_shape=jax.ShapeDtypeStruct((depth, SUBLANES, n6), F32),
        compiler_params=_cparams(("arbitrary", "arbitrary")),
        name="adaln",
    )(c_pad, w_ada, b_ada.reshape(depth, 1, n6))


def _rope(x, c, s1, s2):
    w = x.shape[-1]
    half = ROPE_DIM // 2
    return x * c + pltpu.roll(x, w - half, 1) * s1 + pltpu.roll(x, half, 1) * s2


def _split_hi_lo(x):
    hi = x.astype(BF16)
    lo = (x - hi.astype(F32)).astype(BF16)
    return hi, lo


def _inproj_kernel(x_ref, sh_ref, sc_ref, wa_ref, wk_ref, ww_ref, wc_ref, gq_ref, gk_ref,
                   lng_ref, lnb_ref, rc_ref, rs1_ref, rs2_ref, gsum_ref,
                   q_ref, k_ref, vt_ref, qi_ref, kk_ref, wit_ref, hg_ref):
    x = x_ref[...]
    ms = jnp.mean(x * x, axis=-1, keepdims=True)
    h = x * lax.rsqrt(ms + NORM_EPS) * (1.0 + sc_ref[0]) + sh_ref[0]
    hb = h.astype(BF16)

    rc1, rs11, rs21 = rc_ref[...], rs1_ref[...], rs2_ref[...]
    reps = ATTN_WIDTH // LANES
    rc = jnp.concatenate([rc1] * reps, axis=1)
    rs1 = jnp.concatenate([rs11] * reps, axis=1)
    rs2 = jnp.concatenate([rs21] * reps, axis=1)
    gsum = gsum_ref[...]

    def head_norm_rope(t, gain):
        hi, lo = _split_hi_lo(t * t)
        ssq = _dot(hi, gsum) + _dot(lo, gsum)
        tn = t * lax.rsqrt(ssq * (1.0 / HEAD_DIM) + NORM_EPS) * gain
        return _rope(tn, rc, rs1, rs2)

    lo_half = lax.broadcasted_iota(I32, (x.shape[0], LANES), 1) < HEAD_DIM

    def one_head_per_group(t):
        parts = []
        for p in range(reps):
            g = t[:, p * LANES:(p + 1) * LANES]
            parts += [jnp.where(lo_half, g, 0.0), jnp.where(lo_half, 0.0, g)]
        return jnp.concatenate(parts, axis=1).astype(BF16)

    w = ATTN_WIDTH
    pa = _dot(hb, wa_ref[...])
    q_ref[...] = one_head_per_group(head_norm_rope(pa[:, 0:w], gq_ref[...]) * Q_SCALE)
    k_ref[...] = head_norm_rope(pa[:, w:2 * w], gk_ref[...]).astype(BF16)
    vt_ref[...] = jnp.transpose(pa[:, 2 * w:3 * w]).astype(BF16)
    qi_ref[...] = one_head_per_group(_rope(pa[:, 3 * w:4 * w], rc, rs1, rs2) * (IDX_DIM ** -0.5))

    pk = _dot(hb, wk_ref[...])
    mu = jnp.mean(pk, axis=-1, keepdims=True)
    var = jnp.mean(jnp.square(pk - mu), axis=-1, keepdims=True)
    kn = (pk - mu) * lax.rsqrt(var + NORM_EPS) * lng_ref[...] + lnb_ref[...]
    kk_ref[...] = _rope(kn, rc1, rs11, rs21).astype(BF16)

    wit = jnp.transpose(_dot(hb, ww_ref[...]) * (IDX_HEADS ** -0.5))
    wit_ref[...] = wit[:IDX_HEADS]

    pc = _dot(hb, wc_ref[...])
    cw = pc.shape[1] // 2
    hg_ref[...] = pc[:, :cw] * jax.nn.sigmoid(pc[:, cw:])


def _inproj(x2, sh, sc, wa, wk, ww, wc, gq, gk, lng, lnb, rc, rs1, rs2, gsum, seq, tm):
    n, d = x2.shape
    per_b = seq // tm
    row = lambda i: (i, 0)
    const = lambda i: (0, 0)
    bat = lambda i: (i // per_b, 0, 0)
    cw = wc.shape[1] // 2
    outs = [
        jax.ShapeDtypeStruct((n, 2 * ATTN_WIDTH), BF16),
        jax.ShapeDtypeStruct((n, ATTN_WIDTH), BF16),
        jax.ShapeDtypeStruct((ATTN_WIDTH, n), BF16),
        jax.ShapeDtypeStruct((n, 2 * ATTN_WIDTH), BF16),
        jax.ShapeDtypeStruct((n, LANES), BF16),
        jax.ShapeDtypeStruct((IDX_HEADS, n), F32),
        jax.ShapeDtypeStruct((n, cw), F32),
    ]
    col = lambda i: (0, i)
    token_minor = (False, False, True, False, False, True, False)
    out_specs = [pl.BlockSpec((o.shape[0], tm), col) if tmin else pl.BlockSpec((tm, o.shape[1]), row)
                 for o, tmin in zip(outs, token_minor)]
    return pl.pallas_call(
        _inproj_kernel,
        grid=(n // tm,),
        in_specs=[
            pl.BlockSpec((tm, d), row),
            pl.BlockSpec((1, 1, d), bat),
            pl.BlockSpec((1, 1, d), bat),
            pl.BlockSpec(wa.shape, const),
            pl.BlockSpec(wk.shape, const),
            pl.BlockSpec(ww.shape, const),
            pl.BlockSpec(wc.shape, const),
            pl.BlockSpec(gq.shape, const),
            pl.BlockSpec(gk.shape, const),
            pl.BlockSpec(lng.shape, const),
            pl.BlockSpec(lnb.shape, const),
            pl.BlockSpec((tm, LANES), row),
            pl.BlockSpec((tm, LANES), row),
            pl.BlockSpec((tm, LANES), row),
            pl.BlockSpec(gsum.shape, const),
        ],
        out_specs=out_specs,
        out_shape=outs,
        compiler_params=_cparams(("arbitrary",)),
        name="inproj",
    )(x2, sh, sc, wa, wk, ww, wc, gq, gk, lng, lnb, rc, rs1, rs2, gsum)


CONV_HALO = 32
CONV_ROWS = 64


def _conv_kernel(cur_ref, halo_ref, w_ref, b_ref, lg_ref, lb_ref, og_ref, o_ref, ext_ref, *, tt):
    i = pl.program_id(1)
    ext_ref[0, 0:CONV_HALO, :] = jnp.where(i > 0, halo_ref[0], 0.0)
    ext_ref[0, CONV_HALO:CONV_HALO + tt, :] = cur_ref[0]
    rows = CONV_HALO + tt - SUBLANES
    for s in range(1, SUBLANES):
        ext_ref[s, 0:rows, :] = ext_ref[0, s:s + rows, :]
    lead = CONV_HALO - (CONV_KERNEL - 1)
    for r0 in range(0, tt, CONV_ROWS):
        acc = jnp.zeros((CONV_ROWS, ext_ref.shape[2]), F32)
        for j in range(CONV_KERNEL):
            s = (lead + j) % SUBLANES
            base = r0 + lead + j - s
            acc = acc + ext_ref[s, base:base + CONV_ROWS, :] * w_ref[j:j + 1, :]
        hcv = acc + b_ref[...]
        mu = jnp.mean(hcv, axis=-1, keepdims=True)
        var = jnp.mean(jnp.square(hcv - mu), axis=-1, keepdims=True)
        y = (hcv - mu) * lax.rsqrt(var + NORM_EPS) * lg_ref[...] + lb_ref[...]
        y = y * jax.nn.sigmoid(y)
        ms = jnp.mean(y * y, axis=-1, keepdims=True)
        o_ref[0, r0:r0 + CONV_ROWS, :] = (y * lax.rsqrt(ms + NORM_EPS) * og_ref[...]).astype(BF16)


def _conv(hg3, w_pad, b, lg, lb, og, tt):
    bsz, seq, cw = hg3.shape
    hb = tt // CONV_HALO
    const = lambda bb, i: (0, 0)
    return pl.pallas_call(
        functools.partial(_conv_kernel, tt=tt),
        grid=(bsz, seq // tt),
        in_specs=[
            pl.BlockSpec((1, tt, cw), lambda bb, i: (bb, i, 0)),
            pl.BlockSpec((1, CONV_HALO, cw), lambda bb, i: (bb, jnp.maximum(i * hb - 1, 0), 0)),
            pl.BlockSpec(w_pad.shape, const),
            pl.BlockSpec(b.shape, const),
            pl.BlockSpec(lg.shape, const),
            pl.BlockSpec(lb.shape, const),
            pl.BlockSpec(og.shape, const),
        ],
        out_specs=pl.BlockSpec((1, tt, cw), lambda bb, i: (bb, i, 0)),
        out_shape=jax.ShapeDtypeStruct((bsz, seq, cw), BF16),
        scratch_shapes=[pltpu.VMEM((SUBLANES, CONV_HALO + tt, cw), F32)],
        compiler_params=_cparams(("arbitrary", "arbitrary")),
        name="conv",
    )(hg3, hg3, w_pad, b, lg, lb, og)


def _attn_kernel(q_ref, qi_ref, wit_ref, k_ref, vt_ref, kk_ref, og_ref, o_ref,
                 key_ref, k16_ref, bias_ref, m_ref, a_ref, acc_ref, lm_ref, p_ref, qt_ref, mx_ref, *, tq, n_sel, seq):
    i = pl.program_id(1)
    nk = i + 1
    krow = lax.broadcasted_iota(I32, (tq, tq), 0)
    qcol = lax.broadcasted_iota(I32, (tq, tq), 1) + i * tq

    def chunk_rows(c):
        return pl.ds(pl.multiple_of(c * tq, tq), tq)

    wit = wit_ref[...]

    def score_chunk(c):
        kic = kk_ref[0, chunk_rows(c), :]
        s = jnp.zeros((tq, tq), F32)
        for h in range(IDX_HEADS):
            qh = qi_ref[0, :, h * LANES:(h + 1) * LANES]
            s = s + wit[h:h + 1, :] * jnp.maximum(_nt_dot(kic, qh), 0.0)
        s = jnp.where(krow + c * tq <= qcol, s, -jnp.inf)
        bits = pltpu.bitcast(s, I32)
        key = jnp.where(bits < 0, bits ^ jnp.int32(0x7FFFFFFF), bits)
        key_ref[c] = jnp.where(s == 0.0, 0, key)

    def score_pair(cp, carry):
        score_chunk(2 * cp)
        score_chunk(2 * cp + 1)
        return carry

    lax.fori_loop(0, nk // 2, score_pair, 0)

    @pl.when(nk % 2 == 1)
    def _():
        score_chunk(nk - 1)

    def count(pred_fn):
        def body(c, acc):
            hit = jnp.where(pred_fn(c, key_ref[c]), 1, 0)
            return acc + jnp.sum(hit.reshape(tq // SUBLANES, SUBLANES, tq), axis=0)
        acc = lax.fori_loop(0, nk, body, jnp.zeros((SUBLANES, tq), I32))
        return jnp.sum(acc, axis=0, keepdims=True)

    def count_ge(thr):
        return count(lambda c, key: key >= thr)

    grp = tq // PACK16
    one_h, zero_h = jnp.ones((), BF16), jnp.zeros((), BF16)

    def count16(thr):
        thr16 = jnp.broadcast_to(thr, (PACK16, tq)).astype(jnp.int16)

        def body(cp, acc):
            for c in [COUNT_CHUNKS * cp + u for u in range(COUNT_CHUNKS)]:
                hit = jnp.where(k16_ref[c].reshape(grp, PACK16, tq) >= thr16[None], one_h, zero_h)
                parts = [hit[r] for r in range(grp)]
                while len(parts) > 1:
                    parts = [parts[r] + parts[r + 1] for r in range(0, len(parts), 2)]
                acc = acc + parts[0].astype(F32)
            return acc

        acc = lax.fori_loop(0, pl.cdiv(nk, COUNT_CHUNKS), body, jnp.zeros((PACK16, tq), F32))
        return jnp.sum(acc, axis=0, keepdims=True).astype(I32)

    def search16():
        def step(bi, cand):
            trial = cand + jnp.left_shift(jnp.int32(1), 15 - bi)
            return jnp.where(count16(trial) >= n_sel, trial, cand)
        return lax.fori_loop(0, 16, step, jnp.full((1, tq), INT16_MIN, I32))

    def store_upper(c, carry):
        k16_ref[c] = lax.shift_right_arithmetic(key_ref[c], 16).astype(jnp.int16)
        return carry

    lax.fori_loop(0, nk, store_upper, 0)

    for u in range(COUNT_CHUNKS - 1):
        @pl.when(nk + u < pl.cdiv(nk, COUNT_CHUNKS) * COUNT_CHUNKS)
        def _():
            k16_ref[nk + u] = jnp.full((tq, tq), INT16_MIN, jnp.int16)

    tau_hi = search16()

    def store_lower(c, carry):
        key = key_ref[c]
        hi = lax.shift_right_arithmetic(key, 16)
        lo = (key & 0xFFFF) + INT16_MIN
        k16_ref[c] = jnp.where(hi == tau_hi, lo, jnp.where(hi > tau_hi, -INT16_MIN - 1, INT16_MIN)
                               ).astype(jnp.int16)
        return carry

    lax.fori_loop(0, nk, store_lower, 0)
    tau = jnp.left_shift(tau_hi, 16) | (search16() - INT16_MIN)
    n_ge = count_ge(tau)
    need = n_sel - count_ge(tau + 1)

    def tie_cut():
        def step(bi, jm):
            trial = jm + jnp.left_shift(jnp.int32(1), (seq.bit_length() - 2) - bi)
            f = count(lambda c, key: jnp.where(key == tau, krow + c * tq, seq) < trial)
            return jnp.where(f < need, trial, jm)
        return lax.fori_loop(0, seq.bit_length() - 1, step, jnp.zeros((1, tq), I32))

    has_tie = jnp.max(n_ge) > n_sel
    jm = lax.cond(has_tie, tie_cut, lambda: jnp.full((1, tq), seq, I32))

    def bias_chunk(c, carry):
        key = key_ref[c]
        kidx = krow + c * tq
        tied = jnp.where(kidx <= jm, 0.0, NEG_BIG)
        above = jnp.where(key > tau, 0.0, NEG_BIG)
        bias_ref[c] = jnp.where(kidx <= qcol, jnp.where(key == tau, tied, above), NEG_BIG)
        return carry

    lax.fori_loop(0, nk, bias_chunk, 0)

    m_ref[...] = jnp.full(m_ref.shape, NEG_BIG, F32)
    acc_ref[...] = jnp.zeros(acc_ref.shape, F32)
    ones_rows = jnp.ones((PACK16, tq), BF16)

    for hd in range(N_HEADS):
        qh = q_ref[0, :, hd * LANES:(hd + 1) * LANES]
        qt_ref[hd] = jnp.transpose(qh.astype(F32)).astype(BF16)

    def logits(c, slot):
        rows = chunk_rows(c)
        for hd in range(N_HEADS):
            p = hd // 2
            kp = k_ref[0, rows, p * LANES:(p + 1) * LANES]
            lm = _dot(kp, qt_ref[hd]) + bias_ref[c]
            lm_ref[slot, hd] = lm
            mx_ref[slot, hd] = jnp.max(lm, axis=0, keepdims=True)

    def softmax_pv(c, slot):
        for hd in range(N_HEADS):
            m_old = m_ref[hd]
            m_new = jnp.maximum(m_old, mx_ref[slot, hd])
            m_ref[hd] = m_new
            a_ref[hd] = jnp.exp2(m_old - m_new)
            p_ref[hd] = jnp.exp2(lm_ref[slot, hd] - m_new).astype(BF16)
        rows = chunk_rows(c)
        for hd in range(N_HEADS):
            vth = jnp.concatenate([vt_ref[hd * HEAD_DIM:(hd + 1) * HEAD_DIM, rows], ones_rows], axis=0)
            acc_ref[hd] = acc_ref[hd] * a_ref[hd] + _dot(vth, p_ref[hd])

    def attn_pair(cp, carry):
        c0 = 2 * cp
        logits(c0 + 1, 1)
        softmax_pv(c0, 0)
        logits(jnp.minimum(c0 + 2, last_c), 0)
        softmax_pv(c0 + 1, 1)
        return carry

    @pl.when(nk % 2 == 1)
    def _():
        bias_ref[nk] = jnp.full((tq, tq), NEG_BIG, F32)

    last_c = nk - 1 + nk % 2
    logits(0, 0)
    lax.fori_loop(0, (nk + 1) // 2, attn_pair, 0)

    ot = jnp.concatenate([acc_ref[hd, :HEAD_DIM] / acc_ref[hd, HEAD_DIM:HEAD_DIM + 1]
                          for hd in range(N_HEADS)], axis=0)
    ms = jnp.mean(ot * ot, axis=0, keepdims=True)
    ot = ot * lax.rsqrt(ms + NORM_EPS)
    o_ref[0] = (jnp.transpose(ot) * og_ref[...]).astype(BF16)


def _attention(q3, k3, vt, qi3, kk3, wit, og, tq):
    bsz, seq, w = k3.shape
    n_sel = min(TOPK_KEYS, seq // 4)
    nc = seq // tq
    assert nc % COUNT_CHUNKS == 0 and nc % 2 == 0, "the count and softmax loops walk groups of key chunks"
    qblk = lambda bb, i: (bb, i, 0)
    full = lambda bb, i: (bb, 0, 0)
    return pl.pallas_call(
        functools.partial(_attn_kernel, tq=tq, n_sel=n_sel, seq=seq),
        grid=(bsz, nc),
        in_specs=[
            pl.BlockSpec((1, tq, 2 * w), qblk),
            pl.BlockSpec((1, tq, 2 * w), qblk),
            pl.BlockSpec((SUBLANES, tq), lambda bb, i: (0, bb * nc + i)),
            pl.BlockSpec((1, seq, w), full),
            pl.BlockSpec((w, seq), lambda bb, i: (0, bb)),
            pl.BlockSpec((1, seq, LANES), full),
            pl.BlockSpec(og.shape, lambda bb, i: (0, 0)),
        ],
        out_specs=pl.BlockSpec((1, tq, w), qblk),
        out_shape=jax.ShapeDtypeStruct((bsz, seq, w), BF16),
        scratch_shapes=[
            pltpu.VMEM((nc, tq, tq), I32),
            pltpu.VMEM((nc, tq, tq), jnp.int16),
            pltpu.VMEM((nc, tq, tq), F32),
            pltpu.VMEM((N_HEADS, 1, tq), F32),
            pltpu.VMEM((N_HEADS, 1, tq), F32),
            pltpu.VMEM((N_HEADS, HEAD_DIM + PACK16, tq), F32),
            pltpu.VMEM((2, N_HEADS, tq, tq), F32),
            pltpu.VMEM((N_HEADS, tq, tq), BF16),
            pltpu.VMEM((N_HEADS, LANES, tq), BF16),
            pltpu.VMEM((2, N_HEADS, 1, tq), F32),
        ],
        compiler_params=_cparams(("arbitrary", "arbitrary")),
        name="dsa_attention",
    )(q3, qi3, wit, k3, vt, kk3, og)


def _outproj_kernel(an_ref, cn_ref, x_ref, g1_ref, sh_ref, sc_ref, woa_ref, woc_ref, wq_ref,
                    x1_ref, h2_ref, pq_ref):
    y = _dot(an_ref[...], woa_ref[...]) + _dot(cn_ref[...], woc_ref[...])
    x1 = x_ref[...] + g1_ref[0] * y
    x1_ref[...] = x1
    ms = jnp.mean(x1 * x1, axis=-1, keepdims=True)
    h2 = (x1 * lax.rsqrt(ms + NORM_EPS) * (1.0 + sc_ref[0]) + sh_ref[0]).astype(BF16)
    h2_ref[...] = h2
    pq_ref[...] = _dot(h2, wq_ref[...]).astype(BF16)


def _outproj(an, cn, x2, g1, sh2, sc2, woa, woc, wq, seq, tm):
    n, d = x2.shape
    per_b = seq // tm
    row = lambda i: (i, 0)
    const = lambda i: (0, 0)
    bat = lambda i: (i // per_b, 0, 0)
    return pl.pallas_call(
        _outproj_kernel,
        grid=(n // tm,),
        in_specs=[
            pl.BlockSpec((tm, an.shape[1]), row),
            pl.BlockSpec((tm, cn.shape[1]), row),
            pl.BlockSpec((tm, d), row),
            pl.BlockSpec((1, 1, d), bat),
            pl.BlockSpec((1, 1, d), bat),
            pl.BlockSpec((1, 1, d), bat),
            pl.BlockSpec(woa.shape, const),
            pl.BlockSpec(woc.shape, const),
            pl.BlockSpec(wq.shape, const),
        ],
        out_specs=[pl.BlockSpec((tm, d), row), pl.BlockSpec((tm, d), row),
                   pl.BlockSpec((tm, wq.shape[1]), row)],
        out_shape=[jax.ShapeDtypeStruct((n, d), F32), jax.ShapeDtypeStruct((n, d), BF16),
                   jax.ShapeDtypeStruct((n, wq.shape[1]), BF16)],
        compiler_params=_cparams(("arbitrary",)),
        name="outproj",
    )(an, cn, x2, g1, sh2, sc2, woa, woc, wq)


def _top_rows(s, rank=None, ids=None):
    return _top_rows_multi([(s, rank, ids)])[0]


def _top_rows_multi(problems):
    state = []
    for s, rank, ids in problems:
        if rank is None:
            rank = lax.broadcasted_iota(I32, s.shape, 0).astype(F32)
        state.append([s, rank, ids, [], []])
    for _ in range(PEER_TOPK):
        for st in state:
            s, rank, ids, vals, picks = st
            m = jnp.max(s, axis=0, keepdims=True)
            pos = jnp.min(jnp.where(s == m, rank, jnp.inf), axis=0, keepdims=True)
            hit = rank == pos
            vals.append(m)
            picks.append(pos if ids is None else jnp.max(jnp.where(hit, ids, -1.0), axis=0, keepdims=True))
            st[0] = jnp.where(hit, -jnp.inf, s)
    return [(jnp.concatenate(st[3], axis=0), jnp.concatenate(st[4], axis=0)) for st in state]


def _pair_candidates(v1, i1, v2, i2):
    tr = v1.shape[1]
    k = PEER_TOPK
    r8 = lax.broadcasted_iota(I32, (SUBLANES, tr), 0).astype(F32)
    rk = lax.broadcasted_iota(I32, (k, tr), 0).astype(F32)
    sums = [v1 + v2[0:1]]
    eids = [i1 * PEER_KEYS + i2[0:1]]
    order = [rk * k]
    for b in range(1, SUBLANES):
        ok = r8 < k // (b + 1)
        sums.append(jnp.where(ok, v1[0:SUBLANES] + v2[b:b + 1], -jnp.inf))
        eids.append(i1[0:SUBLANES] * PEER_KEYS + i2[b:b + 1])
        order.append(r8 * k + b)
    sums.append(v1[0:1] + v2[SUBLANES:k])
    eids.append(i1[0:1] * PEER_KEYS + i2[SUBLANES:k])
    order.append(r8 + SUBLANES)
    return jnp.concatenate(sums, axis=0), jnp.concatenate(eids, axis=0), jnp.concatenate(order, axis=0)


def _route_kernel(pq_ref, sk_ref, a_ref, b_ref, g_ref, e_scr, g_scr):
    h = pl.program_id(1)
    half = PEER_KEYS
    s1 = _nt_dot(sk_ref[0, 0], pq_ref[:, :half])
    s2 = _nt_dot(sk_ref[0, 1], pq_ref[:, half:])
    (v1, i1), (v2, i2) = _top_rows_multi([(s1, None, None), (s2, None, None)])
    cand, cid, order = _pair_candidates(v1, i1, v2, i2)
    ts, e = _top_rows(cand, order, cid)
    ex = jnp.exp(ts - ts[0:1])
    g = ex / jnp.sum(ex, axis=0, keepdims=True)
    rows = pl.ds(pl.multiple_of(h * PEER_TOPK, PEER_TOPK), PEER_TOPK)
    e_scr[rows, :] = e
    g_scr[rows, :] = g

    @pl.when(h == PEER_HEADS - 1)
    def _():
        e_all = e_scr[...]
        first = jnp.floor(e_all * (1.0 / PEER_KEYS))
        a_ref[...] = jnp.transpose(first)
        b_ref[...] = jnp.transpose(e_all - first * PEER_KEYS)
        g_ref[...] = jnp.transpose(g_scr[...])


def _route(pq, sk, tr):
    n = pq.shape[0]
    qd = 2 * PEER_KEYS
    out = jax.ShapeDtypeStruct((n, PEER_PAIRS), F32)
    oblk = pl.BlockSpec((tr, PEER_PAIRS), lambda i, h: (i, 0))
    return pl.pallas_call(
        _route_kernel,
        grid=(n // tr, PEER_HEADS),
        in_specs=[
            pl.BlockSpec((tr, qd), lambda i, h: (i, h)),
            pl.BlockSpec((1, 2, PEER_KEYS, PEER_KEYS), lambda i, h: (h, 0, 0, 0)),
        ],
        out_specs=[oblk, oblk, oblk],
        out_shape=[out, out, out],
        scratch_shapes=[pltpu.VMEM((PEER_PAIRS, tr), F32), pltpu.VMEM((PEER_PAIRS, tr), F32)],
        compiler_params=_cparams(("arbitrary", "arbitrary")),
        name="peer_route",
    )(pq, sk)


C_HALF = PEER_KEYS // 2
C_PITCH = C_HALF + SUBLANES
C_UNROLL = 64
HI16 = -65536


def _gelu(x):
    return 0.5 * x * (1.0 + lax.erf(x * (2.0 ** -0.5)))


def _bf16_bits(x):
    return pltpu.bitcast(x.astype(BF16).astype(F32), I32)


def _peer_kernel(h2_ref, a_ref, b_ref, g_ref, ulo_ref, uhi_ref, vlo_ref, vhi_ref, x1_ref, g2_ref, o_ref,
                 c_scr, acc_ref, *, tm, ec):
    j = pl.program_id(1)
    nj = pl.num_programs(1)
    k1 = ec // 2 // PEER_KEYS

    @pl.when(j == 0)
    def _():
        srow = lax.broadcasted_iota(I32, (PEER_KEYS, PEER_PAIRS), 0).astype(F32)

        def tokens(tb, carry):
            for u in range(C_UNROLL):
                t = tb * C_UNROLL + u
                a_row = a_ref[pl.ds(t, 1), :]
                b_row = b_ref[pl.ds(t, 1), :]
                g_row = g_ref[pl.ds(t, 1), :]
                at = jnp.where(srow == a_row, g_row, 0.0).astype(BF16)
                bt = jnp.where(srow == b_row, 1.0, 0.0).astype(BF16)
                ct = _nt_dot(at, bt)
                word = (lax.shift_right_logical(_bf16_bits(ct[:C_HALF]), 16)
                        | (_bf16_bits(ct[C_HALF:]) & HI16))
                c_scr[pl.ds(pl.multiple_of(t * C_PITCH, SUBLANES), C_HALF), :] = word
            return carry

        lax.fori_loop(0, tm // C_UNROLL, tokens, 0)
        acc_ref[...] = jnp.zeros(acc_ref.shape, F32)

    h2 = h2_ref[...]
    act_lo = _gelu(_dot(h2, ulo_ref[...]))
    act_hi = _gelu(_dot(h2, uhi_ref[...]))
    p_lo, p_hi = [], []
    for ii in range(k1):
        word = c_scr[pl.ds(j * k1 + ii, tm, stride=C_PITCH), :]
        c_lo = pltpu.bitcast(lax.shift_left(word, 16), F32)
        c_hi = pltpu.bitcast(word & HI16, F32)
        cols = slice(ii * PEER_KEYS, (ii + 1) * PEER_KEYS)
        p_lo.append((act_lo[:, cols] * c_lo).astype(BF16))
        p_hi.append((act_hi[:, cols] * c_hi).astype(BF16))
    acc_ref[...] += (_dot(jnp.concatenate(p_lo, axis=1), vlo_ref[...])
                     + _dot(jnp.concatenate(p_hi, axis=1), vhi_ref[...]))

    @pl.when(j == nj - 1)
    def _():
        o_ref[...] = x1_ref[...] + g2_ref[0] * acc_ref[...]


def _peer(h2, a, b, g, ut, v, x1, g2, seq, tm, ec):
    n, d = x1.shape
    ne = v.shape[0]
    per_b = seq // tm
    eb = ec // 2
    nhalf = ne // 2 // eb
    row = lambda i, j: (i, 0)
    lo = lambda i, j: (j, 0)
    hi = lambda i, j: (j + nhalf, 0)
    return pl.pallas_call(
        functools.partial(_peer_kernel, tm=tm, ec=ec),
        grid=(n // tm, ne // ec),
        in_specs=[
            pl.BlockSpec((tm, d), row),
            pl.BlockSpec((tm, PEER_PAIRS), row),
            pl.BlockSpec((tm, PEER_PAIRS), row),
            pl.BlockSpec((tm, PEER_PAIRS), row),
            pl.BlockSpec((d, eb), lambda i, j: (0, j)),
            pl.BlockSpec((d, eb), lambda i, j: (0, j + nhalf)),
            pl.BlockSpec((eb, d), lo),
            pl.BlockSpec((eb, d), hi),
            pl.BlockSpec((tm, d), row),
            pl.BlockSpec((1, 1, d), lambda i, j: (i // per_b, 0, 0)),
        ],
        out_specs=pl.BlockSpec((tm, d), row),
        out_shape=jax.ShapeDtypeStruct((n, d), F32),
        scratch_shapes=[pltpu.VMEM((tm * C_PITCH, PEER_KEYS), I32), pltpu.VMEM((tm, d), F32)],
        compiler_params=_cparams(("arbitrary", "arbitrary")),
        name="peer_experts",
    )(h2, a, b, g, ut, ut, v, v, x1, g2)


def _rope_lane_tables(positions):
    half = ROPE_DIM // 2
    inv_freq = jnp.power(jnp.float32(ROPE_THETA), -jnp.arange(0, ROPE_DIM, 2, dtype=F32) / ROPE_DIM)
    ang = positions.astype(F32).reshape(-1, 1) * inv_freq
    cos, sin = jnp.cos(ang), jnp.sin(ang)
    j = jnp.arange(LANES) % HEAD_DIM
    cos_l, sin_l = cos[:, j % half], sin[:, j % half]
    rc = jnp.where(j < ROPE_DIM, cos_l, 1.0)
    rs1 = jnp.where(j < half, -sin_l, 0.0)
    rs2 = jnp.where((j >= half) & (j < ROPE_DIM), sin_l, 0.0)
    return rc, rs1, rs2


def _tile_size(total, want):
    t = min(want, total)
    assert total % t == 0
    return t


def kernel(x, c, positions, w_ada, b_ada, w_in, q_norm_g, k_norm_g, idx_k_ln_g, idx_k_ln_b,
           conv_dw, conv_b, conv_ln_g, conv_ln_b, out_g_attn, out_g_conv, w_out, peer_wq,
           peer_sub_keys, peer_u, peer_v):
    bsz, seq, d = x.shape
    depth = w_ada.shape[0]
    n = bsz * seq
    w = ATTN_WIDTH
    cw = d - w
    assert bsz <= SUBLANES and cw % LANES == 0

    tm = _tile_size(seq, 512)
    tq = _tile_size(seq, 256)
    tt = _tile_size(seq, 512)
    tr = _tile_size(seq, 1024)
    tp = _tile_size(seq, 512)
    ec = 2048

    c_pad = jnp.zeros((SUBLANES, d), F32).at[:bsz].set(c)
    mod = _ada(c_pad, w_ada, b_ada)[:, :bsz]
    rc, rs1, rs2 = _rope_lane_tables(positions)
    hd_id = jnp.arange(w) // HEAD_DIM
    gsum = (hd_id[:, None] == hd_id[None, :]).astype(BF16)

    o_q, o_k, o_v, o_qi, o_ki, o_wi = 0, w, 2 * w, 3 * w, 4 * w, 4 * w + IDX_DIM
    o_cu = o_wi + IDX_HEADS

    x2 = x.reshape(n, d)
    for l in range(depth):
        sh1, sc1, g1, sh2, sc2, g2 = [mod[l, :, i * d:(i + 1) * d].reshape(bsz, 1, d) for i in range(6)]
        wl = w_in[l]
        wa = wl[:, o_q:o_ki].astype(BF16)
        wki = wl[:, o_ki:o_wi]
        wk = jnp.concatenate([wki, wki], axis=1).astype(BF16)
        ww = jnp.pad(wl[:, o_wi:o_cu], ((0, 0), (0, LANES - IDX_HEADS))).astype(BF16)
        wc = wl[:, o_cu:].astype(BF16)
        tile_h = lambda g_: jnp.tile(g_, N_HEADS).reshape(1, w)
        dup = lambda g_: jnp.concatenate([g_, g_]).reshape(1, LANES)
        q, k, vt, qi, kk, wit, hg = _inproj(
            x2, sh1, sc1, wa, wk, ww, wc, tile_h(q_norm_g[l]), tile_h(k_norm_g[l]),
            dup(idx_k_ln_g[l]), dup(idx_k_ln_b[l]), rc, rs1, rs2, gsum, seq, tm)

        r3 = lambda t: t.reshape(bsz, seq, t.shape[-1])
        an = _attention(r3(q), r3(k), vt, r3(qi), r3(kk), wit, out_g_attn[l].reshape(1, w), tq)
        w_pad = jnp.pad(conv_dw[l], ((0, CONV_HALO - CONV_KERNEL), (0, 0)))
        cn = _conv(r3(hg), w_pad, conv_b[l].reshape(1, cw), conv_ln_g[l].reshape(1, cw),
                   conv_ln_b[l].reshape(1, cw), out_g_conv[l].reshape(1, cw), tt)

        wo = w_out[l].astype(BF16)
        x1, h2, pq = _outproj(an.reshape(n, w), cn.reshape(n, cw), x2, g1, sh2, sc2,
                              wo[:w], wo[w:], peer_wq[l].astype(BF16), seq, tm)
        a, b, g = _route(pq, peer_sub_keys[l].astype(BF16), tr)
        x2 = _peer(h2, a, b, g, jnp.transpose(peer_u[l]).astype(BF16), peer_v[l].astype(BF16), x1, g2,
                   seq, tp, ec)
    return x2.reshape(bsz, seq, d)
```

```python
import functools

import jax
import jax.numpy as jnp
from jax import lax
from jax.experimental import pallas as pl
from jax.experimental.pallas import tpu as pltpu

F32 = jnp.float32
BF16 = jnp.bfloat16
I32 = jnp.int32

N_HEADS = 8
HEAD_DIM = 64
ATTN_WIDTH = N_HEADS * HEAD_DIM
CONV_KERNEL = 31
IDX_HEADS = 8
IDX_DIM = 64
TOPK_KEYS = 256
ROPE_THETA = 500000.0
ROPE_DIM = HEAD_DIM // 4
PEER_HEADS = 8
PEER_KEYS = 128
PEER_TOPK = 16
PEER_PAIRS = PEER_HEADS * PEER_TOPK
NORM_EPS = 1e-6
LOG2_E = 1.4426950408889634
Q_SCALE = HEAD_DIM ** -0.5 * LOG2_E

LANES = 128
SUBLANES = 8
VMEM_LIMIT = 56 * 1024 * 1024

PACK16 = 2 * SUBLANES

NEG_BIG = -1e30
INT_MIN = -2 ** 31


def _cparams(sem):
    return pltpu.CompilerParams(dimension_semantics=sem, vmem_limit_bytes=VMEM_LIMIT)


def _nt_dot(a, b):
    return lax.dot_general(a, b, (((1,), (1,)), ((), ())), preferred_element_type=F32)


def _dot(a, b):
    return jnp.dot(a, b, preferred_element_type=F32)


def _ada_kernel(c_ref, w_ref, b_ref, o_ref):
    c = c_ref[...]
    ca = c * jax.nn.sigmoid(c)
    o_ref[0] = _dot(ca, w_ref[0]) + b_ref[0]


def _ada(c_pad, w_ada, b_ada):
    depth, d, n6 = w_ada.shape
    tn = n6 // 4
    return pl.pallas_call(
        _ada_kernel,
        grid=(depth, n6 // tn),
        in_specs=[
            pl.BlockSpec((SUBLANES, d), lambda l, j: (0, 0)),
            pl.BlockSpec((1, d, tn), lambda l, j: (l, 0, j)),
            pl.BlockSpec((1, 1, tn), lambda l, j: (l, 0, j)),
        ],
        out_specs=pl.BlockSpec((1, SUBLANES, tn), lambda l, j: (l, 0, j)),
        out_shape=jax.ShapeDtypeStruct((depth, SUBLANES, n6), F32),
        compiler_params=_cparams(("arbitrary", "arbitrary")),
        name="adaln",
    )(c_pad, w_ada, b_ada.reshape(depth, 1, n6))


def _rope(x, c, s1, s2):
    w = x.shape[-1]
    half = ROPE_DIM // 2
    return x * c + pltpu.roll(x, w - half, 1) * s1 + pltpu.roll(x, half, 1) * s2


def _split_hi_lo(x):
    hi = x.astype(BF16)
    lo = (x - hi.astype(F32)).astype(BF16)
    return hi, lo


def _inproj_kernel(x_ref, sh_ref, sc_ref, wa_ref, wk_ref, ww_ref, wc_ref, gq_ref, gk_ref,
                   lng_ref, lnb_ref, rc_ref, rs1_ref, rs2_ref, gsum_ref,
                   q_ref, k_ref, vt_ref, qi_ref, kk_ref, wit_ref, hg_ref):
    x = x_ref[...]
    ms = jnp.mean(x * x, axis=-1, keepdims=True)
    h = x * lax.rsqrt(ms + NORM_EPS) * (1.0 + sc_ref[0]) + sh_ref[0]
    hb = h.astype(BF16)

    rc1, rs11, rs21 = rc_ref[...], rs1_ref[...], rs2_ref[...]
    reps = ATTN_WIDTH // LANES
    rc = jnp.concatenate([rc1] * reps, axis=1)
    rs1 = jnp.concatenate([rs11] * reps, axis=1)
    rs2 = jnp.concatenate([rs21] * reps, axis=1)
    gsum = gsum_ref[...]

    def head_norm_rope(t, gain):
        hi, lo = _split_hi_lo(t * t)
        ssq = _dot(hi, gsum) + _dot(lo, gsum)
        tn = t * lax.rsqrt(ssq * (1.0 / HEAD_DIM) + NORM_EPS) * gain
        return _rope(tn, rc, rs1, rs2)

    lo_half = lax.broadcasted_iota(I32, (x.shape[0], LANES), 1) < HEAD_DIM

    def one_head_per_group(t):
        parts = []
        for p in range(reps):
            g = t[:, p * LANES:(p + 1) * LANES]
            parts += [jnp.where(lo_half, g, 0.0), jnp.where(lo_half, 0.0, g)]
        return jnp.concatenate(parts, axis=1).astype(BF16)

    w = ATTN_WIDTH
    pa = _dot(hb, wa_ref[...])
    q_ref[...] = one_head_per_group(head_norm_rope(pa[:, 0:w], gq_ref[...]) * Q_SCALE)
    k_ref[...] = head_norm_rope(pa[:, w:2 * w], gk_ref[...]).astype(BF16)
    vt_ref[...] = jnp.transpose(pa[:, 2 * w:3 * w]).astype(BF16)
    qi_ref[...] = one_head_per_group(_rope(pa[:, 3 * w:4 * w], rc, rs1, rs2) * (IDX_DIM ** -0.5))

    pk = _dot(hb, wk_ref[...])
    mu = jnp.mean(pk, axis=-1, keepdims=True)
    var = jnp.mean(jnp.square(pk - mu), axis=-1, keepdims=True)
    kn = (pk - mu) * lax.rsqrt(var + NORM_EPS) * lng_ref[...] + lnb_ref[...]
    kk_ref[...] = _rope(kn, rc1, rs11, rs21).astype(BF16)

    wit = jnp.transpose(_dot(hb, ww_ref[...]) * (IDX_HEADS ** -0.5))
    wit_ref[...] = wit[:IDX_HEADS]

    pc = _dot(hb, wc_ref[...])
    cw = pc.shape[1] // 2
    hg_ref[...] = pc[:, :cw] * jax.nn.sigmoid(pc[:, cw:])


def _inproj(x2, sh, sc, wa, wk, ww, wc, gq, gk, lng, lnb, rc, rs1, rs2, gsum, seq, tm):
    n, d = x2.shape
    per_b = seq // tm
    row = lambda i: (i, 0)
    const = lambda i: (0, 0)
    bat = lambda i: (i // per_b, 0, 0)
    cw = wc.shape[1] // 2
    outs = [
        jax.ShapeDtypeStruct((n, 2 * ATTN_WIDTH), BF16),
        jax.ShapeDtypeStruct((n, ATTN_WIDTH), BF16),
        jax.ShapeDtypeStruct((ATTN_WIDTH, n), BF16),
        jax.ShapeDtypeStruct((n, 2 * ATTN_WIDTH), BF16),
        jax.ShapeDtypeStruct((n, LANES), BF16),
        jax.ShapeDtypeStruct((IDX_HEADS, n), F32),
        jax.ShapeDtypeStruct((n, cw), F32),
    ]
    col = lambda i: (0, i)
    token_minor = (False, False, True, False, False, True, False)
    out_specs = [pl.BlockSpec((o.shape[0], tm), col) if tmin else pl.BlockSpec((tm, o.shape[1]), row)
                 for o, tmin in zip(outs, token_minor)]
    return pl.pallas_call(
        _inproj_kernel,
        grid=(n // tm,),
        in_specs=[
            pl.BlockSpec((tm, d), row),
            pl.BlockSpec((1, 1, d), bat),
            pl.BlockSpec((1, 1, d), bat),
            pl.BlockSpec(wa.shape, const),
            pl.BlockSpec(wk.shape, const),
            pl.BlockSpec(ww.shape, const),
            pl.BlockSpec(wc.shape, const),
            pl.BlockSpec(gq.shape, const),
            pl.BlockSpec(gk.shape, const),
            pl.BlockSpec(lng.shape, const),
            pl.BlockSpec(lnb.shape, const),
            pl.BlockSpec((tm, LANES), row),
            pl.BlockSpec((tm, LANES), row),
            pl.BlockSpec((tm, LANES), row),
            pl.BlockSpec(gsum.shape, const),
        ],
        out_specs=out_specs,
        out_shape=outs,
        compiler_params=_cparams(("arbitrary",)),
        name="inproj",
    )(x2, sh, sc, wa, wk, ww, wc, gq, gk, lng, lnb, rc, rs1, rs2, gsum)


CONV_HALO = 32
CONV_ROWS = 64


def _conv_kernel(cur_ref, halo_ref, w_ref, b_ref, lg_ref, lb_ref, og_ref, o_ref, ext_ref, *, tt):
    i = pl.program_id(1)
    ext_ref[0, 0:CONV_HALO, :] = jnp.where(i > 0, halo_ref[0], 0.0)
    ext_ref[0, CONV_HALO:CONV_HALO + tt, :] = cur_ref[0]
    rows = CONV_HALO + tt - SUBLANES
    for s in range(1, SUBLANES):
        ext_ref[s, 0:rows, :] = ext_ref[0, s:s + rows, :]
    lead = CONV_HALO - (CONV_KERNEL - 1)
    for r0 in range(0, tt, CONV_ROWS):
        acc = jnp.zeros((CONV_ROWS, ext_ref.shape[2]), F32)
        for j in range(CONV_KERNEL):
            s = (lead + j) % SUBLANES
            base = r0 + lead + j - s
            acc = acc + ext_ref[s, base:base + CONV_ROWS, :] * w_ref[j:j + 1, :]
        hcv = acc + b_ref[...]
        mu = jnp.mean(hcv, axis=-1, keepdims=True)
        var = jnp.mean(jnp.square(hcv - mu), axis=-1, keepdims=True)
        y = (hcv - mu) * lax.rsqrt(var + NORM_EPS) * lg_ref[...] + lb_ref[...]
        y = y * jax.nn.sigmoid(y)
        ms = jnp.mean(y * y, axis=-1, keepdims=True)
        o_ref[0, r0:r0 + CONV_ROWS, :] = (y * lax.rsqrt(ms + NORM_EPS) * og_ref[...]).astype(BF16)


def _conv(hg3, w_pad, b, lg, lb, og, tt):
    bsz, seq, cw = hg3.shape
    hb = tt // CONV_HALO
    const = lambda bb, i: (0, 0)
    return pl.pallas_call(
        functools.partial(_conv_kernel, tt=tt),
        grid=(bsz, seq // tt),
        in_specs=[
            pl.BlockSpec((1, tt, cw), lambda bb, i: (bb, i, 0)),
            pl.BlockSpec((1, CONV_HALO, cw), lambda bb, i: (bb, jnp.maximum(i * hb - 1, 0), 0)),
            pl.BlockSpec(w_pad.shape, const),
            pl.BlockSpec(b.shape, const),
            pl.BlockSpec(lg.shape, const),
            pl.BlockSpec(lb.shape, const),
            pl.BlockSpec(og.shape, const),
        ],
        out_specs=pl.BlockSpec((1, tt, cw), lambda bb, i: (bb, i, 0)),
        out_shape=jax.ShapeDtypeStruct((bsz, seq, cw), BF16),
        scratch_shapes=[pltpu.VMEM((SUBLANES, CONV_HALO + tt, cw), F32)],
        compiler_params=_cparams(("arbitrary", "arbitrary")),
        name="conv",
    )(hg3, hg3, w_pad, b, lg, lb, og)


def _attn_kernel(q_ref, qi_ref, wit_ref, k_ref, vt_ref, kk_ref, og_ref, o_ref,
                 sc_ref, bias_ref, m_ref, a_ref, acc_ref, lm_ref, p_ref, qt_ref, mx_ref, *, tq, n_sel, seq):
    i = pl.program_id(1)
    nk = i + 1
    krow = lax.broadcasted_iota(I32, (tq, tq), 0)
    qcol = lax.broadcasted_iota(I32, (tq, tq), 1) + i * tq

    def chunk_rows(c):
        return pl.ds(pl.multiple_of(c * tq, tq), tq)

    wit = wit_ref[...]

    def score_chunk(c):
        kic = kk_ref[0, chunk_rows(c), :]
        s = jnp.zeros((tq, tq), F32)
        for h in range(IDX_HEADS):
            qh = qi_ref[0, :, h * LANES:(h + 1) * LANES]
            s = s + wit[h:h + 1, :] * jnp.maximum(_nt_dot(kic, qh), 0.0)
        sc_ref[c] = jnp.where(krow + c * tq <= qcol, s, -jnp.inf)

    def score_pair(cp, carry):
        score_chunk(2 * cp)
        score_chunk(2 * cp + 1)
        return carry

    lax.fori_loop(0, nk // 2, score_pair, 0)

    @pl.when(nk % 2 == 1)
    def _():
        score_chunk(nk - 1)

    def count(pred_fn):
        def body(c, acc):
            hit = jnp.where(pred_fn(c, sc_ref[c]), 1, 0)
            return acc + jnp.sum(hit.reshape(tq // SUBLANES, SUBLANES, tq), axis=0)
        acc = lax.fori_loop(0, nk, body, jnp.zeros((SUBLANES, tq), I32))
        return jnp.sum(acc, axis=0, keepdims=True)

    def thr_float(t):
        f = pltpu.bitcast(jnp.where(t < 0, t ^ jnp.int32(0x7FFFFFFF), t), F32)
        return jnp.where(f != f, jnp.where(t < 0, -jnp.inf, jnp.inf), f)

    def count_ge(t):
        thr = thr_float(t)
        return count(lambda c, sc: sc >= thr)

    cand = jnp.where(count_ge(jnp.zeros((1, tq), I32)) >= n_sel, 0, INT_MIN).astype(I32)

    def bit_step(bi, cand):
        trial = cand + jnp.left_shift(jnp.int32(1), 30 - bi)
        return jnp.where(count_ge(trial) >= n_sel, trial, cand)

    tau = thr_float(lax.fori_loop(0, 31, bit_step, cand))
    n_ge = count(lambda c, sc: sc >= tau)
    need = n_sel - count(lambda c, sc: sc > tau)

    def tie_cut():
        def step(bi, jm):
            trial = jm + jnp.left_shift(jnp.int32(1), (seq.bit_length() - 2) - bi)
            f = count(lambda c, sc: jnp.where(sc == tau, krow + c * tq, seq) < trial)
            return jnp.where(f < need, trial, jm)
        return lax.fori_loop(0, seq.bit_length() - 1, step, jnp.zeros((1, tq), I32))

    has_tie = jnp.max(n_ge) > n_sel
    jm = lax.cond(has_tie, tie_cut, lambda: jnp.full((1, tq), seq, I32))

    def bias_chunk(c, carry):
        sc = sc_ref[c]
        kidx = krow + c * tq
        tied = jnp.where(kidx <= jm, 0.0, NEG_BIG)
        above = jnp.where(sc > tau, 0.0, NEG_BIG)
        bias_ref[c] = jnp.where(kidx <= qcol, jnp.where(sc == tau, tied, above), NEG_BIG)
        return carry

    lax.fori_loop(0, nk, bias_chunk, 0)

    m_ref[...] = jnp.full(m_ref.shape, NEG_BIG, F32)
    acc_ref[...] = jnp.zeros(acc_ref.shape, F32)
    ones_rows = jnp.ones((PACK16, tq), BF16)

    for hd in range(N_HEADS):
        qh = q_ref[0, :, hd * LANES:(hd + 1) * LANES]
        qt_ref[hd] = jnp.transpose(qh.astype(F32)).astype(BF16)

    def logits(c, slot):
        rows = chunk_rows(c)
        for hd in range(N_HEADS):
            p = hd // 2
            kp = k_ref[0, rows, p * LANES:(p + 1) * LANES]
            lm = _dot(kp, qt_ref[hd]) + bias_ref[c]
            lm_ref[slot, hd] = lm
            mx_ref[slot, hd] = jnp.max(lm, axis=0, keepdims=True)

    def softmax_pv(c, slot):
        for hd in range(N_HEADS):
            m_old = m_ref[hd]
            m_new = jnp.maximum(m_old, mx_ref[slot, hd])
            m_ref[hd] = m_new
            a_ref[hd] = jnp.exp2(m_old - m_new)
            p_ref[hd] = jnp.exp2(lm_ref[slot, hd] - m_new).astype(BF16)
        rows = chunk_rows(c)
        for hd in range(N_HEADS):
            vth = jnp.concatenate([vt_ref[hd * HEAD_DIM:(hd + 1) * HEAD_DIM, rows], ones_rows], axis=0)
            acc_ref[hd] = acc_ref[hd] * a_ref[hd] + _dot(vth, p_ref[hd])

    def attn_pair(cp, carry):
        c0 = 2 * cp
        logits(c0 + 1, 1)
        softmax_pv(c0, 0)
        logits(jnp.minimum(c0 + 2, last_c), 0)
        softmax_pv(c0 + 1, 1)
        return carry

    @pl.when(nk % 2 == 1)
    def _():
        bias_ref[nk] = jnp.full((tq, tq), NEG_BIG, F32)

    last_c = nk - 1 + nk % 2
    logits(0, 0)
    lax.fori_loop(0, (nk + 1) // 2, attn_pair, 0)

    ot = jnp.concatenate([acc_ref[hd, :HEAD_DIM] / acc_ref[hd, HEAD_DIM:HEAD_DIM + 1]
                          for hd in range(N_HEADS)], axis=0)
    ms = jnp.mean(ot * ot, axis=0, keepdims=True)
    ot = ot * lax.rsqrt(ms + NORM_EPS)
    o_ref[0] = (jnp.transpose(ot) * og_ref[...]).astype(BF16)


def _attention(q3, k3, vt, qi3, kk3, wit, og, tq):
    bsz, seq, w = k3.shape
    n_sel = min(TOPK_KEYS, seq // 4)
    nc = seq // tq
    assert nc % 2 == 0, "the count and softmax loops walk key chunks in pairs"
    qblk = lambda bb, i: (bb, i, 0)
    full = lambda bb, i: (bb, 0, 0)
    return pl.pallas_call(
        functools.partial(_attn_kernel, tq=tq, n_sel=n_sel, seq=seq),
        grid=(bsz, nc),
        in_specs=[
            pl.BlockSpec((1, tq, 2 * w), qblk),
            pl.BlockSpec((1, tq, 2 * w), qblk),
            pl.BlockSpec((SUBLANES, tq), lambda bb, i: (0, bb * nc + i)),
            pl.BlockSpec((1, seq, w), full),
            pl.BlockSpec((w, seq), lambda bb, i: (0, bb)),
            pl.BlockSpec((1, seq, LANES), full),
            pl.BlockSpec(og.shape, lambda bb, i: (0, 0)),
        ],
        out_specs=pl.BlockSpec((1, tq, w), qblk),
        out_shape=jax.ShapeDtypeStruct((bsz, seq, w), BF16),
        scratch_shapes=[
            pltpu.VMEM((nc, tq, tq), F32),
            pltpu.VMEM((nc, tq, tq), F32),
            pltpu.VMEM((N_HEADS, 1, tq), F32),
            pltpu.VMEM((N_HEADS, 1, tq), F32),
            pltpu.VMEM((N_HEADS, HEAD_DIM + PACK16, tq), F32),
            pltpu.VMEM((2, N_HEADS, tq, tq), F32),
            pltpu.VMEM((N_HEADS, tq, tq), BF16),
            pltpu.VMEM((N_HEADS, LANES, tq), BF16),
            pltpu.VMEM((2, N_HEADS, 1, tq), F32),
        ],
        compiler_params=_cparams(("arbitrary", "arbitrary")),
        name="dsa_attention",
    )(q3, qi3, wit, k3, vt, kk3, og)


def _outproj_kernel(an_ref, cn_ref, x_ref, g1_ref, sh_ref, sc_ref, woa_ref, woc_ref, wq_ref,
                    x1_ref, h2_ref, pq_ref):
    y = _dot(an_ref[...], woa_ref[...]) + _dot(cn_ref[...], woc_ref[...])
    x1 = x_ref[...] + g1_ref[0] * y
    x1_ref[...] = x1
    ms = jnp.mean(x1 * x1, axis=-1, keepdims=True)
    h2 = (x1 * lax.rsqrt(ms + NORM_EPS) * (1.0 + sc_ref[0]) + sh_ref[0]).astype(BF16)
    h2_ref[...] = h2
    pq_ref[...] = _dot(h2, wq_ref[...]).astype(BF16)


def _outproj(an, cn, x2, g1, sh2, sc2, woa, woc, wq, seq, tm):
    n, d = x2.shape
    per_b = seq // tm
    row = lambda i: (i, 0)
    const = lambda i: (0, 0)
    bat = lambda i: (i // per_b, 0, 0)
    return pl.pallas_call(
        _outproj_kernel,
        grid=(n // tm,),
        in_specs=[
            pl.BlockSpec((tm, an.shape[1]), row),
            pl.BlockSpec((tm, cn.shape[1]), row),
            pl.BlockSpec((tm, d), row),
            pl.BlockSpec((1, 1, d), bat),
            pl.BlockSpec((1, 1, d), bat),
            pl.BlockSpec((1, 1, d), bat),
            pl.BlockSpec(woa.shape, const),
            pl.BlockSpec(woc.shape, const),
            pl.BlockSpec(wq.shape, const),
        ],
        out_specs=[pl.BlockSpec((tm, d), row), pl.BlockSpec((tm, d), row),
                   pl.BlockSpec((tm, wq.shape[1]), row)],
        out_shape=[jax.ShapeDtypeStruct((n, d), F32), jax.ShapeDtypeStruct((n, d), BF16),
                   jax.ShapeDtypeStruct((n, wq.shape[1]), BF16)],
        compiler_params=_cparams(("arbitrary",)),
        name="outproj",
    )(an, cn, x2, g1, sh2, sc2, woa, woc, wq)


def _top_rows(s, rank=None, ids=None):
    return _top_rows_multi([(s, rank, ids)])[0]


def _top_rows_multi(problems):
    state = []
    for s, rank, ids in problems:
        if rank is None:
            rank = lax.broadcasted_iota(I32, s.shape, 0).astype(F32)
        state.append([s, rank, ids, [], []])
    for _ in range(PEER_TOPK):
        for st in state:
            s, rank, ids, vals, picks = st
            m = jnp.max(s, axis=0, keepdims=True)
            pos = jnp.min(jnp.where(s == m, rank, jnp.inf), axis=0, keepdims=True)
            hit = rank == pos
            vals.append(m)
            picks.append(pos if ids is None else jnp.max(jnp.where(hit, ids, -1.0), axis=0, keepdims=True))
            st[0] = jnp.where(hit, -jnp.inf, s)
    return [(jnp.concatenate(st[3], axis=0), jnp.concatenate(st[4], axis=0)) for st in state]


def _pair_candidates(v1, i1, v2, i2):
    tr = v1.shape[1]
    k = PEER_TOPK
    r8 = lax.broadcasted_iota(I32, (SUBLANES, tr), 0).astype(F32)
    rk = lax.broadcasted_iota(I32, (k, tr), 0).astype(F32)
    sums = [v1 + v2[0:1]]
    eids = [i1 * PEER_KEYS + i2[0:1]]
    order = [rk * k]
    for b in range(1, SUBLANES):
        ok = r8 < k // (b + 1)
        sums.append(jnp.where(ok, v1[0:SUBLANES] + v2[b:b + 1], -jnp.inf))
        eids.append(i1[0:SUBLANES] * PEER_KEYS + i2[b:b + 1])
        order.append(r8 * k + b)
    sums.append(v1[0:1] + v2[SUBLANES:k])
    eids.append(i1[0:1] * PEER_KEYS + i2[SUBLANES:k])
    order.append(r8 + SUBLANES)
    return jnp.concatenate(sums, axis=0), jnp.concatenate(eids, axis=0), jnp.concatenate(order, axis=0)


def _route_kernel(pq_ref, sk_ref, a_ref, b_ref, g_ref, e_scr, g_scr):
    h = pl.program_id(1)
    half = PEER_KEYS
    s1 = _nt_dot(sk_ref[0, 0], pq_ref[:, :half])
    s2 = _nt_dot(sk_ref[0, 1], pq_ref[:, half:])
    (v1, i1), (v2, i2) = _top_rows_multi([(s1, None, None), (s2, None, None)])
    cand, cid, order = _pair_candidates(v1, i1, v2, i2)
    ts, e = _top_rows(cand, order, cid)
    ex = jnp.exp(ts - ts[0:1])
    g = ex / jnp.sum(ex, axis=0, keepdims=True)
    rows = pl.ds(pl.multiple_of(h * PEER_TOPK, PEER_TOPK), PEER_TOPK)
    e_scr[rows, :] = e
    g_scr[rows, :] = g

    @pl.when(h == PEER_HEADS - 1)
    def _():
        e_all = e_scr[...]
        first = jnp.floor(e_all * (1.0 / PEER_KEYS))
        a_ref[...] = jnp.transpose(first)
        b_ref[...] = jnp.transpose(e_all - first * PEER_KEYS)
        g_ref[...] = jnp.transpose(g_scr[...])


def _route(pq, sk, tr):
    n = pq.shape[0]
    qd = 2 * PEER_KEYS
    out = jax.ShapeDtypeStruct((n, PEER_PAIRS), F32)
    oblk = pl.BlockSpec((tr, PEER_PAIRS), lambda i, h: (i, 0))
    return pl.pallas_call(
        _route_kernel,
        grid=(n // tr, PEER_HEADS),
        in_specs=[
            pl.BlockSpec((tr, qd), lambda i, h: (i, h)),
            pl.BlockSpec((1, 2, PEER_KEYS, PEER_KEYS), lambda i, h: (h, 0, 0, 0)),
        ],
        out_specs=[oblk, oblk, oblk],
        out_shape=[out, out, out],
        scratch_shapes=[pltpu.VMEM((PEER_PAIRS, tr), F32), pltpu.VMEM((PEER_PAIRS, tr), F32)],
        compiler_params=_cparams(("arbitrary", "arbitrary")),
        name="peer_route",
    )(pq, sk)


C_PITCH = PEER_KEYS + SUBLANES
C_UNROLL = 64


def _gelu(x):
    return 0.5 * x * (1.0 + lax.erf(x * (2.0 ** -0.5)))


def _peer_kernel(h2_ref, a_ref, b_ref, g_ref, u_ref, v_ref, x1_ref, g2_ref, o_ref, c_scr, acc_ref, *, tm, ec):
    j = pl.program_id(1)
    nj = pl.num_programs(1)
    k1 = ec // PEER_KEYS

    @pl.when(j == 0)
    def _():
        srow = lax.broadcasted_iota(I32, (PEER_KEYS, PEER_PAIRS), 0).astype(F32)

        def tokens(tb, carry):
            for u in range(C_UNROLL):
                t = tb * C_UNROLL + u
                a_row = a_ref[pl.ds(t, 1), :]
                b_row = b_ref[pl.ds(t, 1), :]
                g_row = g_ref[pl.ds(t, 1), :]
                at = jnp.where(srow == a_row, g_row, 0.0).astype(BF16)
                bt = jnp.where(srow == b_row, 1.0, 0.0).astype(BF16)
                ct = _nt_dot(at, bt)
                c_scr[pl.ds(pl.multiple_of(t * C_PITCH, SUBLANES), PEER_KEYS), :] = ct
            return carry

        lax.fori_loop(0, tm // C_UNROLL, tokens, 0)
        acc_ref[...] = jnp.zeros(acc_ref.shape, F32)

    act = _gelu(_dot(h2_ref[...], u_ref[...]))
    parts = []
    for ii in range(k1):
        cs = c_scr[pl.ds(j * k1 + ii, tm, stride=C_PITCH), :]
        parts.append((act[:, ii * PEER_KEYS:(ii + 1) * PEER_KEYS] * cs).astype(BF16))
    acc_ref[...] += _dot(jnp.concatenate(parts, axis=1), v_ref[...])

    @pl.when(j == nj - 1)
    def _():
        o_ref[...] = x1_ref[...] + g2_ref[0] * acc_ref[...]


def _peer(h2, a, b, g, ut, v, x1, g2, seq, tm, ec):
    n, d = x1.shape
    ne = v.shape[0]
    per_b = seq // tm
    row = lambda i, j: (i, 0)
    return pl.pallas_call(
        functools.partial(_peer_kernel, tm=tm, ec=ec),
        grid=(n // tm, ne // ec),
        in_specs=[
            pl.BlockSpec((tm, d), row),
            pl.BlockSpec((tm, PEER_PAIRS), row),
            pl.BlockSpec((tm, PEER_PAIRS), row),
            pl.BlockSpec((tm, PEER_PAIRS), row),
            pl.BlockSpec((d, ec), lambda i, j: (0, j)),
            pl.BlockSpec((ec, d), lambda i, j: (j, 0)),
            pl.BlockSpec((tm, d), row),
            pl.BlockSpec((1, 1, d), lambda i, j: (i // per_b, 0, 0)),
        ],
        out_specs=pl.BlockSpec((tm, d), row),
        out_shape=jax.ShapeDtypeStruct((n, d), F32),
        scratch_shapes=[pltpu.VMEM((tm * C_PITCH, PEER_KEYS), F32), pltpu.VMEM((tm, d), F32)],
        compiler_params=_cparams(("arbitrary", "arbitrary")),
        name="peer_experts",
    )(h2, a, b, g, ut, v, x1, g2)


def _rope_lane_tables(positions):
    half = ROPE_DIM // 2
    inv_freq = jnp.power(jnp.float32(ROPE_THETA), -jnp.arange(0, ROPE_DIM, 2, dtype=F32) / ROPE_DIM)
    ang = positions.astype(F32).reshape(-1, 1) * inv_freq
    cos, sin = jnp.cos(ang), jnp.sin(ang)
    j = jnp.arange(LANES) % HEAD_DIM
    cos_l, sin_l = cos[:, j % half], sin[:, j % half]
    rc = jnp.where(j < ROPE_DIM, cos_l, 1.0)
    rs1 = jnp.where(j < half, -sin_l, 0.0)
    rs2 = jnp.where((j >= half) & (j < ROPE_DIM), sin_l, 0.0)
    return rc, rs1, rs2


def _tile_size(total, want):
    t = min(want, total)
    assert total % t == 0
    return t


def kernel(x, c, positions, w_ada, b_ada, w_in, q_norm_g, k_norm_g, idx_k_ln_g, idx_k_ln_b,
           conv_dw, conv_b, conv_ln_g, conv_ln_b, out_g_attn, out_g_conv, w_out, peer_wq,
           peer_sub_keys, peer_u, peer_v):
    bsz, seq, d = x.shape
    depth = w_ada.shape[0]
    n = bsz * seq
    w = ATTN_WIDTH
    cw = d - w
    assert bsz <= SUBLANES and cw % LANES == 0

    tm = _tile_size(seq, 512)
    tq = _tile_size(seq, 256)
    tt = _tile_size(seq, 512)
    tr = _tile_size(seq, 1024)
    tp = _tile_size(seq, 256)
    ec = 2048

    c_pad = jnp.zeros((SUBLANES, d), F32).at[:bsz].set(c)
    mod = _ada(c_pad, w_ada, b_ada)[:, :bsz]
    rc, rs1, rs2 = _rope_lane_tables(positions)
    hd_id = jnp.arange(w) // HEAD_DIM
    gsum = (hd_id[:, None] == hd_id[None, :]).astype(BF16)

    o_q, o_k, o_v, o_qi, o_ki, o_wi = 0, w, 2 * w, 3 * w, 4 * w, 4 * w + IDX_DIM
    o_cu = o_wi + IDX_HEADS

    x2 = x.reshape(n, d)
    for l in range(depth):
        sh1, sc1, g1, sh2, sc2, g2 = [mod[l, :, i * d:(i + 1) * d].reshape(bsz, 1, d) for i in range(6)]
        wl = w_in[l]
        wa = wl[:, o_q:o_ki].astype(BF16)
        wki = wl[:, o_ki:o_wi]
        wk = jnp.concatenate([wki, wki], axis=1).astype(BF16)
        ww = jnp.pad(wl[:, o_wi:o_cu], ((0, 0), (0, LANES - IDX_HEADS))).astype(BF16)
        wc = wl[:, o_cu:].astype(BF16)
        tile_h = lambda g_: jnp.tile(g_, N_HEADS).reshape(1, w)
        dup = lambda g_: jnp.concatenate([g_, g_]).reshape(1, LANES)
        q, k, vt, qi, kk, wit, hg = _inproj(
            x2, sh1, sc1, wa, wk, ww, wc, tile_h(q_norm_g[l]), tile_h(k_norm_g[l]),
            dup(idx_k_ln_g[l]), dup(idx_k_ln_b[l]), rc, rs1, rs2, gsum, seq, tm)

        r3 = lambda t: t.reshape(bsz, seq, t.shape[-1])
        an = _attention(r3(q), r3(k), vt, r3(qi), r3(kk), wit, out_g_attn[l].reshape(1, w), tq)
        w_pad = jnp.pad(conv_dw[l], ((0, CONV_HALO - CONV_KERNEL), (0, 0)))
        cn = _conv(r3(hg), w_pad, conv_b[l].reshape(1, cw), conv_ln_g[l].reshape(1, cw),
                   conv_ln_b[l].reshape(1, cw), out_g_conv[l].reshape(1, cw), tt)

        wo = w_out[l].astype(BF16)
        x1, h2, pq = _outproj(an.reshape(n, w), cn.reshape(n, cw), x2, g1, sh2, sc2,
                              wo[:w], wo[w:], peer_wq[l].astype(BF16), seq, tm)
        a, b, g = _route(pq, peer_sub_keys[l].astype(BF16), tr)
        x2 = _peer(h2, a, b, g, jnp.transpose(peer_u[l]).astype(BF16), peer_v[l].astype(BF16), x1, g2,
                   seq, tp, ec)
    return x2.reshape(bsz, seq, d)
```
